```python
import functools
import jax, jax.numpy as jnp
from jax import lax
import numpy as np

D_MODEL = 1024
BATCH = 32
SEQ = 2048
DEPTH = 1
DEC_BATCH = 32
DEC_SEQ = 16
PAST_LEN = 2048

CHUNK = 64
LRU_WIDTH = D_MODEL // 2
ATTN_WIDTH = D_MODEL - LRU_WIDTH
HEAD_DIM = 64
N_HEADS = ATTN_WIDTH // HEAD_DIM
N_LRU_BLOCKS = 8
LRU_BLOCK = LRU_WIDTH // N_LRU_BLOCKS
CONV_WIDTH = 4
LRU_C = 8.0
LEFT_CHUNKS = 8
BAND = (LEFT_CHUNKS + 1) * CHUNK
MAX_REL = 128
D_FF = ((8 * D_MODEL // 3 + 255) // 256) * 256
IN_COLS = 2 * LRU_WIDTH + 3 * ATTN_WIDTH
EPS = 1e-6
CACHE_ROWS = min(LEFT_CHUNKS * CHUNK, PAST_LEN)

kernel_name = 'hymba_rglru_chunkattn_stream_step'


def rms_norm(x, g):
    xf = x.astype(jnp.float32)
    y = xf * lax.rsqrt(jnp.mean(xf * xf, axis=-1, keepdims=True) + EPS)
    return (y * g.astype(jnp.float32)).astype(x.dtype)


def causal_conv(u, buf, w, b):
    T = u.shape[1]
    up = jnp.concatenate([buf.astype(u.dtype), u], axis=1)
    y = b + up[:, 0:T] * w[0]
    for k in range(1, CONV_WIDTH):
        y = y + up[:, k:k + T] * w[k]
    return y, up[:, -(CONV_WIDTH - 1):]


def rg_lru(u, h0, w_a, b_a, w_x, b_x, lam):
    B, T, _ = u.shape
    ub = u.reshape(B, T, N_LRU_BLOCKS, LRU_BLOCK)
    r = jax.nn.sigmoid(jnp.einsum('btnc,ncd->btnd', ub, w_a).reshape(B, T, LRU_WIDTH) + b_a)
    i = jax.nn.sigmoid(jnp.einsum('btnc,ncd->btnd', ub, w_x).reshape(B, T, LRU_WIDTH) + b_x)
    log_a = (LRU_C * r.astype(jnp.float32)) * jax.nn.log_sigmoid(lam.astype(jnp.float32))
    a = jnp.exp(log_a)
    gain = jnp.sqrt(-jnp.expm1(2.0 * log_a))
    bterm = gain * (i * u).astype(jnp.float32)
    bterm = bterm.at[:, 0].add(a[:, 0] * h0.astype(jnp.float32))

    def combine(c1, c2):
        a1, b1 = c1
        a2, b2 = c2
        return a1 * a2, a2 * b1 + b2

    _, h = lax.associative_scan(combine, (a, bterm), axis=1)
    return h.astype(u.dtype), h[:, -1].astype(h0.dtype)


def attend_prompt(q, k, v, table):
    B, T, H, Dh = q.shape
    n_chunks = T // CHUNK
    pad = LEFT_CHUNKS * CHUNK
    kp = jnp.pad(k, ((0, 0), (pad, 0), (0, 0), (0, 0)))
    vp = jnp.pad(v, ((0, 0), (pad, 0), (0, 0), (0, 0)))
    qc = q.reshape(B, n_chunks, CHUNK, H, Dh).transpose(1, 0, 2, 3, 4)
    dist = jnp.arange(CHUNK)[:, None] + pad - jnp.arange(BAND)[None, :]
    bias = table[:, jnp.clip(dist, -MAX_REL, MAX_REL) + MAX_REL].astype(jnp.float32)
    scale = HEAD_DIM ** -0.5

    def one_chunk(args):
        c, qb = args
        kb = lax.dynamic_slice_in_dim(kp, c * CHUNK, BAND, axis=1)
        vb = lax.dynamic_slice_in_dim(vp, c * CHUNK, BAND, axis=1)
        s = jnp.einsum('bqhd,bkhd->bhqk', qb, kb).astype(jnp.float32) * scale + bias
        valid = jnp.arange(BAND) >= pad - c * CHUNK
        s = jnp.where(valid[None, None, None, :], s, -1e30)
        p = jax.nn.softmax(s, axis=-1).astype(vb.dtype)
        return jnp.einsum('bhqk,bkhd->bqhd', p, vb)

    o = lax.map(one_chunk, (jnp.arange(n_chunks), qc))
    return o.transpose(1, 0, 2, 3, 4).reshape(B, T, H * Dh)


def attend_sample(q, k, v, k_cache, v_cache, table):
    B, T, H, Dh = q.shape
    R = k_cache.shape[1]
    kk = jnp.concatenate([k_cache.astype(k.dtype), k], axis=1)
    vv = jnp.concatenate([v_cache.astype(v.dtype), v], axis=1)
    qpos = PAST_LEN + jnp.arange(T)
    kpos = jnp.concatenate([PAST_LEN - R + jnp.arange(R), qpos])
    dist = jnp.clip(qpos[:, None] - kpos[None, :], -MAX_REL, MAX_REL) + MAX_REL
    bias = table[:, dist].astype(jnp.float32)
    s = jnp.einsum('bqhd,bkhd->bhqk', q, kk).astype(jnp.float32) * (HEAD_DIM ** -0.5) + bias
    p = jax.nn.softmax(s, axis=-1).astype(vv.dtype)
    return jnp.einsum('bhqk,bkhd->bqhd', p, vv).reshape(B, T, H * Dh)


def layer(x, conv_buf, h0, attend, norm_mix, w_in, conv_w, conv_b, lru_wa, lru_ba, lru_wx, lru_bx,
          lru_lambda, norm_lru_out, norm_attn_out, w_out, norm_ffn, w_gate, w_up, w_down):
    B, T, _ = x.shape
    xn = rms_norm(x, norm_mix)
    proj = xn @ w_in
    splits = [LRU_WIDTH, 2 * LRU_WIDTH, 2 * LRU_WIDTH + ATTN_WIDTH, 2 * LRU_WIDTH + 2 * ATTN_WIDTH]
    u, g, q, k, v = jnp.split(proj, splits, axis=-1)
    uc, conv_new = causal_conv(u, conv_buf, conv_w, conv_b)
    h, h_last = rg_lru(uc, h0, lru_wa, lru_ba, lru_wx, lru_bx, lru_lambda)
    y_lru = jax.nn.gelu(g) * h
    q = q.reshape(B, T, N_HEADS, HEAD_DIM)
    k = k.reshape(B, T, N_HEADS, HEAD_DIM)
    v = v.reshape(B, T, N_HEADS, HEAD_DIM)
    y_att = attend(q, k, v)
    mix = jnp.concatenate([rms_norm(y_lru, norm_lru_out), rms_norm(y_att, norm_attn_out)], axis=-1) @ w_out
    x = x + mix
    xn = rms_norm(x, norm_ffn)
    x = x + (jax.nn.silu(xn @ w_gate) * (xn @ w_up)) @ w_down
    return x, conv_new, h_last, k, v


def setup_inputs(seed: int = 0) -> dict:
    key = jax.random.key(seed)
    ks = jax.random.split(key, 24)
    f32 = jnp.float32

    def nrm(k, shape, s):
        return s * jax.random.normal(k, shape, f32)

    u = jax.random.uniform(ks[14], (DEPTH, LRU_WIDTH), f32, 0.9, 0.999)
    s = u ** (1.0 / LRU_C)
    lam = jnp.log(s) - jnp.log1p(-s)
    return {
        'x_prompt': nrm(ks[0], (BATCH, SEQ, D_MODEL), 1.0),
        'x_sample': nrm(ks[1], (DEC_BATCH, DEC_SEQ, D_MODEL), 1.0),
        'state_conv': nrm(ks[2], (DEPTH, DEC_BATCH, CONV_WIDTH - 1, LRU_WIDTH), 1.0),
        'state_lru': nrm(ks[3], (DEPTH, DEC_BATCH, LRU_WIDTH), 0.5),
        'cache_k': nrm(ks[4], (DEPTH, DEC_BATCH, CACHE_ROWS, N_HEADS, HEAD_DIM), 1.0),
        'cache_v': nrm(ks[5], (DEPTH, DEC_BATCH, CACHE_ROWS, N_HEADS, HEAD_DIM), 1.0),
        'norm_mix': 1.0 + nrm(ks[6], (DEPTH, D_MODEL), 0.05),
        'w_in': nrm(ks[7], (DEPTH, D_MODEL, IN_COLS), D_MODEL ** -0.5),
        'conv_w': nrm(ks[8], (DEPTH, CONV_WIDTH, LRU_WIDTH), CONV_WIDTH ** -0.5),
        'conv_b': nrm(ks[9], (DEPTH, LRU_WIDTH), 0.02),
        'lru_wa': nrm(ks[10], (DEPTH, N_LRU_BLOCKS, LRU_BLOCK, LRU_BLOCK), LRU_BLOCK ** -0.5),
        'lru_ba': nrm(ks[11], (DEPTH, LRU_WIDTH), 0.02),
        'lru_wx': nrm(ks[12], (DEPTH, N_LRU_BLOCKS, LRU_BLOCK, LRU_BLOCK), LRU_BLOCK ** -0.5),
        'lru_bx': nrm(ks[13], (DEPTH, LRU_WIDTH), 0.02),
        'lru_lambda': lam,
        'rel_bias': nrm(ks[15], (DEPTH, N_HEADS, 2 * MAX_REL + 1), 0.1),
        'norm_lru_out': 1.0 + nrm(ks[16], (DEPTH, LRU_WIDTH), 0.05),
        'norm_attn_out': 1.0 + nrm(ks[17], (DEPTH, ATTN_WIDTH), 0.05),
        'w_out': nrm(ks[18], (DEPTH, D_MODEL, D_MODEL), D_MODEL ** -0.5),
        'norm_ffn': 1.0 + nrm(ks[19], (DEPTH, D_MODEL), 0.05),
        'w_gate': nrm(ks[20], (DEPTH, D_MODEL, D_FF), D_MODEL ** -0.5),
        'w_up': nrm(ks[21], (DEPTH, D_MODEL, D_FF), D_MODEL ** -0.5),
        'w_down': nrm(ks[22], (DEPTH, D_FF, D_MODEL), D_FF ** -0.5),
        'norm_final': 1.0 + nrm(ks[23], (D_MODEL,), 0.05),
    }


def reference(x_prompt, x_sample, state_conv, state_lru, cache_k, cache_v, norm_mix, w_in, conv_w,
              conv_b, lru_wa, lru_ba, lru_wx, lru_bx, lru_lambda, rel_bias, norm_lru_out,
              norm_attn_out, w_out, norm_ffn, w_gate, w_up, w_down, norm_final):
    xp, xs = x_prompt, x_sample
    B, T, _ = xp.shape
    keep = min(LEFT_CHUNKS * CHUNK, T)
    p_conv, p_lru, p_k, p_v = [], [], [], []
    s_conv, s_lru, s_k, s_v = [], [], [], []
    for l in range(DEPTH):
        params = (norm_mix[l], w_in[l], conv_w[l], conv_b[l], lru_wa[l], lru_ba[l], lru_wx[l],
                  lru_bx[l], lru_lambda[l], norm_lru_out[l], norm_attn_out[l], w_out[l],
                  norm_ffn[l], w_gate[l], w_up[l], w_down[l])
        conv0 = jnp.zeros((B, CONV_WIDTH - 1, LRU_WIDTH), xp.dtype)
        h0 = jnp.zeros((B, LRU_WIDTH), xp.dtype)
        att_p = functools.partial(attend_prompt, table=rel_bias[l])
        xp, cp, hp, kp, vp = layer(xp, conv0, h0, att_p, *params)
        p_conv.append(cp)
        p_lru.append(hp)
        p_k.append(kp[:, T - keep:])
        p_v.append(vp[:, T - keep:])
        att_s = functools.partial(attend_sample, k_cache=cache_k[l], v_cache=cache_v[l], table=rel_bias[l])
        xs, cs, hs, ksn, vsn = layer(xs, state_conv[l], state_lru[l], att_s, *params)
        s_conv.append(cs)
        s_lru.append(hs)
        s_k.append(ksn)
        s_v.append(vsn)
    y_prompt = rms_norm(xp, norm_final)
    y_sample = rms_norm(xs, norm_final)
    return (y_prompt, y_sample,
            jnp.stack(p_conv), jnp.stack(p_lru), jnp.stack(p_k), jnp.stack(p_v),
            jnp.stack(s_conv), jnp.stack(s_lru), jnp.stack(s_k), jnp.stack(s_v))
```

```python
import functools
import math

import jax
import jax.numpy as jnp
from jax import lax
from jax.experimental import pallas as pl
from jax.experimental.pallas import tpu as pltpu

F32 = jnp.float32
BF16 = jnp.bfloat16

EPS = 1e-6
CHUNK = 64
LEFT_CHUNKS = 8
PAD_ROWS = LEFT_CHUNKS * CHUNK
MAX_REL = 128
HEAD_DIM = 64
N_LRU_BLOCKS = 8
CONV_WIDTH = 4
LRU_C = 8.0
NEG_BIG = -1e30

ROW_TILE = 512
LRU_TIME_TILE = 512
Q_BLOCK = 256
SUBLANES = 8
MXU_DIM = 256
LANES = 128
VMEM_LIMIT = 56 * 1024 * 1024


def _params(sem):
    return pltpu.CompilerParams(dimension_semantics=sem, vmem_limit_bytes=VMEM_LIMIT)


def _const_spec(shape):
    nd = len(shape)
    return pl.BlockSpec(shape, lambda *_: (0,) * nd, pipeline_mode=pl.Buffered(1))


def _rms(x, gain):
    return x * lax.rsqrt(jnp.mean(x * x, axis=-1, keepdims=True) + EPS) * gain


def _sigmoid(x):
    return 0.5 * jnp.tanh(0.5 * x) + 0.5


def _gelu_tanh(x):
    c = math.sqrt(2.0 / math.pi)
    return 0.5 * x * (1.0 + jnp.tanh(c * (x + 0.044715 * (x * x * x))))


def _inproj_kernel(x_ref, gain_ref, w_ref, u_ref, gate_ref, q_ref, k_ref, v_ref, kt_ref, vt_ref,
                   *, lru_w, attn_w):
    xn = _rms(x_ref[...], gain_ref[...]).astype(BF16)
    proj = jnp.dot(xn, w_ref[...], preferred_element_type=F32)
    c0, c1, c2, c3 = lru_w, 2 * lru_w, 2 * lru_w + attn_w, 2 * lru_w + 2 * attn_w
    u_ref[...] = proj[:, :c0]
    gate_ref[...] = proj[:, c0:c1]
    q_ref[...] = (proj[:, c1:c2] * (HEAD_DIM ** -0.5)).astype(BF16)
    k = proj[:, c2:c3]
    v = proj[:, c3:]
    k_ref[...] = k.astype(BF16)
    v_ref[...] = v.astype(BF16)
    kt_ref[...] = k
    vt_ref[...] = v


def _in_projection(x2d, gain, w_bf16, tiles_per_tail, lru_w, attn_w):
    rows, d = x2d.shape
    tm = min(ROW_TILE, rows)
    assert rows % tm == 0
    n_tiles = rows // tm
    n_cols = w_bf16.shape[1]
    row_spec = lambda w: pl.BlockSpec((tm, w), lambda i: (i, 0))
    tail_spec = pl.BlockSpec((tm, attn_w), lambda i: (i // tiles_per_tail, 0))
    tail_rows = (n_tiles // tiles_per_tail) * tm
    return pl.pallas_call(
        functools.partial(_inproj_kernel, lru_w=lru_w, attn_w=attn_w),
        grid=(n_tiles,),
        in_specs=[row_spec(d), _const_spec((1, d)), _const_spec((d, n_cols))],
        out_specs=[row_spec(lru_w), row_spec(lru_w), row_spec(attn_w), row_spec(attn_w),
                   row_spec(attn_w), tail_spec, tail_spec],
        out_shape=[jax.ShapeDtypeStruct((rows, lru_w), F32), jax.ShapeDtypeStruct((rows, lru_w), F32),
                   jax.ShapeDtypeStruct((rows, attn_w), BF16), jax.ShapeDtypeStruct((rows, attn_w), BF16),
                   jax.ShapeDtypeStruct((rows, attn_w), BF16),
                   jax.ShapeDtypeStruct((tail_rows, attn_w), F32),
                   jax.ShapeDtypeStruct((tail_rows, attn_w), F32)],
        compiler_params=_params(("arbitrary",)),
        name="in_projection",
    )(x2d, gain, w_bf16)


def _lru_kernel(u_ref, gate_ref, conv0_ref, h0_ref, cw_ref, cb_ref, wbd_ref, ba_ref, bx_ref, lam_ref,
                y_ref, convn_ref, hlast_ref, ext_s, a_s, b_s, h_s, hc_s, *, tc):
    c = pl.program_id(1)
    hist = CONV_WIDTH - 1
    base = SUBLANES

    @pl.when(c == 0)
    def _():
        ext_s[base - hist:base, :] = conv0_ref[0]
        hc_s[...] = h0_ref[0]

    u = u_ref[0]
    ext_s[base:base + tc, :] = u
    w = cw_ref[...]
    uc = cb_ref[...] + u * w[hist:hist + 1]
    for kk in range(hist):
        uc = uc + ext_s[base - hist + kk:base - hist + kk + tc, :] * w[kk:kk + 1]
    tail = ext_s[base + tc - hist:base + tc, :]
    ext_s[base - hist:base, :] = tail
    convn_ref[0] = tail

    lam = lam_ref[...]
    log_sig_lam = jnp.minimum(lam, 0.0) - jnp.log1p(jnp.exp(-jnp.abs(lam)))
    ucb = uc.astype(BF16)
    half = wbd_ref.shape[1]
    for hf in range(wbd_ref.shape[0]):
        cols = slice(hf * half, (hf + 1) * half)
        pre = jnp.dot(ucb[:, cols], wbd_ref[hf], preferred_element_type=F32)
        r = _sigmoid(pre[:, :half] + ba_ref[:, cols])
        i = _sigmoid(pre[:, half:] + bx_ref[:, cols])
        a = jnp.exp((LRU_C * r) * log_sig_lam[:, cols])
        gain = jnp.sqrt(1.0 - a * a)
        a_s[:, cols] = a
        b_s[:, cols] = gain * (i * uc[:, cols])

    row = lax.broadcasted_iota(jnp.int32, (SUBLANES, a_s.shape[1]), 0)

    def group(gi, h_prev):
        r0 = pl.multiple_of(gi * SUBLANES, SUBLANES)
        a = a_s[pl.ds(r0, SUBLANES), :]
        b = b_s[pl.ds(r0, SUBLANES), :]
        sh = 1
        while sh < SUBLANES:
            keep = row >= sh
            a_sh = jnp.where(keep, pltpu.roll(a, sh, 0), 1.0)
            b_sh = jnp.where(keep, pltpu.roll(b, sh, 0), 0.0)
            b = a * b_sh + b
            a = a * a_sh
            sh *= 2
        h = a * h_prev + b
        h_s[pl.ds(r0, SUBLANES), :] = h
        return h[SUBLANES - 1:SUBLANES, :]

    h_last = lax.fori_loop(0, tc // SUBLANES, group, hc_s[...])
    hc_s[...] = h_last
    hlast_ref[0] = h_last
    y_ref[0] = _gelu_tanh(gate_ref[0]) * h_s[...]


def _lru_group(u, gate, conv0, h0, conv_w, conv_b, wbd, ba, bx, lam):
    b, t, w = u.shape
    tc = min(LRU_TIME_TILE, t)
    assert t % tc == 0 and tc % SUBLANES == 0 and tc >= CONV_WIDTH - 1
    hist = CONV_WIDTH - 1
    seq_spec = pl.BlockSpec((1, tc, w), lambda i, c: (i, c, 0))
    per_batch = lambda r: pl.BlockSpec((1, r, w), lambda i, c: (i, 0, 0))
    return pl.pallas_call(
        functools.partial(_lru_kernel, tc=tc),
        grid=(b, t // tc),
        in_specs=[seq_spec, seq_spec, per_batch(hist), per_batch(1),
                  _const_spec(conv_w.shape), _const_spec((1, w)), _const_spec(wbd.shape),
                  _const_spec((1, w)), _const_spec((1, w)), _const_spec((1, w))],
        out_specs=[seq_spec, per_batch(hist), per_batch(1)],
        out_shape=[jax.ShapeDtypeStruct((b, t, w), F32), jax.ShapeDtypeStruct((b, hist, w), F32),
                   jax.ShapeDtypeStruct((b, 1, w), F32)],
        scratch_shapes=[pltpu.VMEM((SUBLANES + tc, w), F32), pltpu.VMEM((tc, w), F32),
                        pltpu.VMEM((tc, w), F32), pltpu.VMEM((tc, w), F32), pltpu.VMEM((1, w), F32)],
        compiler_params=_params(("arbitrary", "arbitrary")),
        name="rg_lru",
    )(u, gate, conv0, h0.reshape(b, 1, w), conv_w, conv_b, wbd, ba, bx, lam)


def _attn_prompt_kernel(q_ref, k_ref, v_ref, bias_ref, o_ref, kpad, vpad, *, n_heads, band):
    j = pl.program_id(1)

    @pl.when(j == 0)
    def _():
        zeros = jnp.zeros((PAD_ROWS, kpad.shape[1]), BF16)
        kpad[:PAD_ROWS, :] = zeros
        vpad[:PAD_ROWS, :] = zeros
        kpad[PAD_ROWS:, :] = k_ref[0]
        vpad[PAD_ROWS:, :] = v_ref[0]

    start = pl.multiple_of(j * Q_BLOCK, Q_BLOCK)
    col = lax.broadcasted_iota(jnp.int32, (1, band), 1)
    valid = (start + col) >= PAD_ROWS
    for h in range(n_heads):
        hs = slice(h * HEAD_DIM, (h + 1) * HEAD_DIM)
        qh = q_ref[0, :, hs]
        kh = kpad[pl.ds(start, band), hs]
        s = lax.dot_general(qh, kh, (((1,), (1,)), ((), ())), preferred_element_type=F32)
        s = jnp.where(valid, s + bias_ref[h], NEG_BIG)
        m = jnp.max(s, axis=-1, keepdims=True)
        p = jnp.exp(s - m)
        l = jnp.sum(p, axis=-1, keepdims=True)
        vh = vpad[pl.ds(start, band), hs]
        o = jnp.dot(p.astype(BF16), vh, preferred_element_type=F32)
        o_ref[0, :, hs] = o / l


def _prompt_bias(table, band):
    i = jnp.arange(Q_BLOCK)[:, None]
    c = jnp.arange(band)[None, :]
    dist = i + PAD_ROWS - c
    chunk_gap = i // CHUNK + LEFT_CHUNKS - c // CHUNK
    in_band = (chunk_gap >= 0) & (chunk_gap <= LEFT_CHUNKS)
    bias = table[:, jnp.clip(dist, -MAX_REL, MAX_REL) + MAX_REL].astype(F32)
    return jnp.where(in_band[None], bias, NEG_BIG)


def _attend_prompt(q, k, v, table):
    b, t, w = q.shape
    n_heads = w // HEAD_DIM
    assert t % Q_BLOCK == 0 and Q_BLOCK % CHUNK == 0
    band = PAD_ROWS + Q_BLOCK
    bias = _prompt_bias(table, band)
    full = pl.BlockSpec((1, t, w), lambda i, j: (i, 0, 0))
    blk = pl.BlockSpec((1, Q_BLOCK, w), lambda i, j: (i, j, 0))
    return pl.pallas_call(
        functools.partial(_attn_prompt_kernel, n_heads=n_heads, band=band),
        grid=(b, t // Q_BLOCK),
        in_specs=[blk, full, full, _const_spec(bias.shape)],
        out_specs=blk,
        out_shape=jax.ShapeDtypeStruct((b, t, w), F32),
        scratch_shapes=[pltpu.VMEM((PAD_ROWS + t, w), BF16), pltpu.VMEM((PAD_ROWS + t, w), BF16)],
        compiler_params=_params(("arbitrary", "arbitrary")),
        name="attn_prompt",
    )(q, k, v, bias)


def _attn_sample_kernel(q_ref, k_ref, v_ref, ck_ref, cv_ref, bc_ref, bn_ref, o_ref, *, n_heads):
    nt = (((1,), (1,)), ((), ()))
    for h in range(n_heads):
        hs = slice(h * HEAD_DIM, (h + 1) * HEAD_DIM)
        qh = q_ref[0, :, hs]
        s_c = lax.dot_general(qh, ck_ref[0, :, hs].astype(BF16), nt, preferred_element_type=F32) + bc_ref[h]
        s_n = lax.dot_general(qh, k_ref[0, :, hs], nt, preferred_element_type=F32) + bn_ref[h]
        m = jnp.maximum(jnp.max(s_c, axis=-1, keepdims=True), jnp.max(s_n, axis=-1, keepdims=True))
        p_c = jnp.exp(s_c - m)
        p_n = jnp.exp(s_n - m)
        l = jnp.sum(p_c, axis=-1, keepdims=True) + jnp.sum(p_n, axis=-1, keepdims=True)
        o = jnp.dot(p_c.astype(BF16), cv_ref[0, :, hs].astype(BF16), preferred_element_type=F32)
        o = o + jnp.dot(p_n.astype(BF16), v_ref[0, :, hs], preferred_element_type=F32)
        o_ref[0, :, hs] = o / l


def _attend_sample(q, k, v, k_cache, v_cache, table):
    b, t, w = q.shape
    n_heads = w // HEAD_DIM
    r = k_cache.shape[1]
    dist_c = jnp.arange(t)[:, None] + r - jnp.arange(r)[None, :]
    dist_n = jnp.arange(t)[:, None] - jnp.arange(t)[None, :]
    bias_c = table[:, jnp.clip(dist_c, -MAX_REL, MAX_REL) + MAX_REL].astype(F32)
    bias_n = table[:, jnp.clip(dist_n, -MAX_REL, MAX_REL) + MAX_REL].astype(F32)
    new = pl.BlockSpec((1, t, w), lambda i: (i, 0, 0))
    old = pl.BlockSpec((1, r, w), lambda i: (i, 0, 0))
    return pl.pallas_call(
        functools.partial(_attn_sample_kernel, n_heads=n_heads),
        grid=(b,),
        in_specs=[new, new, new, old, old, _const_spec(bias_c.shape), _const_spec(bias_n.shape)],
        out_specs=new,
        out_shape=jax.ShapeDtypeStruct((b, t, w), F32),
        compiler_params=_params(("arbitrary",)),
        name="attn_sample",
    )(q, k, v, k_cache.reshape(b, r, w), v_cache.reshape(b, r, w), bias_c, bias_n)


def _out_ffn_kernel(x_ref, yl_ref, ya_ref, gl_ref, ga_ref, wo_ref, gf_ref, wg_ref, wu_ref, wd_ref,
                    gfin_ref, o_ref, *, ff_tile, final_norm):
    lru_w = yl_ref.shape[1]
    nl = _rms(yl_ref[...], gl_ref[...]).astype(BF16)
    na = _rms(ya_ref[...], ga_ref[...]).astype(BF16)
    mix = jnp.dot(nl, wo_ref[:lru_w, :], preferred_element_type=F32)
    mix = mix + jnp.dot(na, wo_ref[lru_w:, :], preferred_element_type=F32)
    x1 = x_ref[...] + mix
    xn = _rms(x1, gf_ref[...]).astype(BF16)
    acc = x1
    for c0 in range(0, wg_ref.shape[1], ff_tile):
        cols = slice(c0, c0 + ff_tile)
        g = jnp.dot(xn, wg_ref[:, cols], preferred_element_type=F32)
        up = jnp.dot(xn, wu_ref[:, cols], preferred_element_type=F32)
        hmid = (g * _sigmoid(g) * up).astype(BF16)
        acc = acc + jnp.dot(hmid, wd_ref[cols, :], preferred_element_type=F32)
    o_ref[...] = _rms(acc, gfin_ref[...]) if final_norm else acc


def _ffn_tile(d_ff):
    best = LANES
    for n in range(1, d_ff // LANES + 1):
        tile = n * LANES
        if d_ff % tile == 0 and tile <= 1536:
            best = tile
    return best


def _out_ffn(x2d, y_lru, y_att, gl, ga, wo, gf, wg, wu, wd, gfin, final_norm):
    rows, d = x2d.shape
    tm = min(ROW_TILE, rows)
    assert rows % tm == 0
    lru_w, attn_w, d_ff = y_lru.shape[1], y_att.shape[1], wg.shape[1]
    row_spec = lambda w: pl.BlockSpec((tm, w), lambda i: (i, 0))
    return pl.pallas_call(
        functools.partial(_out_ffn_kernel, ff_tile=_ffn_tile(d_ff), final_norm=final_norm),
        grid=(rows // tm,),
        in_specs=[row_spec(d), row_spec(lru_w), row_spec(attn_w), _const_spec((1, lru_w)),
                  _const_spec((1, attn_w)), _const_spec(wo.shape), _const_spec((1, d)),
                  _const_spec(wg.shape), _const_spec(wu.shape), _const_spec(wd.shape), _const_spec((1, d))],
        out_specs=row_spec(d),
        out_shape=jax.ShapeDtypeStruct((rows, d), F32),
        compiler_params=_params(("arbitrary",)),
        name="out_ffn",
    )(x2d, y_lru, y_att, gl, ga, wo, gf, wg, wu, wd, gfin)


def _block_diag_gates(wa, wx):
    n, c, _ = wa.shape
    per = MXU_DIM // c
    assert n % per == 0
    eye = jnp.eye(per, dtype=wa.dtype)

    def bd(w):
        w = w.reshape(n // per, per, c, c)
        return (eye[None, :, None, :, None] * w[:, :, :, None, :]).reshape(n // per, per * c, per * c)

    return jnp.concatenate([bd(wa), bd(wx)], axis=-1).astype(BF16)


def _layer(x, conv0, h0, attend, tail_rows, final_gain, p):
    b, t, d = x.shape
    lru_w = p["conv_w"].shape[1]
    attn_w = p["norm_attn_out"].shape[1]
    rows = b * t
    tm = min(ROW_TILE, rows)
    assert tail_rows == tm or tail_rows == t, "key/value tail must be one row tile"
    tiles_per_tail = 1 if tail_rows == t and t <= tm else t // tm
    x2d = x.reshape(rows, d)
    u, gate, q, k, v, k_tail, v_tail = _in_projection(x2d, p["norm_mix"], p["w_in"], tiles_per_tail, lru_w, attn_w)
    y_lru, conv_new, h_last = _lru_group(u.reshape(b, t, lru_w), gate.reshape(b, t, lru_w), conv0, h0,
                                         p["conv_w"], p["conv_b"], p["wbd"], p["lru_ba"], p["lru_bx"],
                                         p["lru_lambda"])
    y_att = attend(q.reshape(b, t, attn_w), k.reshape(b, t, attn_w), v.reshape(b, t, attn_w))
    final_norm = final_gain is not None
    gfin = final_gain if final_norm else p["norm_ffn"]
    y = _out_ffn(x2d, y_lru.reshape(rows, lru_w), y_att.reshape(rows, attn_w), p["norm_lru_out"],
                 p["norm_attn_out"], p["w_out"], p["norm_ffn"], p["w_gate"], p["w_up"], p["w_down"],
                 gfin, final_norm)
    n_heads = attn_w // HEAD_DIM
    k_tail = k_tail.reshape(b, tail_rows, n_heads, HEAD_DIM)
    v_tail = v_tail.reshape(b, tail_rows, n_heads, HEAD_DIM)
    return y.reshape(b, t, d), conv_new, h_last.reshape(b, lru_w), k_tail, v_tail


def kernel(x_prompt, x_sample, state_conv, state_lru, cache_k, cache_v, norm_mix, w_in, conv_w, conv_b, lru_wa, lru_ba, lru_wx, lru_bx, lru_lambda, rel_bias, norm_lru_out, norm_attn_out, w_out, norm_ffn, w_gate, w_up, w_down, norm_final):
    depth = w_in.shape[0]
    xp, xs = x_prompt, x_sample
    bp, tp, d = xp.shape
    lru_w = conv_w.shape[-1]
    keep = min(PAD_ROWS, tp)
    row = lambda a: a.reshape(1, -1)
    outs = [[] for _ in range(8)]
    for l in range(depth):
        p = dict(
            norm_mix=row(norm_mix[l]), w_in=w_in[l].astype(BF16), conv_w=conv_w[l], conv_b=row(conv_b[l]),
            wbd=_block_diag_gates(lru_wa[l], lru_wx[l]), lru_ba=row(lru_ba[l]), lru_bx=row(lru_bx[l]),
            lru_lambda=row(lru_lambda[l]), norm_lru_out=row(norm_lru_out[l]),
            norm_attn_out=row(norm_attn_out[l]), w_out=w_out[l].astype(BF16), norm_ffn=row(norm_ffn[l]),
            w_gate=w_gate[l].astype(BF16), w_up=w_up[l].astype(BF16), w_down=w_down[l].astype(BF16))
        final_gain = row(norm_final) if l == depth - 1 else None
        conv0 = jnp.zeros((bp, CONV_WIDTH - 1, lru_w), xp.dtype)
        h0 = jnp.zeros((bp, lru_w), xp.dtype)
        att_p = functools.partial(_attend_prompt, table=rel_bias[l])
        xp, cp, hp, kp, vp = _layer(xp, conv0, h0, att_p, keep, final_gain, p)
        att_s = functools.partial(_attend_sample, k_cache=cache_k[l], v_cache=cache_v[l], table=rel_bias[l])
        xs, cs, hs, ksn, vsn = _layer(xs, state_conv[l], state_lru[l], att_s, xs.shape[1], final_gain, p)
        for lst, val in zip(outs, (cp, hp, kp, vp, cs, hs, ksn, vsn)):
            lst.append(val)
    return (xp, xs) + tuple(jnp.stack(o) for o in outs)
```

```python
import functools
import math

import jax
import jax.numpy as jnp
import numpy as np
from jax import lax
from jax.experimental import pallas as pl
from jax.experimental.pallas import tpu as pltpu

F32 = jnp.float32
BF16 = jnp.bfloat16

EPS = 1e-6
CHUNK = 64
LEFT_CHUNKS = 8
PAD_ROWS = LEFT_CHUNKS * CHUNK
MAX_REL = 128
HEAD_DIM = 64
N_LRU_BLOCKS = 8
CONV_WIDTH = 4
LRU_C = 8.0
NEG_BIG = -1e30

ROW_TILE = 512
LRU_TIME_TILE = 512
Q_BLOCK = 256
SUBLANES = 8
MXU_DIM = 256
LANES = 128
VMEM_LIMIT = 56 * 1024 * 1024


def _params(sem):
    return pltpu.CompilerParams(dimension_semantics=sem, vmem_limit_bytes=VMEM_LIMIT)


def _const_spec(shape):
    nd = len(shape)
    return pl.BlockSpec(shape, lambda *_: (0,) * nd, pipeline_mode=pl.Buffered(1))


def _rms(x, gain):
    return x * lax.rsqrt(jnp.mean(x * x, axis=-1, keepdims=True) + EPS) * gain


def _sigmoid(x):
    return 0.5 * jnp.tanh(0.5 * x) + 0.5


def _gelu_tanh(x):
    c = math.sqrt(2.0 / math.pi)
    return 0.5 * x * (1.0 + jnp.tanh(c * (x + 0.044715 * (x * x * x))))


def _inproj_kernel(x_ref, gain_ref, w_ref, u_ref, gate_ref, q_ref, k_ref, v_ref, kt_ref, vt_ref,
                   *, lru_w, attn_w, tiles_per_tail):
    xn = _rms(x_ref[...], gain_ref[...]).astype(BF16)
    proj = jnp.dot(xn, w_ref[...], preferred_element_type=F32)
    c0, c1, c2, c3 = lru_w, 2 * lru_w, 2 * lru_w + attn_w, 2 * lru_w + 2 * attn_w
    u_ref[...] = proj[:, :c0]
    gate_ref[...] = proj[:, c0:c1]
    q_ref[...] = (proj[:, c1:c2] * (HEAD_DIM ** -0.5)).astype(BF16)
    k = proj[:, c2:c3]
    v = proj[:, c3:]
    k_ref[...] = k.astype(BF16)
    v_ref[...] = v.astype(BF16)
    tm = k.shape[0]
    n_heads = attn_w // HEAD_DIM

    @pl.when(pl.program_id(0) % tiles_per_tail == tiles_per_tail - 1)
    def _():
        for h in range(n_heads):
            hs = slice(h * HEAD_DIM, (h + 1) * HEAD_DIM)
            kt_ref[pl.ds(h, tm, stride=n_heads), :] = k[:, hs]
            vt_ref[pl.ds(h, tm, stride=n_heads), :] = v[:, hs]


def _in_projection(x2d, gain, w_bf16, tiles_per_tail, lru_w, attn_w):
    rows, d = x2d.shape
    tm = min(ROW_TILE, rows)
    assert rows % tm == 0
    n_tiles = rows // tm
    n_cols = w_bf16.shape[1]
    row_spec = lambda w: pl.BlockSpec((tm, w), lambda i: (i, 0))
    n_heads = attn_w // HEAD_DIM
    tail_spec = pl.BlockSpec((tm * n_heads, HEAD_DIM), lambda i: (i // tiles_per_tail, 0))
    tail_rows = (n_tiles // tiles_per_tail) * tm * n_heads
    return pl.pallas_call(
        functools.partial(_inproj_kernel, lru_w=lru_w, attn_w=attn_w, tiles_per_tail=tiles_per_tail),
        grid=(n_tiles,),
        in_specs=[row_spec(d), _const_spec((1, d)), _const_spec((d, n_cols))],
        out_specs=[row_spec(lru_w), row_spec(lru_w), row_spec(attn_w), row_spec(attn_w),
                   row_spec(attn_w), tail_spec, tail_spec],
        out_shape=[jax.ShapeDtypeStruct((rows, lru_w), F32), jax.ShapeDtypeStruct((rows, lru_w), F32),
                   jax.ShapeDtypeStruct((rows, attn_w), BF16), jax.ShapeDtypeStruct((rows, attn_w), BF16),
                   jax.ShapeDtypeStruct((rows, attn_w), BF16),
                   jax.ShapeDtypeStruct((tail_rows, HEAD_DIM), F32),
                   jax.ShapeDtypeStruct((tail_rows, HEAD_DIM), F32)],
        compiler_params=_params(("arbitrary",)),
        name="in_projection",
    )(x2d, gain, w_bf16)


def _lru_kernel(u_ref, gate_ref, conv0_ref, h0_ref, cw_ref, cb_ref, wbd_ref, ba_ref, bx_ref, lam_ref,
                y_ref, convn_ref, hlast_ref, ext_s, a_s, b_s, h_s, hc_s, *, tc):
    c = pl.program_id(1)
    hist = CONV_WIDTH - 1
    base = SUBLANES

    @pl.when(c == 0)
    def _():
        ext_s[base - hist:base, :] = conv0_ref[0]
        hc_s[...] = h0_ref[0]

    u = u_ref[0]
    ext_s[base:base + tc, :] = u
    w = cw_ref[...]
    uc = cb_ref[...] + u * w[hist:hist + 1]
    for kk in range(hist):
        uc = uc + ext_s[base - hist + kk:base - hist + kk + tc, :] * w[kk:kk + 1]
    tail = ext_s[base + tc - hist:base + tc, :]
    ext_s[base - hist:base, :] = tail
    convn_ref[0] = tail

    lam = lam_ref[...]
    log_sig_lam = jnp.minimum(lam, 0.0) - jnp.log1p(jnp.exp(-jnp.abs(lam)))
    ucb = uc.astype(BF16)
    half = wbd_ref.shape[1]
    for hf in range(wbd_ref.shape[0]):
        cols = slice(hf * half, (hf + 1) * half)
        pre = jnp.dot(ucb[:, cols], wbd_ref[hf], preferred_element_type=F32)
        r = _sigmoid(pre[:, :half] + ba_ref[:, cols])
        i = _sigmoid(pre[:, half:] + bx_ref[:, cols])
        a = jnp.exp((LRU_C * r) * log_sig_lam[:, cols])
        gain = jnp.sqrt(1.0 - a * a)
        a_s[:, cols] = a
        b_s[:, cols] = gain * (i * uc[:, cols])

    row = lax.broadcasted_iota(jnp.int32, (SUBLANES, a_s.shape[1]), 0)

    def group(gi, h_prev):
        r0 = pl.multiple_of(gi * SUBLANES, SUBLANES)
        a = a_s[pl.ds(r0, SUBLANES), :]
        b = b_s[pl.ds(r0, SUBLANES), :]
        sh = 1
        while sh < SUBLANES:
            keep = row >= sh
            a_sh = jnp.where(keep, pltpu.roll(a, sh, 0), 1.0)
            b_sh = jnp.where(keep, pltpu.roll(b, sh, 0), 0.0)
            b = a * b_sh + b
            a = a * a_sh
            sh *= 2
        h = a * h_prev + b
        h_s[pl.ds(r0, SUBLANES), :] = h
        return h[SUBLANES - 1:SUBLANES, :]

    h_last = lax.fori_loop(0, tc // SUBLANES, group, hc_s[...])
    hc_s[...] = h_last
    hlast_ref[0] = h_last
    y_ref[0] = _gelu_tanh(gate_ref[0]) * h_s[...]


def _lru_group(u, gate, conv0, h0, conv_w, conv_b, wbd, ba, bx, lam):
    b, t, w = u.shape
    tc = min(LRU_TIME_TILE, t)
    assert t % tc == 0 and tc % SUBLANES == 0 and tc >= CONV_WIDTH - 1
    hist = CONV_WIDTH - 1
    seq_spec = pl.BlockSpec((1, tc, w), lambda i, c: (i, c, 0))
    per_batch = lambda r: pl.BlockSpec((1, r, w), lambda i, c: (i, 0, 0))
    return pl.pallas_call(
        functools.partial(_lru_kernel, tc=tc),
        grid=(b, t // tc),
        in_specs=[seq_spec, seq_spec, per_batch(hist), per_batch(1),
                  _const_spec(conv_w.shape), _const_spec((1, w)), _const_spec(wbd.shape),
                  _const_spec((1, w)), _const_spec((1, w)), _const_spec((1, w))],
        out_specs=[seq_spec, per_batch(hist), per_batch(1)],
        out_shape=[jax.ShapeDtypeStruct((b, t, w), F32), jax.ShapeDtypeStruct((b, hist, w), F32),
                   jax.ShapeDtypeStruct((b, 1, w), F32)],
        scratch_shapes=[pltpu.VMEM((SUBLANES + tc, w), F32), pltpu.VMEM((tc, w), F32),
                        pltpu.VMEM((tc, w), F32), pltpu.VMEM((tc, w), F32), pltpu.VMEM((1, w), F32)],
        compiler_params=_params(("arbitrary", "arbitrary")),
        name="rg_lru",
    )(u, gate, conv0, h0.reshape(b, 1, w), conv_w, conv_b, wbd, ba, bx, lam)


def _attn_prompt_kernel(q_ref, k_ref, v_ref, bias_ref, o_ref, kpad, vpad, *, n_heads, band):
    j = pl.program_id(1)

    @pl.when(j == 0)
    def _():
        zeros = jnp.zeros((PAD_ROWS, kpad.shape[1]), BF16)
        kpad[:PAD_ROWS, :] = zeros
        vpad[:PAD_ROWS, :] = zeros
        kpad[PAD_ROWS:, :] = k_ref[0]
        vpad[PAD_ROWS:, :] = v_ref[0]

    start = pl.multiple_of(j * Q_BLOCK, Q_BLOCK)
    col = lax.broadcasted_iota(jnp.int32, (1, band), 1)
    valid = (start + col) >= PAD_ROWS
    for h in range(n_heads):
        hs = slice(h * HEAD_DIM, (h + 1) * HEAD_DIM)
        qh = q_ref[0, :, hs]
        kh = kpad[pl.ds(start, band), hs]
        s = lax.dot_general(qh, kh, (((1,), (1,)), ((), ())), preferred_element_type=F32)
        s = jnp.where(valid, s + bias_ref[h], NEG_BIG)
        m = jnp.max(s, axis=-1, keepdims=True)
        p = jnp.exp(s - m)
        l = jnp.sum(p, axis=-1, keepdims=True)
        vh = vpad[pl.ds(start, band), hs]
        o = jnp.dot(p.astype(BF16), vh, preferred_element_type=F32)
        o_ref[0, :, hs] = o / l


def _rel_bias_matrix(table, rows, cols, dist00):
    n = rows + cols - 1
    dist = np.arange(n) - (cols - 1) + dist00
    g = table[:, np.clip(dist, -MAX_REL, MAX_REL) + MAX_REL].astype(F32)
    period = rows + cols
    g = jnp.pad(g, ((0, 0), (0, period - n)))
    flat = jnp.tile(g, (1, rows + 1))[:, :rows * (period + 1)]
    hankel = flat.reshape(-1, rows, period + 1)[:, :, :cols]
    return hankel[:, :, ::-1]


def _prompt_bias(table, band):
    i = np.arange(Q_BLOCK)[:, None]
    c = np.arange(band)[None, :]
    chunk_gap = i // CHUNK + LEFT_CHUNKS - c // CHUNK
    in_band = (chunk_gap >= 0) & (chunk_gap <= LEFT_CHUNKS)
    bias = _rel_bias_matrix(table, Q_BLOCK, band, PAD_ROWS)
    return jnp.where(in_band[None], bias, NEG_BIG)


def _attend_prompt(q, k, v, table):
    b, t, w = q.shape
    n_heads = w // HEAD_DIM
    assert t % Q_BLOCK == 0 and Q_BLOCK % CHUNK == 0
    band = PAD_ROWS + Q_BLOCK
    bias = _prompt_bias(table, band)
    full = pl.BlockSpec((1, t, w), lambda i, j: (i, 0, 0))
    blk = pl.BlockSpec((1, Q_BLOCK, w), lambda i, j: (i, j, 0))
    return pl.pallas_call(
        functools.partial(_attn_prompt_kernel, n_heads=n_heads, band=band),
        grid=(b, t // Q_BLOCK),
        in_specs=[blk, full, full, _const_spec(bias.shape)],
        out_specs=blk,
        out_shape=jax.ShapeDtypeStruct((b, t, w), F32),
        scratch_shapes=[pltpu.VMEM((PAD_ROWS + t, w), BF16), pltpu.VMEM((PAD_ROWS + t, w), BF16)],
        compiler_params=_params(("arbitrary", "arbitrary")),
        name="attn_prompt",
    )(q, k, v, bias)


def _attn_sample_kernel(q_ref, k_ref, v_ref, ck_ref, cv_ref, bc_ref, bn_ref, o_ref, *, n_heads):
    nt = (((1,), (1,)), ((), ()))
    r = ck_ref.shape[0] // n_heads
    for h in range(n_heads):
        hs = slice(h * HEAD_DIM, (h + 1) * HEAD_DIM)
        qh = q_ref[0, :, hs]
        kc = ck_ref[pl.ds(h, r, stride=n_heads), :].astype(BF16)
        vc = cv_ref[pl.ds(h, r, stride=n_heads), :].astype(BF16)
        s_c = lax.dot_general(qh, kc, nt, preferred_element_type=F32) + bc_ref[h]
        s_n = lax.dot_general(qh, k_ref[0, :, hs], nt, preferred_element_type=F32) + bn_ref[h]
        m = jnp.maximum(jnp.max(s_c, axis=-1, keepdims=True), jnp.max(s_n, axis=-1, keepdims=True))
        p_c = jnp.exp(s_c - m)
        p_n = jnp.exp(s_n - m)
        l = jnp.sum(p_c, axis=-1, keepdims=True) + jnp.sum(p_n, axis=-1, keepdims=True)
        o = jnp.dot(p_c.astype(BF16), vc, preferred_element_type=F32)
        o = o + jnp.dot(p_n.astype(BF16), v_ref[0, :, hs], preferred_element_type=F32)
        o_ref[0, :, hs] = o / l


def _attend_sample(q, k, v, k_cache, v_cache, table):
    b, t, w = q.shape
    n_heads = w // HEAD_DIM
    r = k_cache.shape[1]
    bias_c = _rel_bias_matrix(table, t, r, r)
    bias_n = _rel_bias_matrix(table, t, t, 0)
    new = pl.BlockSpec((1, t, w), lambda i: (i, 0, 0))
    old = pl.BlockSpec((r * n_heads, HEAD_DIM), lambda i: (i, 0))
    return pl.pallas_call(
        functools.partial(_attn_sample_kernel, n_heads=n_heads),
        grid=(b,),
        in_specs=[new, new, new, old, old, _const_spec(bias_c.shape), _const_spec(bias_n.shape)],
        out_specs=new,
        out_shape=jax.ShapeDtypeStruct((b, t, w), F32),
        compiler_params=_params(("arbitrary",)),
        name="attn_sample",
    )(q, k, v, k_cache.reshape(b * r * n_heads, HEAD_DIM), v_cache.reshape(b * r * n_heads, HEAD_DIM),
      bias_c, bias_n)


def _out_ffn_kernel(x_ref, yl_ref, ya_ref, gl_ref, ga_ref, wo_ref, gf_ref, wg_ref, wu_ref, wd_ref,
                    gfin_ref, o_ref, *, ff_tile, final_norm):
    lru_w = yl_ref.shape[1]
    nl = _rms(yl_ref[...], gl_ref[...]).astype(BF16)
    na = _rms(ya_ref[...], ga_ref[...]).astype(BF16)
    mix = jnp.dot(nl, wo_ref[:lru_w, :], preferred_element_type=F32)
    mix = mix + jnp.dot(na, wo_ref[lru_w:, :], preferred_element_type=F32)
    x1 = x_ref[...] + mix
    xn = _rms(x1, gf_ref[...]).astype(BF16)
    acc = x1
    for c0 in range(0, wg_ref.shape[1], ff_tile):
        cols = slice(c0, c0 + ff_tile)
        g = jnp.dot(xn, wg_ref[:, cols], preferred_element_type=F32)
        up = jnp.dot(xn, wu_ref[:, cols], preferred_element_type=F32)
        hmid = (g * _sigmoid(g) * up).astype(BF16)
        acc = acc + jnp.dot(hmid, wd_ref[cols, :], preferred_element_type=F32)
    o_ref[...] = _rms(acc, gfin_ref[...]) if final_norm else acc


def _ffn_tile(d_ff):
    best = LANES
    for n in range(1, d_ff // LANES + 1):
        tile = n * LANES
        if d_ff % tile == 0 and tile <= 1536:
            best = tile
    return best


def _out_ffn(x2d, y_lru, y_att, gl, ga, wo, gf, wg, wu, wd, gfin, final_norm):
    rows, d = x2d.shape
    tm = min(ROW_TILE, rows)
    assert rows % tm == 0
    lru_w, attn_w, d_ff = y_lru.shape[1], y_att.shape[1], wg.shape[1]
    row_spec = lambda w: pl.BlockSpec((tm, w), lambda i: (i, 0))
    return pl.pallas_call(
        functools.partial(_out_ffn_kernel, ff_tile=_ffn_tile(d_ff), final_norm=final_norm),
        grid=(rows // tm,),
        in_specs=[row_spec(d), row_spec(lru_w), row_spec(attn_w), _const_spec((1, lru_w)),
                  _const_spec((1, attn_w)), _const_spec(wo.shape), _const_spec((1, d)),
                  _const_spec(wg.shape), _const_spec(wu.shape), _const_spec(wd.shape), _const_spec((1, d))],
        out_specs=row_spec(d),
        out_shape=jax.ShapeDtypeStruct((rows, d), F32),
        compiler_params=_params(("arbitrary",)),
        name="out_ffn",
    )(x2d, y_lru, y_att, gl, ga, wo, gf, wg, wu, wd, gfin)


def _block_diag_gates(wa, wx):
    n, c, _ = wa.shape
    per = MXU_DIM // c
    assert n % per == 0
    eye = jnp.eye(per, dtype=wa.dtype)

    def bd(w):
        w = w.reshape(n // per, per, c, c)
        return (eye[None, :, None, :, None] * w[:, :, :, None, :]).reshape(n // per, per * c, per * c)

    return jnp.concatenate([bd(wa), bd(wx)], axis=-1).astype(BF16)


def _layer(x, conv0, h0, attend, tail_rows, final_gain, p):
    b, t, d = x.shape
    lru_w = p["conv_w"].shape[1]
    attn_w = p["norm_attn_out"].shape[1]
    rows = b * t
    tm = min(ROW_TILE, rows)
    assert tail_rows == tm or tail_rows == t, "key/value tail must be one row tile"
    tiles_per_tail = 1 if tail_rows == t and t <= tm else t // tm
    x2d = x.reshape(rows, d)
    u, gate, q, k, v, k_tail, v_tail = _in_projection(x2d, p["norm_mix"], p["w_in"], tiles_per_tail, lru_w, attn_w)
    y_lru, conv_new, h_last = _lru_group(u.reshape(b, t, lru_w), gate.reshape(b, t, lru_w), conv0, h0,
                                         p["conv_w"], p["conv_b"], p["wbd"], p["lru_ba"], p["lru_bx"],
                                         p["lru_lambda"])
    y_att = attend(q.reshape(b, t, attn_w), k.reshape(b, t, attn_w), v.reshape(b, t, attn_w))
    final_norm = final_gain is not None
    gfin = final_gain if final_norm else p["norm_ffn"]
    y = _out_ffn(x2d, y_lru.reshape(rows, lru_w), y_att.reshape(rows, attn_w), p["norm_lru_out"],
                 p["norm_attn_out"], p["w_out"], p["norm_ffn"], p["w_gate"], p["w_up"], p["w_down"],
                 gfin, final_norm)
    n_heads = attn_w // HEAD_DIM
    k_tail = k_tail.reshape(b, tail_rows, n_heads, HEAD_DIM)
    v_tail = v_tail.reshape(b, tail_rows, n_heads, HEAD_DIM)
    return y.reshape(b, t, d), conv_new, h_last.reshape(b, lru_w), k_tail, v_tail


def kernel(x_prompt, x_sample, state_conv, state_lru, cache_k, cache_v, norm_mix, w_in, conv_w, conv_b, lru_wa, lru_ba, lru_wx, lru_bx, lru_lambda, rel_bias, norm_lru_out, norm_attn_out, w_out, norm_ffn, w_gate, w_up, w_down, norm_final):
    depth = w_in.shape[0]
    xp, xs = x_prompt, x_sample
    bp, tp, d = xp.shape
    lru_w = conv_w.shape[-1]
    keep = min(PAD_ROWS, tp)
    row = lambda a: a.reshape(1, -1)
    outs = [[] for _ in range(8)]
    for l in range(depth):
        p = dict(
            norm_mix=row(norm_mix[l]), w_in=w_in[l].astype(BF16), conv_w=conv_w[l], conv_b=row(conv_b[l]),
            wbd=_block_diag_gates(lru_wa[l], lru_wx[l]), lru_ba=row(lru_ba[l]), lru_bx=row(lru_bx[l]),
            lru_lambda=row(lru_lambda[l]), norm_lru_out=row(norm_lru_out[l]),
            norm_attn_out=row(norm_attn_out[l]), w_out=w_out[l].astype(BF16), norm_ffn=row(norm_ffn[l]),
            w_gate=w_gate[l].astype(BF16), w_up=w_up[l].astype(BF16), w_down=w_down[l].astype(BF16))
        final_gain = row(norm_final) if l == depth - 1 else None
        conv0 = jnp.zeros((bp, CONV_WIDTH - 1, lru_w), xp.dtype)
        h0 = jnp.zeros((bp, lru_w), xp.dtype)
        att_p = functools.partial(_attend_prompt, table=rel_bias[l])
        xp, cp, hp, kp, vp = _layer(xp, conv0, h0, att_p, keep, final_gain, p)
        att_s = functools.partial(_attend_sample, k_cache=cache_k[l], v_cache=cache_v[l], table=rel_bias[l])
        xs, cs, hs, ksn, vsn = _layer(xs, state_conv[l], state_lru[l], att_s, xs.shape[1], final_gain, p)
        for lst, val in zip(outs, (cp, hp, kp, vp, cs, hs, ksn, vsn)):
            lst.append(val)
    return (xp, xs) + tuple(jnp.stack(o) for o in outs)
```

```python
import functools
import math

import jax
import jax.numpy as jnp
import numpy as np
from jax import lax
from jax.experimental import pallas as pl
from jax.experimental.pallas import tpu as pltpu

F32 = jnp.float32
BF16 = jnp.bfloat16

EPS = 1e-6
CHUNK = 64
LEFT_CHUNKS = 8
PAD_ROWS = LEFT_CHUNKS * CHUNK
MAX_REL = 128
HEAD_DIM = 64
N_LRU_BLOCKS = 8
CONV_WIDTH = 4
LRU_C = 8.0
NEG_BIG = -1e30

ROW_TILE = 512
LRU_TIME_TILE = 512
Q_BLOCK = 256
KEY_BLOCK = Q_BLOCK
SUBLANES = 8
MXU_DIM = 256
LANES = 128
VMEM_LIMIT = 56 * 1024 * 1024

NT_DIMS = (((1,), (1,)), ((), ()))

NEAR_GAPS = -(-(MAX_REL + CHUNK - 1) // CHUNK)
TILE_GAPS = (-1,) + tuple(range(NEAR_GAPS)) + (LEFT_CHUNKS,)


def _params(sem):
    return pltpu.CompilerParams(dimension_semantics=sem, vmem_limit_bytes=VMEM_LIMIT)


def _const_spec(shape):
    nd = len(shape)
    return pl.BlockSpec(shape, lambda *_: (0,) * nd, pipeline_mode=pl.Buffered(1))


def _rms(x, gain):
    return x * lax.rsqrt(jnp.mean(x * x, axis=-1, keepdims=True) + EPS) * gain


def _sigmoid(x):
    return 0.5 * jnp.tanh(0.5 * x) + 0.5


def _gelu_tanh(x):
    c = math.sqrt(2.0 / math.pi)
    return 0.5 * x * (1.0 + jnp.tanh(c * (x + 0.044715 * (x * x * x))))


def _project(x_ref, gain_ref, w_ref, u_ref, gate_ref, q_ref, lru_w, attn_w):
    xn = _rms(x_ref[...], gain_ref[...]).astype(BF16)
    proj = jnp.dot(xn, w_ref[...], preferred_element_type=F32)
    c0, c1, c2, c3 = lru_w, 2 * lru_w, 2 * lru_w + attn_w, 2 * lru_w + 2 * attn_w
    u_ref[...] = proj[:, :c0]
    gate_ref[...] = proj[:, c0:c1]
    q_ref[...] = (proj[:, c1:c2] * (HEAD_DIM ** -0.5)).astype(BF16)
    return proj[:, c2:c3], proj[:, c3:]


def _inproj_prompt_kernel(x_ref, gain_ref, w_ref, u_ref, gate_ref, q_ref, k_ref, vt_ref, ktail_ref,
                          vtail_ref, *, lru_w, attn_w, tiles_per_seq):
    k, v = _project(x_ref, gain_ref, w_ref, u_ref, gate_ref, q_ref, lru_w, attn_w)
    k_ref[...] = k.astype(BF16)
    v_t = v.T
    for blk in range(vt_ref.shape[0]):
        vt_ref[blk] = v_t[:, blk * KEY_BLOCK:(blk + 1) * KEY_BLOCK].astype(BF16)

    @pl.when(pl.program_id(0) % tiles_per_seq == tiles_per_seq - 1)
    def _():
        ktail_ref[0] = k.T
        vtail_ref[0] = v_t


def _inproj_sample_kernel(x_ref, gain_ref, w_ref, u_ref, gate_ref, q_ref, k_ref, v_ref, kt_ref, vt_ref,
                          *, lru_w, attn_w):
    k, v = _project(x_ref, gain_ref, w_ref, u_ref, gate_ref, q_ref, lru_w, attn_w)
    k_ref[...] = k.astype(BF16)
    v_ref[...] = v.astype(BF16)
    tm = k.shape[0]
    n_heads = attn_w // HEAD_DIM
    for h in range(n_heads):
        hs = slice(h * HEAD_DIM, (h + 1) * HEAD_DIM)
        kt_ref[pl.ds(h, tm, stride=n_heads), :] = k[:, hs]
        vt_ref[pl.ds(h, tm, stride=n_heads), :] = v[:, hs]


def _in_projection(x2d, gain, w_bf16, lru_w, attn_w, seq_len, prompt):
    rows, d = x2d.shape
    tm = min(ROW_TILE, rows)
    assert rows % tm == 0
    n_tiles = rows // tm
    n_cols = w_bf16.shape[1]
    row_spec = lambda w: pl.BlockSpec((tm, w), lambda i: (i, 0))
    in_specs = [row_spec(d), _const_spec((1, d)), _const_spec((d, n_cols))]
    common_specs = [row_spec(lru_w), row_spec(lru_w), row_spec(attn_w), row_spec(attn_w)]
    common_shapes = [jax.ShapeDtypeStruct((rows, lru_w), F32), jax.ShapeDtypeStruct((rows, lru_w), F32),
                     jax.ShapeDtypeStruct((rows, attn_w), BF16), jax.ShapeDtypeStruct((rows, attn_w), BF16)]
    if prompt:
        assert seq_len % tm == 0 and tm % KEY_BLOCK == 0 and tm == min(PAD_ROWS, seq_len)
        tiles_per_seq = seq_len // tm
        blocks = tm // KEY_BLOCK
        tail_spec = pl.BlockSpec((1, attn_w, tm), lambda i: (i // tiles_per_seq, 0, 0))
        tail_shape = jax.ShapeDtypeStruct((rows // seq_len, attn_w, tm), F32)
        body = functools.partial(_inproj_prompt_kernel, lru_w=lru_w, attn_w=attn_w, tiles_per_seq=tiles_per_seq)
        out_specs = common_specs + [pl.BlockSpec((blocks, attn_w, KEY_BLOCK), lambda i: (i, 0, 0)),
                                    tail_spec, tail_spec]
        out_shape = common_shapes + [jax.ShapeDtypeStruct((rows // KEY_BLOCK, attn_w, KEY_BLOCK), BF16),
                                     tail_shape, tail_shape]
    else:
        n_heads = attn_w // HEAD_DIM
        tail_spec = pl.BlockSpec((tm * n_heads, HEAD_DIM), lambda i: (i, 0))
        tail_shape = jax.ShapeDtypeStruct((rows * n_heads, HEAD_DIM), F32)
        body = functools.partial(_inproj_sample_kernel, lru_w=lru_w, attn_w=attn_w)
        out_specs = common_specs + [row_spec(attn_w), tail_spec, tail_spec]
        out_shape = common_shapes + [jax.ShapeDtypeStruct((rows, attn_w), BF16), tail_shape, tail_shape]
    return pl.pallas_call(
        body, grid=(n_tiles,), in_specs=in_specs, out_specs=out_specs, out_shape=out_shape,
        compiler_params=_params(("arbitrary",)), name="in_projection",
    )(x2d, gain, w_bf16)


def _lru_kernel(u_ref, gate_ref, conv0_ref, h0_ref, cw_ref, cb_ref, wbd_ref, ba_ref, bx_ref, lam_ref,
                y_ref, convn_ref, hlast_ref, ext_s, a_s, b_s, h_s, hc_s, *, tc):
    c = pl.program_id(1)
    hist = CONV_WIDTH - 1
    base = SUBLANES

    @pl.when(c == 0)
    def _():
        ext_s[base - hist:base, :] = conv0_ref[0]
        hc_s[...] = h0_ref[0]

    u = u_ref[0]
    ext_s[base:base + tc, :] = u
    w = cw_ref[...]
    uc = cb_ref[...] + u * w[hist:hist + 1]
    for kk in range(hist):
        uc = uc + ext_s[base - hist + kk:base - hist + kk + tc, :] * w[kk:kk + 1]
    tail = ext_s[base + tc - hist:base + tc, :]
    ext_s[base - hist:base, :] = tail
    convn_ref[0] = tail

    lam = lam_ref[...]
    log_sig_lam = jnp.minimum(lam, 0.0) - jnp.log1p(jnp.exp(-jnp.abs(lam)))
    ucb = uc.astype(BF16)
    half = wbd_ref.shape[1]
    for hf in range(wbd_ref.shape[0]):
        cols = slice(hf * half, (hf + 1) * half)
        pre = jnp.dot(ucb[:, cols], wbd_ref[hf], preferred_element_type=F32)
        r = _sigmoid(pre[:, :half] + ba_ref[:, cols])
        i = _sigmoid(pre[:, half:] + bx_ref[:, cols])
        a = jnp.exp((LRU_C * r) * log_sig_lam[:, cols])
        gain = jnp.sqrt(1.0 - a * a)
        a_s[:, cols] = a
        b_s[:, cols] = gain * (i * uc[:, cols])

    row = lax.broadcasted_iota(jnp.int32, (SUBLANES, a_s.shape[1]), 0)

    def group(gi, h_prev):
        r0 = pl.multiple_of(gi * SUBLANES, SUBLANES)
        a = a_s[pl.ds(r0, SUBLANES), :]
        b = b_s[pl.ds(r0, SUBLANES), :]
        sh = 1
        while sh < SUBLANES:
            keep = row >= sh
            a_sh = jnp.where(keep, pltpu.roll(a, sh, 0), 1.0)
            b_sh = jnp.where(keep, pltpu.roll(b, sh, 0), 0.0)
            b = a * b_sh + b
            a = a * a_sh
            sh *= 2
        h = a * h_prev + b
        h_s[pl.ds(r0, SUBLANES), :] = h
        return h[SUBLANES - 1:SUBLANES, :]

    h_last = lax.fori_loop(0, tc // SUBLANES, group, hc_s[...])
    hc_s[...] = h_last
    hlast_ref[0] = h_last
    y_ref[0] = _gelu_tanh(gate_ref[0]) * h_s[...]


def _lru_group(u, gate, conv0, h0, conv_w, conv_b, wbd, ba, bx, lam):
    b, t, w = u.shape
    tc = min(LRU_TIME_TILE, t)
    assert t % tc == 0 and tc % SUBLANES == 0 and tc >= CONV_WIDTH - 1
    hist = CONV_WIDTH - 1
    seq_spec = pl.BlockSpec((1, tc, w), lambda i, c: (i, c, 0))
    per_batch = lambda r: pl.BlockSpec((1, r, w), lambda i, c: (i, 0, 0))
    return pl.pallas_call(
        functools.partial(_lru_kernel, tc=tc),
        grid=(b, t // tc),
        in_specs=[seq_spec, seq_spec, per_batch(hist), per_batch(1),
                  _const_spec(conv_w.shape), _const_spec((1, w)), _const_spec(wbd.shape),
                  _const_spec((1, w)), _const_spec((1, w)), _const_spec((1, w))],
        out_specs=[seq_spec, per_batch(hist), per_batch(1)],
        out_shape=[jax.ShapeDtypeStruct((b, t, w), F32), jax.ShapeDtypeStruct((b, hist, w), F32),
                   jax.ShapeDtypeStruct((b, 1, w), F32)],
        scratch_shapes=[pltpu.VMEM((SUBLANES + tc, w), F32), pltpu.VMEM((tc, w), F32),
                        pltpu.VMEM((tc, w), F32), pltpu.VMEM((tc, w), F32), pltpu.VMEM((1, w), F32)],
        compiler_params=_params(("arbitrary", "arbitrary")),
        name="rg_lru",
    )(u, gate, conv0, h0.reshape(b, 1, w), conv_w, conv_b, wbd, ba, bx, lam)


def _gap_kind(gap):
    if gap < 0 or gap > LEFT_CHUNKS:
        return "masked"
    return "near" if gap < NEAR_GAPS else "far"


def _attn_prompt_kernel(q_ref, k_ref, vt_ref, bt_ref, o_ref, s_scr, p_scr, ot_scr, *, n_heads, n_blocks):
    j = pl.program_id(1)
    lane = lax.broadcasted_iota(jnp.int32, (1, LANES), 1)
    head_lanes = (lane < HEAD_DIM, lane >= HEAD_DIM)
    q_chunks_per_tile = LANES // CHUNK
    zero_tile = jnp.zeros((CHUNK, LANES), BF16)

    def block(lead, k_start, vt_start):
        k_rows = lead * CHUNK + Q_BLOCK
        plan = []
        for lt in range(Q_BLOCK // LANES):
            lanes = slice(lt * LANES, (lt + 1) * LANES)
            live = []
            for kc in range(k_rows // CHUNK):
                gap = lt * q_chunks_per_tile + lead - kc
                if _gap_kind(gap) == "masked" and _gap_kind(gap + 1) == "masked":
                    p_scr[0, kc * CHUNK:(kc + 1) * CHUNK, lanes] = zero_tile
                    p_scr[1, kc * CHUNK:(kc + 1) * CHUNK, lanes] = zero_tile
                else:
                    live.append((kc, gap))
            plan.append((lanes, live))

        def scores(h):
            pair = slice((h // 2) * LANES, (h // 2 + 1) * LANES)
            qm = jnp.where(head_lanes[h % 2], q_ref[0, :, pair], jnp.zeros((), BF16))
            kw = k_ref[0, pl.ds(k_start, k_rows), pair]
            s_scr[h % 2, :k_rows, :] = lax.dot_general(kw, qm, NT_DIMS, preferred_element_type=F32)

        def softmax(h):
            slot = h % 2
            inv_l = []
            for lanes, live in plan:
                def tile(kc, gap):
                    t = s_scr[slot, kc * CHUNK:(kc + 1) * CHUNK, lanes]
                    if gap in TILE_GAPS:
                        t = t + bt_ref[h, TILE_GAPS.index(gap)]
                    return t

                m = None
                for kc, gap in live:
                    t = tile(kc, gap)
                    m = t if m is None else jnp.maximum(m, t)
                m = jnp.max(m, axis=0, keepdims=True)
                l = None
                for kc, gap in live:
                    p = jnp.exp(tile(kc, gap) - m)
                    l = p if l is None else l + p
                    p_scr[slot, kc * CHUNK:(kc + 1) * CHUNK, lanes] = p.astype(BF16)
                inv_l.append(1.0 / jnp.sum(l, axis=0, keepdims=True))
            return jnp.concatenate(inv_l, axis=1)

        def weighted_values(h, inv_l):
            rows = slice(h * HEAD_DIM, (h + 1) * HEAD_DIM)
            acc = None
            for blk in range(k_rows // KEY_BLOCK):
                part = jnp.dot(vt_ref[vt_start + blk, rows, :],
                               p_scr[h % 2, blk * KEY_BLOCK:(blk + 1) * KEY_BLOCK, :],
                               preferred_element_type=F32)
                acc = part if acc is None else acc + part
            ot_scr[rows, :] = acc * inv_l

        scores(0)
        inv_prev = None
        for h in range(n_heads):
            if h + 1 < n_heads:
                scores(h + 1)
            if h > 0:
                weighted_values(h - 1, inv_prev)
            inv_prev = softmax(h)
        weighted_values(n_heads - 1, inv_prev)
        o_ref[0] = ot_scr[...].T

    ramp = PAD_ROWS // Q_BLOCK
    for jj in range(min(ramp, n_blocks)):
        pl.when(j == jj)(functools.partial(block, jj * (Q_BLOCK // CHUNK), 0, 0))
    if n_blocks > ramp:
        @pl.when(j >= ramp)
        def _():
            block(LEFT_CHUNKS, pl.multiple_of((j - ramp) * Q_BLOCK, Q_BLOCK), j - ramp)


def _rel_bias_matrix(table, rows, cols, dist00):
    n = rows + cols - 1
    dist = np.arange(n) - (cols - 1) + dist00
    g = table[:, np.clip(dist, -MAX_REL, MAX_REL) + MAX_REL].astype(F32)
    period = rows + cols
    g = jnp.pad(g, ((0, 0), (0, period - n)))
    flat = jnp.tile(g, (1, rows + 1))[:, :rows * (period + 1)]
    hankel = flat.reshape(-1, rows, period + 1)[:, :, :cols]
    return hankel[:, :, ::-1]


def _prompt_bias_tiles(table):
    far = table[:, 2 * MAX_REL].astype(F32)[:, None, None]

    def chunk_pair(gap):
        kind = _gap_kind(gap)
        if kind == "near":
            return jnp.swapaxes(_rel_bias_matrix(table, CHUNK, CHUNK, gap * CHUNK), 1, 2) - far
        fill = NEG_BIG if kind == "masked" else 0.0
        return jnp.full((table.shape[0], CHUNK, CHUNK), fill, F32)

    tiles = [jnp.concatenate([chunk_pair(g), chunk_pair(g + 1)], axis=2) for g in TILE_GAPS]
    return jnp.stack(tiles, axis=1)


def _attend_prompt(q, k, v_t, table):
    b, t, w = q.shape
    n_heads = w // HEAD_DIM
    assert t % Q_BLOCK == 0 and PAD_ROWS % Q_BLOCK == 0 and LANES // CHUNK == 2 and Q_BLOCK % LANES == 0
    n_blocks = t // Q_BLOCK
    k_rows = PAD_ROWS + Q_BLOCK
    tiles = _prompt_bias_tiles(table)
    blk = pl.BlockSpec((1, Q_BLOCK, w), lambda i, j: (i, j, 0))
    return pl.pallas_call(
        functools.partial(_attn_prompt_kernel, n_heads=n_heads, n_blocks=n_blocks),
        grid=(b, n_blocks),
        in_specs=[blk, pl.BlockSpec((1, t, w), lambda i, j: (i, 0, 0)),
                  pl.BlockSpec((n_blocks, w, KEY_BLOCK), lambda i, j: (i, 0, 0)), _const_spec(tiles.shape)],
        out_specs=blk,
        out_shape=jax.ShapeDtypeStruct((b, t, w), F32),
        scratch_shapes=[pltpu.VMEM((2, k_rows, Q_BLOCK), F32), pltpu.VMEM((2, k_rows, Q_BLOCK), BF16),
                        pltpu.VMEM((w, Q_BLOCK), F32)],
        compiler_params=_params(("arbitrary", "arbitrary")),
        name="attn_prompt",
    )(q, k, v_t, tiles)


def _attn_sample_kernel(q_ref, k_ref, v_ref, ckt_ref, cvt_ref, bc_ref, bn_ref, o_ref, *, n_heads):
    for h in range(n_heads):
        hs = slice(h * HEAD_DIM, (h + 1) * HEAD_DIM)
        qh = q_ref[0, :, hs]
        s_c = jnp.dot(qh, ckt_ref[0, hs, :].astype(BF16), preferred_element_type=F32) + bc_ref[h]
        s_n = lax.dot_general(qh, k_ref[0, :, hs], NT_DIMS, preferred_element_type=F32) + bn_ref[h]
        m = jnp.maximum(jnp.max(s_c, axis=-1, keepdims=True), jnp.max(s_n, axis=-1, keepdims=True))
        p_c = jnp.exp(s_c - m)
        p_n = jnp.exp(s_n - m)
        l = jnp.sum(p_c, axis=-1, keepdims=True) + jnp.sum(p_n, axis=-1, keepdims=True)
        o = lax.dot_general(p_c.astype(BF16), cvt_ref[0, hs, :].astype(BF16), NT_DIMS,
                            preferred_element_type=F32)
        o = o + jnp.dot(p_n.astype(BF16), v_ref[0, :, hs], preferred_element_type=F32)
        o_ref[0, :, hs] = o / l


def _attend_sample(q, k, v, k_cache, v_cache, table):
    b, t, w = q.shape
    n_heads = w // HEAD_DIM
    r = k_cache.shape[1]
    bias_c = _rel_bias_matrix(table, t, r, r)
    bias_n = _rel_bias_matrix(table, t, t, 0)
    channel_major = lambda c: jnp.transpose(c, (0, 2, 3, 1)).reshape(b, w, r)
    new = pl.BlockSpec((1, t, w), lambda i: (i, 0, 0))
    old = pl.BlockSpec((1, w, r), lambda i: (i, 0, 0))
    return pl.pallas_call(
        functools.partial(_attn_sample_kernel, n_heads=n_heads),
        grid=(b,),
        in_specs=[new, new, new, old, old, _const_spec(bias_c.shape), _const_spec(bias_n.shape)],
        out_specs=new,
        out_shape=jax.ShapeDtypeStruct((b, t, w), F32),
        compiler_params=_params(("arbitrary",)),
        name="attn_sample",
    )(q, k, v, channel_major(k_cache), channel_major(v_cache), bias_c, bias_n)


def _out_ffn_kernel(x_ref, yl_ref, ya_ref, gl_ref, ga_ref, wo_ref, gf_ref, wg_ref, wu_ref, wd_ref,
                    gfin_ref, o_ref, *, ff_tile, final_norm):
    lru_w = yl_ref.shape[1]
    nl = _rms(yl_ref[...], gl_ref[...]).astype(BF16)
    na = _rms(ya_ref[...], ga_ref[...]).astype(BF16)
    mix = jnp.dot(nl, wo_ref[:lru_w, :], preferred_element_type=F32)
    mix = mix + jnp.dot(na, wo_ref[lru_w:, :], preferred_element_type=F32)
    x1 = x_ref[...] + mix
    xn = _rms(x1, gf_ref[...]).astype(BF16)
    acc = x1
    for c0 in range(0, wg_ref.shape[1], ff_tile):
        cols = slice(c0, c0 + ff_tile)
        g = jnp.dot(xn, wg_ref[:, cols], preferred_element_type=F32)
        up = jnp.dot(xn, wu_ref[:, cols], preferred_element_type=F32)
        hmid = (g * _sigmoid(g) * up).astype(BF16)
        acc = acc + jnp.dot(hmid, wd_ref[cols, :], preferred_element_type=F32)
    o_ref[...] = _rms(acc, gfin_ref[...]) if final_norm else acc


def _ffn_tile(d_ff):
    best = LANES
    for n in range(1, d_ff // LANES + 1):
        tile = n * LANES
        if d_ff % tile == 0 and tile <= 1536:
            best = tile
    return best


def _out_ffn(x2d, y_lru, y_att, gl, ga, wo, gf, wg, wu, wd, gfin, final_norm):
    rows, d = x2d.shape
    tm = min(ROW_TILE, rows)
    assert rows % tm == 0
    lru_w, attn_w, d_ff = y_lru.shape[1], y_att.shape[1], wg.shape[1]
    row_spec = lambda w: pl.BlockSpec((tm, w), lambda i: (i, 0))
    return pl.pallas_call(
        functools.partial(_out_ffn_kernel, ff_tile=_ffn_tile(d_ff), final_norm=final_norm),
        grid=(rows // tm,),
        in_specs=[row_spec(d), row_spec(lru_w), row_spec(attn_w), _const_spec((1, lru_w)),
                  _const_spec((1, attn_w)), _const_spec(wo.shape), _const_spec((1, d)),
                  _const_spec(wg.shape), _const_spec(wu.shape), _const_spec(wd.shape), _const_spec((1, d))],
        out_specs=row_spec(d),
        out_shape=jax.ShapeDtypeStruct((rows, d), F32),
        compiler_params=_params(("arbitrary",)),
        name="out_ffn",
    )(x2d, y_lru, y_att, gl, ga, wo, gf, wg, wu, wd, gfin)


def _block_diag_gates(wa, wx):
    n, c, _ = wa.shape
    per = MXU_DIM // c
    assert n % per == 0
    eye = jnp.eye(per, dtype=wa.dtype)

    def bd(w):
        w = w.reshape(n // per, per, c, c)
        return (eye[None, :, None, :, None] * w[:, :, :, None, :]).reshape(n // per, per * c, per * c)

    return jnp.concatenate([bd(wa), bd(wx)], axis=-1).astype(BF16)


def _layer(x, conv0, h0, prompt, attend, final_gain, p):
    b, t, d = x.shape
    lru_w = p["conv_w"].shape[1]
    attn_w = p["norm_attn_out"].shape[1]
    n_heads = attn_w // HEAD_DIM
    rows = b * t
    x2d = x.reshape(rows, d)
    u, gate, q, k, v, k_tail, v_tail = _in_projection(x2d, p["norm_mix"], p["w_in"], lru_w, attn_w, t, prompt)
    y_lru, conv_new, h_last = _lru_group(u.reshape(b, t, lru_w), gate.reshape(b, t, lru_w), conv0, h0,
                                         p["conv_w"], p["conv_b"], p["wbd"], p["lru_ba"], p["lru_bx"],
                                         p["lru_lambda"])
    y_att = attend(q.reshape(b, t, attn_w), k.reshape(b, t, attn_w), v if prompt else v.reshape(b, t, attn_w))
    final_norm = final_gain is not None
    gfin = final_gain if final_norm else p["norm_ffn"]
    y = _out_ffn(x2d, y_lru.reshape(rows, lru_w), y_att.reshape(rows, attn_w), p["norm_lru_out"],
                 p["norm_attn_out"], p["w_out"], p["norm_ffn"], p["w_gate"], p["w_up"], p["w_down"],
                 gfin, final_norm)
    if prompt:
        keep = k_tail.shape[2]
        k_tail = k_tail.reshape(b, n_heads, HEAD_DIM, keep).transpose(0, 3, 1, 2)
        v_tail = v_tail.reshape(b, n_heads, HEAD_DIM, keep).transpose(0, 3, 1, 2)
    else:
        k_tail = k_tail.reshape(b, t, n_heads, HEAD_DIM)
        v_tail = v_tail.reshape(b, t, n_heads, HEAD_DIM)
    return y.reshape(b, t, d), conv_new, h_last.reshape(b, lru_w), k_tail, v_tail


def kernel(x_prompt, x_sample, state_conv, state_lru, cache_k, cache_v, norm_mix, w_in, conv_w, conv_b, lru_wa, lru_ba, lru_wx, lru_bx, lru_lambda, rel_bias, norm_lru_out, norm_attn_out, w_out, norm_ffn, w_gate, w_up, w_down, norm_final):
    depth = w_in.shape[0]
    xp, xs = x_prompt, x_sample
    bp = xp.shape[0]
    lru_w = conv_w.shape[-1]
    row = lambda a: a.reshape(1, -1)
    outs = [[] for _ in range(8)]
    for l in range(depth):
        p = dict(
            norm_mix=row(norm_mix[l]), w_in=w_in[l].astype(BF16), conv_w=conv_w[l], conv_b=row(conv_b[l]),
            wbd=_block_diag_gates(lru_wa[l], lru_wx[l]), lru_ba=row(lru_ba[l]), lru_bx=row(lru_bx[l]),
            lru_lambda=row(lru_lambda[l]), norm_lru_out=row(norm_lru_out[l]),
            norm_attn_out=row(norm_attn_out[l]), w_out=w_out[l].astype(BF16), norm_ffn=row(norm_ffn[l]),
            w_gate=w_gate[l].astype(BF16), w_up=w_up[l].astype(BF16), w_down=w_down[l].astype(BF16))
        final_gain = row(norm_final) if l == depth - 1 else None
        conv0 = jnp.zeros((bp, CONV_WIDTH - 1, lru_w), xp.dtype)
        h0 = jnp.zeros((bp, lru_w), xp.dtype)
        att_p = functools.partial(_attend_prompt, table=rel_bias[l])
        xp, cp, hp, kp, vp = _layer(xp, conv0, h0, True, att_p, final_gain, p)
        att_s = functools.partial(_attend_sample, k_cache=cache_k[l], v_cache=cache_v[l], table=rel_bias[l])
        xs, cs, hs, ksn, vsn = _layer(xs, state_conv[l], state_lru[l], False, att_s, final_gain, p)
        for lst, val in zip(outs, (cp, hp, kp, vp, cs, hs, ksn, vsn)):
            lst.append(val)
    return (xp, xs) + tuple(jnp.stack(o) for o in outs)
```

```python
import functools
import math

import jax
import jax.numpy as jnp
import numpy as np
from jax import lax
from jax.experimental import pallas as pl
from jax.experimental.pallas import tpu as pltpu

F32 = jnp.float32
BF16 = jnp.bfloat16

EPS = 1e-6
CHUNK = 64
LEFT_CHUNKS = 8
PAD_ROWS = LEFT_CHUNKS * CHUNK
MAX_REL = 128
HEAD_DIM = 64
N_LRU_BLOCKS = 8
CONV_WIDTH = 4
LRU_C = 8.0
NEG_BIG = -1e30

ROW_TILE = 512
LRU_TIME_TILE = 512
Q_BLOCK = 256
KEY_BLOCK = Q_BLOCK
FFN_TILE = 256
DOWN_GROUP = 4
LRU_GATE_ROWS = 256
LRU_PIECE_ROWS = 64
SUBLANES = 8
MXU_DIM = 256
LANES = 128
VMEM_LIMIT = 56 * 1024 * 1024

NT_DIMS = (((1,), (1,)), ((), ()))

NEAR_GAPS = -(-(MAX_REL + CHUNK - 1) // CHUNK)
TILE_GAPS = (-1,) + tuple(range(NEAR_GAPS)) + (LEFT_CHUNKS,)


def _params(sem):
    return pltpu.CompilerParams(dimension_semantics=sem, vmem_limit_bytes=VMEM_LIMIT)


def _const_spec(shape):
    nd = len(shape)
    return pl.BlockSpec(shape, lambda *_: (0,) * nd, pipeline_mode=pl.Buffered(1))


def _rms(x, gain):
    return x * lax.rsqrt(jnp.mean(x * x, axis=-1, keepdims=True) + EPS) * gain


def _sigmoid(x):
    return 0.5 * jnp.tanh(0.5 * x) + 0.5


def _gelu_tanh(x):
    c = math.sqrt(2.0 / math.pi)
    return 0.5 * x * (1.0 + jnp.tanh(c * (x + 0.044715 * (x * x * x))))


def _project(x_ref, gain_ref, w_ref, u_ref, gate_ref, q_ref, lru_w, attn_w):
    xn = _rms(x_ref[...], gain_ref[...]).astype(BF16)
    proj = jnp.dot(xn, w_ref[...], preferred_element_type=F32)
    c0, c1, c2, c3 = lru_w, 2 * lru_w, 2 * lru_w + attn_w, 2 * lru_w + 2 * attn_w
    u_ref[...] = proj[:, :c0]
    gate_ref[...] = proj[:, c0:c1]
    q_ref[...] = (proj[:, c1:c2] * (HEAD_DIM ** -0.5)).astype(BF16)
    return proj[:, c2:c3], proj[:, c3:]


def _inproj_prompt_kernel(x_ref, gain_ref, w_ref, u_ref, gate_ref, q_ref, k_ref, vt_ref, ktail_ref,
                          vtail_ref, *, lru_w, attn_w, tiles_per_seq):
    k, v = _project(x_ref, gain_ref, w_ref, u_ref, gate_ref, q_ref, lru_w, attn_w)
    k_ref[...] = k.astype(BF16)
    v_t = v.T
    for blk in range(vt_ref.shape[0]):
        vt_ref[blk] = v_t[:, blk * KEY_BLOCK:(blk + 1) * KEY_BLOCK].astype(BF16)

    @pl.when(pl.program_id(0) % tiles_per_seq == tiles_per_seq - 1)
    def _():
        ktail_ref[0] = k.T
        vtail_ref[0] = v_t


def _inproj_sample_kernel(x_ref, gain_ref, w_ref, u_ref, gate_ref, q_ref, k_ref, v_ref, kt_ref, vt_ref,
                          *, lru_w, attn_w):
    k, v = _project(x_ref, gain_ref, w_ref, u_ref, gate_ref, q_ref, lru_w, attn_w)
    k_ref[...] = k.astype(BF16)
    v_ref[...] = v.astype(BF16)
    tm = k.shape[0]
    n_heads = attn_w // HEAD_DIM
    for h in range(n_heads):
        hs = slice(h * HEAD_DIM, (h + 1) * HEAD_DIM)
        kt_ref[pl.ds(h, tm, stride=n_heads), :] = k[:, hs]
        vt_ref[pl.ds(h, tm, stride=n_heads), :] = v[:, hs]


def _in_projection(x2d, gain, w_bf16, lru_w, attn_w, seq_len, prompt):
    rows, d = x2d.shape
    tm = min(ROW_TILE, rows)
    assert rows % tm == 0
    n_tiles = rows // tm
    n_cols = w_bf16.shape[1]
    row_spec = lambda w: pl.BlockSpec((tm, w), lambda i: (i, 0))
    in_specs = [row_spec(d), _const_spec((1, d)), _const_spec((d, n_cols))]
    common_specs = [row_spec(lru_w), row_spec(lru_w), row_spec(attn_w), row_spec(attn_w)]
    common_shapes = [jax.ShapeDtypeStruct((rows, lru_w), F32), jax.ShapeDtypeStruct((rows, lru_w), F32),
                     jax.ShapeDtypeStruct((rows, attn_w), BF16), jax.ShapeDtypeStruct((rows, attn_w), BF16)]
    if prompt:
        assert seq_len % tm == 0 and tm % KEY_BLOCK == 0 and tm == min(PAD_ROWS, seq_len)
        tiles_per_seq = seq_len // tm
        blocks = tm // KEY_BLOCK
        tail_spec = pl.BlockSpec((1, attn_w, tm), lambda i: (i // tiles_per_seq, 0, 0))
        tail_shape = jax.ShapeDtypeStruct((rows // seq_len, attn_w, tm), F32)
        body = functools.partial(_inproj_prompt_kernel, lru_w=lru_w, attn_w=attn_w, tiles_per_seq=tiles_per_seq)
        out_specs = common_specs + [pl.BlockSpec((blocks, attn_w, KEY_BLOCK), lambda i: (i, 0, 0)),
                                    tail_spec, tail_spec]
        out_shape = common_shapes + [jax.ShapeDtypeStruct((rows // KEY_BLOCK, attn_w, KEY_BLOCK), BF16),
                                     tail_shape, tail_shape]
    else:
        n_heads = attn_w // HEAD_DIM
        tail_spec = pl.BlockSpec((tm * n_heads, HEAD_DIM), lambda i: (i, 0))
        tail_shape = jax.ShapeDtypeStruct((rows * n_heads, HEAD_DIM), F32)
        body = functools.partial(_inproj_sample_kernel, lru_w=lru_w, attn_w=attn_w)
        out_specs = common_specs + [row_spec(attn_w), tail_spec, tail_spec]
        out_shape = common_shapes + [jax.ShapeDtypeStruct((rows, attn_w), BF16), tail_shape, tail_shape]
    return pl.pallas_call(
        body, grid=(n_tiles,), in_specs=in_specs, out_specs=out_specs, out_shape=out_shape,
        compiler_params=_params(("arbitrary",)), name="in_projection",
    )(x2d, gain, w_bf16)


def _log_sigmoid(x):
    return jnp.minimum(x, 0.0) - jnp.log1p(jnp.exp(-jnp.abs(x)))


def _conv_rows(ext_s, r0, n, w, bias):
    hist = CONV_WIDTH - 1
    uc = bias + ext_s[r0:r0 + n, :] * w[hist:hist + 1]
    for kk in range(hist):
        uc = uc + ext_s[r0 - hist + kk:r0 - hist + kk + n, :] * w[kk:kk + 1]
    return uc


def _gate_terms(pre_a, pre_x, uc, ba, bx, log_sig_lam):
    r = _sigmoid(pre_a + ba)
    i = _sigmoid(pre_x + bx)
    a = jnp.exp((LRU_C * r) * log_sig_lam)
    gain = jnp.sqrt(1.0 - a * a)
    return a, gain * (i * uc)


def _scan_group(a, b, h_prev, row):
    sh = 1
    while sh < SUBLANES:
        keep = row >= sh
        a_sh = jnp.where(keep, pltpu.roll(a, sh, 0), 1.0)
        b_sh = jnp.where(keep, pltpu.roll(b, sh, 0), 0.0)
        b = a * b_sh + b
        a = a * a_sh
        sh *= 2
    return a * h_prev + b


def _lru_kernel(u_ref, gate_ref, conv0_ref, h0_ref, cw_ref, cb_ref, wbd_ref, ba_ref, bx_ref, lam_ref,
                y_ref, convn_ref, hlast_ref, ext_s, a_s, b_s, h_s, hc_s, *, tc):
    c = pl.program_id(1)
    hist = CONV_WIDTH - 1
    base = SUBLANES

    @pl.when(c == 0)
    def _():
        ext_s[base - hist:base, :] = conv0_ref[0]
        hc_s[...] = h0_ref[0]

    ext_s[base:base + tc, :] = u_ref[0]
    uc = _conv_rows(ext_s, base, tc, cw_ref[...], cb_ref[...])
    tail = ext_s[base + tc - hist:base + tc, :]
    ext_s[base - hist:base, :] = tail
    convn_ref[0] = tail

    log_sig_lam = _log_sigmoid(lam_ref[...])
    ucb = uc.astype(BF16)
    half = wbd_ref.shape[1]
    for hf in range(wbd_ref.shape[0]):
        cols = slice(hf * half, (hf + 1) * half)
        pre = jnp.dot(ucb[:, cols], wbd_ref[hf], preferred_element_type=F32)
        a, b = _gate_terms(pre[:, :half], pre[:, half:], uc[:, cols], ba_ref[:, cols], bx_ref[:, cols],
                           log_sig_lam[:, cols])
        a_s[:, cols] = a
        b_s[:, cols] = b

    row = lax.broadcasted_iota(jnp.int32, (SUBLANES, a_s.shape[1]), 0)

    def group(gi, h_prev):
        r0 = pl.multiple_of(gi * SUBLANES, SUBLANES)
        h = _scan_group(a_s[pl.ds(r0, SUBLANES), :], b_s[pl.ds(r0, SUBLANES), :], h_prev, row)
        h_s[pl.ds(r0, SUBLANES), :] = h
        return h[SUBLANES - 1:SUBLANES, :]

    h_last = lax.fori_loop(0, tc // SUBLANES, group, hc_s[...])
    hc_s[...] = h_last
    hlast_ref[0] = h_last
    y_ref[0] = _gelu_tanh(gate_ref[0]) * h_s[...]


def _lru_group(u, gate, conv0, h0, conv_w, conv_b, wbd, ba, bx, lam):
    b, t, w = u.shape
    tc = min(LRU_TIME_TILE, t)
    assert t % tc == 0 and tc % SUBLANES == 0 and tc >= CONV_WIDTH - 1
    hist = CONV_WIDTH - 1
    seq_spec = pl.BlockSpec((1, tc, w), lambda i, c: (i, c, 0))
    per_batch = lambda r: pl.BlockSpec((1, r, w), lambda i, c: (i, 0, 0))
    return pl.pallas_call(
        functools.partial(_lru_kernel, tc=tc),
        grid=(b, t // tc),
        in_specs=[seq_spec, seq_spec, per_batch(hist), per_batch(1),
                  _const_spec(conv_w.shape), _const_spec((1, w)), _const_spec(wbd.shape),
                  _const_spec((1, w)), _const_spec((1, w)), _const_spec((1, w))],
        out_specs=[seq_spec, per_batch(hist), per_batch(1)],
        out_shape=[jax.ShapeDtypeStruct((b, t, w), F32), jax.ShapeDtypeStruct((b, hist, w), F32),
                   jax.ShapeDtypeStruct((b, 1, w), F32)],
        scratch_shapes=[pltpu.VMEM((SUBLANES + tc, w), F32), pltpu.VMEM((tc, w), F32),
                        pltpu.VMEM((tc, w), F32), pltpu.VMEM((tc, w), F32), pltpu.VMEM((1, w), F32)],
        compiler_params=_params(("arbitrary", "arbitrary")),
        name="rg_lru",
    )(u, gate, conv0, h0.reshape(b, 1, w), conv_w, conv_b, wbd, ba, bx, lam)


def _gap_kind(gap):
    if gap < 0 or gap > LEFT_CHUNKS:
        return "masked"
    return "near" if gap < NEAR_GAPS else "far"


def _attn_prompt_kernel(q_ref, k_ref, vt_ref, bt_ref, o_ref, s_scr, p_scr, ot_scr, *, n_heads, n_blocks):
    lane = lax.broadcasted_iota(jnp.int32, (1, LANES), 1)
    head_lanes = (lane < HEAD_DIM, lane >= HEAD_DIM)
    q_chunks_per_tile = LANES // CHUNK
    zero_tile = jnp.zeros((CHUNK, LANES), BF16)

    def block(lead, q_start, k_start, vt_start):
        k_rows = lead * CHUNK + Q_BLOCK
        q_rows = pl.ds(q_start, Q_BLOCK)
        plan = []
        for lt in range(Q_BLOCK // LANES):
            lanes = slice(lt * LANES, (lt + 1) * LANES)
            live = []
            for kc in range(k_rows // CHUNK):
                gap = lt * q_chunks_per_tile + lead - kc
                if _gap_kind(gap) == "masked" and _gap_kind(gap + 1) == "masked":
                    p_scr[0, kc * CHUNK:(kc + 1) * CHUNK, lanes] = zero_tile
                    p_scr[1, kc * CHUNK:(kc + 1) * CHUNK, lanes] = zero_tile
                else:
                    live.append((kc, gap))
            plan.append((lanes, live))

        def scores(h):
            pair = slice((h // 2) * LANES, (h // 2 + 1) * LANES)
            qm = jnp.where(head_lanes[h % 2], q_ref[0, q_rows, pair], jnp.zeros((), BF16))
            kw = k_ref[0, pl.ds(k_start, k_rows), pair]
            s_scr[h % 2, :k_rows, :] = lax.dot_general(kw, qm, NT_DIMS, preferred_element_type=F32)

        def softmax(h):
            slot = h % 2
            inv_l = []
            for lanes, live in plan:
                def tile(kc, gap):
                    t = s_scr[slot, kc * CHUNK:(kc + 1) * CHUNK, lanes]
                    if gap in TILE_GAPS:
                        t = t + bt_ref[h, TILE_GAPS.index(gap)]
                    return t

                m = None
                for kc, gap in live:
                    t = tile(kc, gap)
                    m = t if m is None else jnp.maximum(m, t)
                m = jnp.max(m, axis=0, keepdims=True)
                l = None
                for kc, gap in live:
                    p = jnp.exp(tile(kc, gap) - m)
                    l = p if l is None else l + p
                    p_scr[slot, kc * CHUNK:(kc + 1) * CHUNK, lanes] = p.astype(BF16)
                inv_l.append(1.0 / jnp.sum(l, axis=0, keepdims=True))
            return jnp.concatenate(inv_l, axis=1)

        def weighted_values(h, inv_l):
            rows = slice(h * HEAD_DIM, (h + 1) * HEAD_DIM)
            acc = None
            for blk in range(k_rows // KEY_BLOCK):
                part = jnp.dot(vt_ref[vt_start + blk, rows, :],
                               p_scr[h % 2, blk * KEY_BLOCK:(blk + 1) * KEY_BLOCK, :],
                               preferred_element_type=F32)
                acc = part if acc is None else acc + part
            ot_scr[rows, :] = acc * inv_l

        scores(0)
        inv_prev = None
        for h in range(n_heads):
            if h + 1 < n_heads:
                scores(h + 1)
            if h > 0:
                weighted_values(h - 1, inv_prev)
            inv_prev = softmax(h)
        weighted_values(n_heads - 1, inv_prev)
        o_ref[0, q_rows, :] = ot_scr[...].T

    ramp = PAD_ROWS // Q_BLOCK
    for jj in range(min(ramp, n_blocks)):
        block(jj * (Q_BLOCK // CHUNK), jj * Q_BLOCK, 0, 0)

    def full_block(j, carry):
        block(LEFT_CHUNKS, pl.multiple_of(j * Q_BLOCK, Q_BLOCK),
              pl.multiple_of((j - ramp) * Q_BLOCK, Q_BLOCK), j - ramp)
        return carry

    lax.fori_loop(ramp, n_blocks, full_block, 0)


def _rel_bias_matrix(table, rows, cols, dist00):
    n = rows + cols - 1
    dist = np.arange(n) - (cols - 1) + dist00
    g = table[:, np.clip(dist, -MAX_REL, MAX_REL) + MAX_REL].astype(F32)
    period = rows + cols
    g = jnp.pad(g, ((0, 0), (0, period - n)))
    flat = jnp.tile(g, (1, rows + 1))[:, :rows * (period + 1)]
    hankel = flat.reshape(-1, rows, period + 1)[:, :, :cols]
    return hankel[:, :, ::-1]


def _prompt_bias_tiles(table):
    far = table[:, 2 * MAX_REL].astype(F32)[:, None, None]

    def chunk_pair(gap):
        kind = _gap_kind(gap)
        if kind == "near":
            return jnp.swapaxes(_rel_bias_matrix(table, CHUNK, CHUNK, gap * CHUNK), 1, 2) - far
        fill = NEG_BIG if kind == "masked" else 0.0
        return jnp.full((table.shape[0], CHUNK, CHUNK), fill, F32)

    tiles = [jnp.concatenate([chunk_pair(g), chunk_pair(g + 1)], axis=2) for g in TILE_GAPS]
    return jnp.stack(tiles, axis=1)


def _attend_prompt(q, k, v_t, table):
    b, t, w = q.shape
    n_heads = w // HEAD_DIM
    assert t % Q_BLOCK == 0 and PAD_ROWS % Q_BLOCK == 0 and LANES // CHUNK == 2 and Q_BLOCK % LANES == 0
    n_blocks = t // Q_BLOCK
    k_rows = PAD_ROWS + Q_BLOCK
    tiles = _prompt_bias_tiles(table)
    seq = pl.BlockSpec((1, t, w), lambda i: (i, 0, 0))
    return pl.pallas_call(
        functools.partial(_attn_prompt_kernel, n_heads=n_heads, n_blocks=n_blocks),
        grid=(b,),
        in_specs=[seq, seq, pl.BlockSpec((n_blocks, w, KEY_BLOCK), lambda i: (i, 0, 0)),
                  _const_spec(tiles.shape)],
        out_specs=seq,
        out_shape=jax.ShapeDtypeStruct((b, t, w), F32),
        scratch_shapes=[pltpu.VMEM((2, k_rows, Q_BLOCK), F32), pltpu.VMEM((2, k_rows, Q_BLOCK), BF16),
                        pltpu.VMEM((w, Q_BLOCK), F32)],
        compiler_params=_params(("arbitrary",)),
        name="attn_prompt",
    )(q, k, v_t, tiles)


def _attn_sample_kernel(q_ref, k_ref, v_ref, ckt_ref, cvt_ref, bc_ref, bn_ref, o_ref, *, n_heads):
    for h in range(n_heads):
        hs = slice(h * HEAD_DIM, (h + 1) * HEAD_DIM)
        qh = q_ref[0, :, hs]
        s_c = jnp.dot(qh, ckt_ref[0, hs, :].astype(BF16), preferred_element_type=F32) + bc_ref[h]
        s_n = lax.dot_general(qh, k_ref[0, :, hs], NT_DIMS, preferred_element_type=F32) + bn_ref[h]
        m = jnp.maximum(jnp.max(s_c, axis=-1, keepdims=True), jnp.max(s_n, axis=-1, keepdims=True))
        p_c = jnp.exp(s_c - m)
        p_n = jnp.exp(s_n - m)
        l = jnp.sum(p_c, axis=-1, keepdims=True) + jnp.sum(p_n, axis=-1, keepdims=True)
        o = lax.dot_general(p_c.astype(BF16), cvt_ref[0, hs, :].astype(BF16), NT_DIMS,
                            preferred_element_type=F32)
        o = o + jnp.dot(p_n.astype(BF16), v_ref[0, :, hs], preferred_element_type=F32)
        o_ref[0, :, hs] = o / l


def _attend_sample(q, k, v, k_cache, v_cache, table):
    b, t, w = q.shape
    n_heads = w // HEAD_DIM
    r = k_cache.shape[1]
    bias_c = _rel_bias_matrix(table, t, r, r)
    bias_n = _rel_bias_matrix(table, t, t, 0)
    channel_major = lambda c: jnp.transpose(c, (0, 2, 3, 1)).reshape(b, w, r)
    new = pl.BlockSpec((1, t, w), lambda i: (i, 0, 0))
    old = pl.BlockSpec((1, w, r), lambda i: (i, 0, 0))
    return pl.pallas_call(
        functools.partial(_attn_sample_kernel, n_heads=n_heads),
        grid=(b,),
        in_specs=[new, new, new, old, old, _const_spec(bias_c.shape), _const_spec(bias_n.shape)],
        out_specs=new,
        out_shape=jax.ShapeDtypeStruct((b, t, w), F32),
        compiler_params=_params(("arbitrary",)),
        name="attn_sample",
    )(q, k, v, channel_major(k_cache), channel_major(v_cache), bias_c, bias_n)


def _zero_token(v):
    bits = pltpu.bitcast(v[:SUBLANES, :LANES], jnp.uint32)
    zero = lax.shift_right_logical(lax.shift_right_logical(bits, jnp.uint32(31)), jnp.uint32(1))
    return pltpu.bitcast(zero, F32)[0:1, :]


def _out_ffn_steps(y_lru, x_ref, ya_ref, gl_ref, ga_ref, wo_ref, gf_ref, wg_ref, wu_ref, wd_ref, gfin_ref,
                   o_ref, xn_s, hmid_s, acc_s, final_norm, side_work=()):
    lru_w = y_lru.shape[1]
    d_ff = wg_ref.shape[1]
    n_tiles = d_ff // FFN_TILE
    nl = _rms(y_lru, gl_ref[...]).astype(BF16)
    na = _rms(ya_ref[...], ga_ref[...]).astype(BF16)
    mix = jnp.dot(nl, wo_ref[:lru_w, :], preferred_element_type=F32)
    mix = mix + jnp.dot(na, wo_ref[lru_w:, :], preferred_element_type=F32)
    x1 = x_ref[...] + mix
    acc_s[...] = x1
    xn_s[...] = _rms(x1, gf_ref[...]).astype(BF16)

    def gate_up(c):
        cols = slice(c * FFN_TILE, (c + 1) * FFN_TILE)
        xn = xn_s[...]
        return (jnp.dot(xn, wg_ref[:, cols], preferred_element_type=F32),
                jnp.dot(xn, wu_ref[:, cols], preferred_element_type=F32))

    side_work = list(side_work)
    pending = gate_up(0)
    for c in range(n_tiles):
        upcoming = gate_up(c + 1) if c + 1 < n_tiles else None
        g, up = pending
        for _ in range(-(-len(side_work) // (n_tiles - c))):
            side_work.pop(0)(_zero_token(g))
        hmid_s[:, c * FFN_TILE:(c + 1) * FFN_TILE] = (g * _sigmoid(g) * up).astype(BF16)
        if (c + 1) % DOWN_GROUP == 0 or c + 1 == n_tiles:
            k0 = (c // DOWN_GROUP) * DOWN_GROUP * FFN_TILE
            ks = slice(k0, (c + 1) * FFN_TILE)
            acc_s[...] += jnp.dot(hmid_s[:, ks], wd_ref[ks, :], preferred_element_type=F32)
        pending = upcoming
    assert not side_work
    o_ref[...] = _rms(acc_s[...], gfin_ref[...]) if final_norm else acc_s[...]


def _out_ffn_kernel(x_ref, yl_ref, ya_ref, gl_ref, ga_ref, wo_ref, gf_ref, wg_ref, wu_ref, wd_ref,
                    gfin_ref, o_ref, xn_s, hmid_s, acc_s, *, final_norm):
    _out_ffn_steps(yl_ref[...], x_ref, ya_ref, gl_ref, ga_ref, wo_ref, gf_ref, wg_ref, wu_ref, wd_ref,
                   gfin_ref, o_ref, xn_s, hmid_s, acc_s, final_norm)


def _ffn_scratch(tm, d, d_ff):
    return [pltpu.VMEM((tm, d), BF16), pltpu.VMEM((tm, d_ff), BF16), pltpu.VMEM((tm, d), F32)]


def _out_ffn(x2d, y_lru, y_att, gl, ga, wo, gf, wg, wu, wd, gfin, final_norm):
    rows, d = x2d.shape
    tm = min(ROW_TILE, rows)
    assert rows % tm == 0
    lru_w, attn_w, d_ff = y_lru.shape[1], y_att.shape[1], wg.shape[1]
    assert d_ff % FFN_TILE == 0
    row_spec = lambda w: pl.BlockSpec((tm, w), lambda i: (i, 0))
    return pl.pallas_call(
        functools.partial(_out_ffn_kernel, final_norm=final_norm),
        grid=(rows // tm,),
        in_specs=[row_spec(d), row_spec(lru_w), row_spec(attn_w), _const_spec((1, lru_w)),
                  _const_spec((1, attn_w)), _const_spec(wo.shape), _const_spec((1, d)),
                  _const_spec(wg.shape), _const_spec(wu.shape), _const_spec(wd.shape), _const_spec((1, d))],
        out_specs=row_spec(d),
        out_shape=jax.ShapeDtypeStruct((rows, d), F32),
        scratch_shapes=_ffn_scratch(tm, d, d_ff),
        compiler_params=_params(("arbitrary",)),
        name="out_ffn",
    )(x2d, y_lru, y_att, gl, ga, wo, gf, wg, wu, wd, gfin)


def _lru_out_ffn_kernel(x_ref, ya_ref, u_ref, gate_ref, cw_ref, cb_ref, wbd_ref, ba_ref, bx_ref, lam_ref,
                        gl_ref, ga_ref, wo_ref, gf_ref, wg_ref, wu_ref, wd_ref, gfin_ref,
                        o_ref, convn_ref, hlast_ref,
                        ext_s, uc_s, ucb_s, pre_s, y_s, hc_s, xn_s, hmid_s, acc_s,
                        *, tiles_per_seq, n_row_tiles, final_norm):
    s = pl.program_id(0)
    tm, lru_w = u_ref.shape
    hist = CONV_WIDTH - 1
    base = SUBLANES
    n_halves, half = wbd_ref.shape[0], wbd_ref.shape[1]

    @pl.when(s == 0)
    def _():
        y_s[1] = jnp.zeros((tm, lru_w), F32)
        ext_s[base - hist:base, :] = jnp.zeros((hist, lru_w), F32)
        hc_s[...] = jnp.zeros((1, lru_w), F32)

    seq_start = (s % tiles_per_seq) == 0
    ext_s[base - hist:base, :] = jnp.where(seq_start, 0.0, ext_s[base - hist:base, :])
    ext_s[base:base + tm, :] = u_ref[...]
    h_start = jnp.where(seq_start, 0.0, hc_s[...])
    carry = [h_start[:, hf * half:(hf + 1) * half] for hf in range(n_halves)]
    log_sig_lam = _log_sigmoid(lam_ref[...])
    row = lax.broadcasted_iota(jnp.int32, (SUBLANES, half), 0)
    slot = s % 2

    def conv(pb, token):
        rows = slice(pb * LRU_PIECE_ROWS, (pb + 1) * LRU_PIECE_ROWS)
        bias = cb_ref[...] + jnp.tile(token, (1, lru_w // LANES))
        uc = _conv_rows(ext_s, base + pb * LRU_PIECE_ROWS, LRU_PIECE_ROWS, cw_ref[...], bias)
        uc_s[rows, :] = uc
        ucb_s[rows, :] = uc.astype(BF16)

    def gate_matmul(gb, token):
        del token
        rows = slice(gb * LRU_GATE_ROWS, (gb + 1) * LRU_GATE_ROWS)
        for hf in range(n_halves):
            pre_s[rows, hf * 2 * half:(hf + 1) * 2 * half] = jnp.dot(
                ucb_s[rows, hf * half:(hf + 1) * half], wbd_ref[hf], preferred_element_type=F32)

    def recur(pb, hf, token):
        rows = slice(pb * LRU_PIECE_ROWS, (pb + 1) * LRU_PIECE_ROWS)
        cols = slice(hf * half, (hf + 1) * half)
        zero = jnp.tile(token, (1, half // LANES))
        a, b = _gate_terms(pre_s[rows, hf * 2 * half:hf * 2 * half + half],
                           pre_s[rows, hf * 2 * half + half:(hf + 1) * 2 * half],
                           uc_s[rows, cols], ba_ref[:, cols] + zero, bx_ref[:, cols] + zero,
                           log_sig_lam[:, cols])
        h_prev = carry[hf]
        hs = []
        for g in range(LRU_PIECE_ROWS // SUBLANES):
            grp = slice(g * SUBLANES, (g + 1) * SUBLANES)
            h = _scan_group(a[grp], b[grp], h_prev, row)
            h_prev = h[SUBLANES - 1:SUBLANES, :]
            hs.append(h)
        carry[hf] = h_prev
        y_s[slot, rows, cols] = _gelu_tanh(gate_ref[rows, cols]) * jnp.concatenate(hs, axis=0)

    per = LRU_GATE_ROWS // LRU_PIECE_ROWS
    n_gate_blocks = tm // LRU_GATE_ROWS
    side_work = [functools.partial(conv, pb) for pb in range(per)]
    for gb in range(n_gate_blocks):
        side_work.append(functools.partial(gate_matmul, gb))
        for i in range(per):
            if gb + 1 < n_gate_blocks:
                side_work.append(functools.partial(conv, (gb + 1) * per + i))
            side_work += [functools.partial(recur, gb * per + i, hf) for hf in range(n_halves)]

    _out_ffn_steps(y_s[1 - slot], x_ref, ya_ref, gl_ref, ga_ref, wo_ref, gf_ref, wg_ref, wu_ref, wd_ref,
                   gfin_ref, o_ref, xn_s, hmid_s, acc_s, final_norm, side_work)

    tail = ext_s[base + tm - hist:base + tm, :]
    ext_s[base - hist:base, :] = tail
    h_end = jnp.concatenate(carry, axis=1)
    hc_s[...] = h_end

    @pl.when(s < n_row_tiles)
    def _():
        convn_ref[0] = tail
        hlast_ref[0] = h_end


def _lru_out_ffn(x2d, y_att, u, gate, seq_len, p, gfin, final_norm):
    rows, d = x2d.shape
    tm = ROW_TILE
    lru_w, attn_w, d_ff = u.shape[1], y_att.shape[1], p["w_gate"].shape[1]
    assert rows % tm == 0 and seq_len % tm == 0 and d_ff % FFN_TILE == 0
    assert tm % LRU_GATE_ROWS == 0 and LRU_GATE_ROWS % LRU_PIECE_ROWS == 0
    n = rows // tm
    tiles_per_seq = seq_len // tm
    hist = CONV_WIDTH - 1
    n_halves = p["wbd"].shape[0]
    ffn_rows = lambda w: pl.BlockSpec((tm, w), lambda s: (jnp.maximum(s - 1, 0), 0))
    lru_rows = lambda w: pl.BlockSpec((tm, w), lambda s: (jnp.minimum(s, n - 1), 0))
    per_seq = lambda r: pl.BlockSpec((1, r, lru_w), lambda s: (jnp.minimum(s, n - 1) // tiles_per_seq, 0, 0))
    consts = [p["conv_w"], p["conv_b"], p["wbd"], p["lru_ba"], p["lru_bx"], p["lru_lambda"],
              p["norm_lru_out"], p["norm_attn_out"], p["w_out"], p["norm_ffn"], p["w_gate"], p["w_up"],
              p["w_down"], gfin]
    return pl.pallas_call(
        functools.partial(_lru_out_ffn_kernel, tiles_per_seq=tiles_per_seq, n_row_tiles=n, final_norm=final_norm),
        grid=(n + 1,),
        in_specs=[ffn_rows(d), ffn_rows(attn_w), lru_rows(lru_w), lru_rows(lru_w)]
                 + [_const_spec(c.shape) for c in consts],
        out_specs=[ffn_rows(d), per_seq(hist), per_seq(1)],
        out_shape=[jax.ShapeDtypeStruct((rows, d), F32),
                   jax.ShapeDtypeStruct((rows // seq_len, hist, lru_w), F32),
                   jax.ShapeDtypeStruct((rows // seq_len, 1, lru_w), F32)],
        scratch_shapes=[pltpu.VMEM((SUBLANES + tm, lru_w), F32), pltpu.VMEM((tm, lru_w), F32),
                        pltpu.VMEM((tm, lru_w), BF16), pltpu.VMEM((tm, 2 * n_halves * p["wbd"].shape[1]), F32),
                        pltpu.VMEM((2, tm, lru_w), F32), pltpu.VMEM((1, lru_w), F32)]
                       + _ffn_scratch(tm, d, d_ff),
        compiler_params=_params(("arbitrary",)),
        name="lru_out_ffn",
    )(x2d, y_att, u, gate, *consts)


def _block_diag_gates(wa, wx):
    n, c, _ = wa.shape
    per = MXU_DIM // c
    assert n % per == 0
    eye = jnp.eye(per, dtype=wa.dtype)

    def bd(w):
        w = w.reshape(n // per, per, c, c)
        return (eye[None, :, None, :, None] * w[:, :, :, None, :]).reshape(n // per, per * c, per * c)

    return jnp.concatenate([bd(wa), bd(wx)], axis=-1).astype(BF16)


def _layer(x, conv0, h0, prompt, attend, final_gain, p):
    b, t, d = x.shape
    lru_w = p["conv_w"].shape[1]
    attn_w = p["norm_attn_out"].shape[1]
    n_heads = attn_w // HEAD_DIM
    rows = b * t
    x2d = x.reshape(rows, d)
    u, gate, q, k, v, k_tail, v_tail = _in_projection(x2d, p["norm_mix"], p["w_in"], lru_w, attn_w, t, prompt)
    y_att = attend(q.reshape(b, t, attn_w), k.reshape(b, t, attn_w), v if prompt else v.reshape(b, t, attn_w))
    y_att = y_att.reshape(rows, attn_w)
    final_norm = final_gain is not None
    gfin = final_gain if final_norm else p["norm_ffn"]
    if prompt:
        y, conv_new, h_last = _lru_out_ffn(x2d, y_att, u, gate, t, p, gfin, final_norm)
    else:
        y_lru, conv_new, h_last = _lru_group(u.reshape(b, t, lru_w), gate.reshape(b, t, lru_w), conv0, h0,
                                             p["conv_w"], p["conv_b"], p["wbd"], p["lru_ba"], p["lru_bx"],
                                             p["lru_lambda"])
        y = _out_ffn(x2d, y_lru.reshape(rows, lru_w), y_att, p["norm_lru_out"], p["norm_attn_out"],
                     p["w_out"], p["norm_ffn"], p["w_gate"], p["w_up"], p["w_down"], gfin, final_norm)
    if prompt:
        keep = k_tail.shape[2]
        k_tail = k_tail.reshape(b, n_heads, HEAD_DIM, keep).transpose(0, 3, 1, 2)
        v_tail = v_tail.reshape(b, n_heads, HEAD_DIM, keep).transpose(0, 3, 1, 2)
    else:
        k_tail = k_tail.reshape(b, t, n_heads, HEAD_DIM)
        v_tail = v_tail.reshape(b, t, n_heads, HEAD_DIM)
    return y.reshape(b, t, d), conv_new, h_last.reshape(b, lru_w), k_tail, v_tail


def kernel(x_prompt, x_sample, state_conv, state_lru, cache_k, cache_v, norm_mix, w_in, conv_w, conv_b, lru_wa, lru_ba, lru_wx, lru_bx, lru_lambda, rel_bias, norm_lru_out, norm_attn_out, w_out, norm_ffn, w_gate, w_up, w_down, norm_final):
    depth = w_in.shape[0]
    xp, xs = x_prompt, x_sample
    row = lambda a: a.reshape(1, -1)
    outs = [[] for _ in range(8)]
    for l in range(depth):
        p = dict(
            norm_mix=row(norm_mix[l]), w_in=w_in[l].astype(BF16), conv_w=conv_w[l], conv_b=row(conv_b[l]),
            wbd=_block_diag_gates(lru_wa[l], lru_wx[l]), lru_ba=row(lru_ba[l]), lru_bx=row(lru_bx[l]),
            lru_lambda=row(lru_lambda[l]), norm_lru_out=row(norm_lru_out[l]),
            norm_attn_out=row(norm_attn_out[l]), w_out=w_out[l].astype(BF16), norm_ffn=row(norm_ffn[l]),
            w_gate=w_gate[l].astype(BF16), w_up=w_up[l].astype(BF16), w_down=w_down[l].astype(BF16))
        final_gain = row(norm_final) if l == depth - 1 else None
        att_p = functools.partial(_attend_prompt, table=rel_bias[l])
        xp, cp, hp, kp, vp = _layer(xp, None, None, True, att_p, final_gain, p)
        att_s = functools.partial(_attend_sample, k_cache=cache_k[l], v_cache=cache_v[l], table=rel_bias[l])
        xs, cs, hs, ksn, vsn = _layer(xs, state_conv[l], state_lru[l], False, att_s, final_gain, p)
        for lst, val in zip(outs, (cp, hp, kp, vp, cs, hs, ksn, vsn)):
            lst.append(val)
    return (xp, xs) + tuple(jnp.stack(o) for o in outs)
```

```python
import functools
import math

import jax
import jax.numpy as jnp
import numpy as np
from jax import lax
from jax.experimental import pallas as pl
from jax.experimental.pallas import tpu as pltpu

F32 = jnp.float32
BF16 = jnp.bfloat16

EPS = 1e-6
CHUNK = 64
LEFT_CHUNKS = 8
PAD_ROWS = LEFT_CHUNKS * CHUNK
MAX_REL = 128
HEAD_DIM = 64
N_LRU_BLOCKS = 8
CONV_WIDTH = 4
LRU_C = 8.0
NEG_BIG = -1e30

ROW_TILE = 512
LRU_TIME_TILE = 512
Q_BLOCK = 256
KEY_BLOCK = Q_BLOCK
FFN_TILE = 256
DOWN_GROUP = 4
LRU_GATE_ROWS = 256
LRU_PIECE_ROWS = 64
SUBLANES = 8
MXU_DIM = 256
LANES = 128
VMEM_LIMIT = 56 * 1024 * 1024

NT_DIMS = (((1,), (1,)), ((), ()))

NEAR_GAPS = -(-(MAX_REL + CHUNK - 1) // CHUNK)
TILE_GAPS = (-1,) + tuple(range(NEAR_GAPS)) + (LEFT_CHUNKS,)


def _params(sem):
    return pltpu.CompilerParams(dimension_semantics=sem, vmem_limit_bytes=VMEM_LIMIT)


def _const_spec(shape):
    nd = len(shape)
    return pl.BlockSpec(shape, lambda *_: (0,) * nd, pipeline_mode=pl.Buffered(1))


def _rms(x, gain):
    return x * lax.rsqrt(jnp.mean(x * x, axis=-1, keepdims=True) + EPS) * gain


def _zero_token(v):
    bits = pltpu.bitcast(v[:SUBLANES, :LANES], jnp.uint32)
    zero = lax.shift_right_logical(lax.shift_right_logical(bits, jnp.uint32(31)), jnp.uint32(1))
    return pltpu.bitcast(zero, F32)[0:1, :]


def _silu_from_half(hx):
    return hx * (jnp.tanh(hx) + 1.0)


def _gelu_tanh_from_half(hx):
    c = math.sqrt(2.0 / math.pi)
    return hx * (1.0 + jnp.tanh(hx * (2.0 * c + (8.0 * 0.044715 * c) * (hx * hx))))


def _in_weights(w, lru_w, attn_w):
    u, gate, q, k, v = jnp.split(w, np.cumsum([lru_w, lru_w, attn_w, attn_w]), axis=1)
    return jnp.concatenate([v, k, q * (HEAD_DIM ** -0.5), gate * 0.5, u], axis=1).astype(BF16)


def _split_columns(proj, lru_w, attn_w):
    c0, c1, c2, c3 = attn_w, 2 * attn_w, 3 * attn_w, 3 * attn_w + lru_w
    return proj[:, :c0], proj[:, c0:c1], proj[:, c1:c2], proj[:, c2:c3], proj[:, c3:]


def _project(x_ref, gain_ref, w_ref, u_ref, gate_ref, q_ref, lru_w, attn_w):
    xn = _rms(x_ref[...], gain_ref[...]).astype(BF16)
    proj = jnp.dot(xn, w_ref[...], preferred_element_type=F32)
    v, k, q, gate, u = _split_columns(proj, lru_w, attn_w)
    u_ref[...] = u
    gate_ref[...] = gate
    q_ref[...] = q.astype(BF16)
    return k, v


def _inproj_prompt_kernel(x_ref, gain_ref, w_ref, u_ref, gate_ref, q_ref, k_ref, vt_ref, ktail_ref,
                          vtail_ref, *, lru_w, attn_w, tiles_per_seq):
    k, v = _project(x_ref, gain_ref, w_ref, u_ref, gate_ref, q_ref, lru_w, attn_w)
    k_ref[...] = k.astype(BF16)
    v_t = v.T
    for blk in range(vt_ref.shape[0]):
        vt_ref[blk] = v_t[:, blk * KEY_BLOCK:(blk + 1) * KEY_BLOCK].astype(BF16)

    @pl.when(pl.program_id(0) % tiles_per_seq == tiles_per_seq - 1)
    def _():
        ktail_ref[0] = k.T
        vtail_ref[0] = v_t


def _inproj_sample_kernel(x_ref, gain_ref, w_ref, u_ref, gate_ref, q_ref, k_ref, v_ref, kt_ref, vt_ref,
                          *, lru_w, attn_w):
    k, v = _project(x_ref, gain_ref, w_ref, u_ref, gate_ref, q_ref, lru_w, attn_w)
    k_ref[...] = k.astype(BF16)
    v_ref[...] = v.astype(BF16)
    tm = k.shape[0]
    n_heads = attn_w // HEAD_DIM
    for h in range(n_heads):
        hs = slice(h * HEAD_DIM, (h + 1) * HEAD_DIM)
        kt_ref[pl.ds(h, tm, stride=n_heads), :] = k[:, hs]
        vt_ref[pl.ds(h, tm, stride=n_heads), :] = v[:, hs]


def _in_projection(x2d, gain, w_bf16, lru_w, attn_w, seq_len, prompt):
    rows, d = x2d.shape
    tm = min(ROW_TILE, rows)
    assert rows % tm == 0
    n_tiles = rows // tm
    n_cols = w_bf16.shape[1]
    row_spec = lambda w: pl.BlockSpec((tm, w), lambda i: (i, 0))
    in_specs = [row_spec(d), _const_spec((1, d)), _const_spec((d, n_cols))]
    common_specs = [row_spec(lru_w), row_spec(lru_w), row_spec(attn_w), row_spec(attn_w)]
    common_shapes = [jax.ShapeDtypeStruct((rows, lru_w), F32), jax.ShapeDtypeStruct((rows, lru_w), F32),
                     jax.ShapeDtypeStruct((rows, attn_w), BF16), jax.ShapeDtypeStruct((rows, attn_w), BF16)]
    if prompt:
        assert seq_len % tm == 0 and tm % KEY_BLOCK == 0 and tm == min(PAD_ROWS, seq_len)
        tiles_per_seq = seq_len // tm
        blocks = tm // KEY_BLOCK
        tail_spec = pl.BlockSpec((1, attn_w, tm), lambda i: (i // tiles_per_seq, 0, 0))
        tail_shape = jax.ShapeDtypeStruct((rows // seq_len, attn_w, tm), F32)
        body = functools.partial(_inproj_prompt_kernel, lru_w=lru_w, attn_w=attn_w, tiles_per_seq=tiles_per_seq)
        out_specs = common_specs + [pl.BlockSpec((blocks, attn_w, KEY_BLOCK), lambda i: (i, 0, 0)),
                                    tail_spec, tail_spec]
        out_shape = common_shapes + [jax.ShapeDtypeStruct((rows // KEY_BLOCK, attn_w, KEY_BLOCK), BF16),
                                     tail_shape, tail_shape]
    else:
        n_heads = attn_w // HEAD_DIM
        tail_spec = pl.BlockSpec((tm * n_heads, HEAD_DIM), lambda i: (i, 0))
        tail_shape = jax.ShapeDtypeStruct((rows * n_heads, HEAD_DIM), F32)
        body = functools.partial(_inproj_sample_kernel, lru_w=lru_w, attn_w=attn_w)
        out_specs = common_specs + [row_spec(attn_w), tail_spec, tail_spec]
        out_shape = common_shapes + [jax.ShapeDtypeStruct((rows, attn_w), BF16), tail_shape, tail_shape]
    return pl.pallas_call(
        body, grid=(n_tiles,), in_specs=in_specs, out_specs=out_specs, out_shape=out_shape,
        compiler_params=_params(("arbitrary",)), name="in_projection",
    )(x2d, gain, w_bf16)


def _log_sigmoid(x):
    return jnp.minimum(x, 0.0) - jnp.log1p(jnp.exp(-jnp.abs(x)))


def _conv_rows(ext_s, r0, n, w, bias):
    hist = CONV_WIDTH - 1
    uc = bias + ext_s[r0:r0 + n, :] * w[hist:hist + 1]
    for kk in range(hist):
        uc = uc + ext_s[r0 - hist + kk:r0 - hist + kk + n, :] * w[kk:kk + 1]
    return uc


def _gate_terms(half_pre_a, half_pre_x, half_uc, half_ba, half_bx, half_c_log_sig_lam):
    t_r = jnp.tanh(half_pre_a + half_ba)
    t_i = jnp.tanh(half_pre_x + half_bx)
    a = jnp.exp(half_c_log_sig_lam * t_r + half_c_log_sig_lam)
    gain = jnp.sqrt(1.0 - a * a)
    return a, gain * (half_uc * (t_i + 1.0))


def _scan_group(a, b, h_prev, row):
    sh = 1
    while sh < SUBLANES:
        keep = row >= sh
        a_sh = jnp.where(keep, pltpu.roll(a, sh, 0), 1.0)
        b_sh = jnp.where(keep, pltpu.roll(b, sh, 0), 0.0)
        b = a * b_sh + b
        a = a * a_sh
        sh *= 2
    return a * h_prev + b


def _lru_kernel(u_ref, gate_ref, conv0_ref, h0_ref, cw_ref, cb_ref, wbd_ref, ba_ref, bx_ref, lam_ref,
                y_ref, convn_ref, hlast_ref, ext_s, a_s, b_s, h_s, hc_s, *, tc):
    c = pl.program_id(1)
    hist = CONV_WIDTH - 1
    base = SUBLANES

    @pl.when(c == 0)
    def _():
        ext_s[base - hist:base, :] = conv0_ref[0]
        hc_s[...] = h0_ref[0]

    ext_s[base:base + tc, :] = u_ref[0]
    uc = _conv_rows(ext_s, base, tc, cw_ref[...], cb_ref[...])
    tail = ext_s[base + tc - hist:base + tc, :]
    ext_s[base - hist:base, :] = tail
    convn_ref[0] = tail

    half_c_lsl = (0.5 * LRU_C) * _log_sigmoid(lam_ref[...])
    ucb = uc.astype(BF16)
    half = wbd_ref.shape[1]
    for hf in range(wbd_ref.shape[0]):
        cols = slice(hf * half, (hf + 1) * half)
        pre = jnp.dot(ucb[:, cols], wbd_ref[hf], preferred_element_type=F32)
        a, b = _gate_terms(pre[:, :half], pre[:, half:], uc[:, cols], ba_ref[:, cols], bx_ref[:, cols],
                           half_c_lsl[:, cols])
        a_s[:, cols] = a
        b_s[:, cols] = b

    row = lax.broadcasted_iota(jnp.int32, (SUBLANES, a_s.shape[1]), 0)

    def group(gi, h_prev):
        r0 = pl.multiple_of(gi * SUBLANES, SUBLANES)
        h = _scan_group(a_s[pl.ds(r0, SUBLANES), :], b_s[pl.ds(r0, SUBLANES), :], h_prev, row)
        h_s[pl.ds(r0, SUBLANES), :] = h
        return h[SUBLANES - 1:SUBLANES, :]

    h_last = lax.fori_loop(0, tc // SUBLANES, group, hc_s[...])
    hc_s[...] = h_last
    hlast_ref[0] = h_last
    y_ref[0] = _gelu_tanh_from_half(gate_ref[0]) * h_s[...]


def _lru_group(u, gate, conv0, h0, conv_w, conv_b, wbd, ba, bx, lam):
    b, t, w = u.shape
    tc = min(LRU_TIME_TILE, t)
    assert t % tc == 0 and tc % SUBLANES == 0 and tc >= CONV_WIDTH - 1
    hist = CONV_WIDTH - 1
    seq_spec = pl.BlockSpec((1, tc, w), lambda i, c: (i, c, 0))
    per_batch = lambda r: pl.BlockSpec((1, r, w), lambda i, c: (i, 0, 0))
    return pl.pallas_call(
        functools.partial(_lru_kernel, tc=tc),
        grid=(b, t // tc),
        in_specs=[seq_spec, seq_spec, per_batch(hist), per_batch(1),
                  _const_spec(conv_w.shape), _const_spec((1, w)), _const_spec(wbd.shape),
                  _const_spec((1, w)), _const_spec((1, w)), _const_spec((1, w))],
        out_specs=[seq_spec, per_batch(hist), per_batch(1)],
        out_shape=[jax.ShapeDtypeStruct((b, t, w), F32), jax.ShapeDtypeStruct((b, hist, w), F32),
                   jax.ShapeDtypeStruct((b, 1, w), F32)],
        scratch_shapes=[pltpu.VMEM((SUBLANES + tc, w), F32), pltpu.VMEM((tc, w), F32),
                        pltpu.VMEM((tc, w), F32), pltpu.VMEM((tc, w), F32), pltpu.VMEM((1, w), F32)],
        compiler_params=_params(("arbitrary", "arbitrary")),
        name="rg_lru",
    )(u, gate, conv0, h0.reshape(b, 1, w), conv_w, conv_b, wbd, ba, bx, lam)


def _gap_kind(gap):
    if gap < 0 or gap > LEFT_CHUNKS:
        return "masked"
    return "near" if gap < NEAR_GAPS else "far"


def _attn_prompt_kernel(q_ref, k_ref, vt_ref, bt_ref, o_ref, s_scr, p_scr, ot_scr, *, n_heads, n_blocks):
    lane = lax.broadcasted_iota(jnp.int32, (1, LANES), 1)
    head_lanes = (lane < HEAD_DIM, lane >= HEAD_DIM)
    q_chunks_per_tile = LANES // CHUNK
    zero_tile = jnp.zeros((CHUNK, LANES), BF16)

    def block(lead, q_start, k_start, vt_start):
        k_rows = lead * CHUNK + Q_BLOCK
        q_rows = pl.ds(q_start, Q_BLOCK)
        plan = []
        for lt in range(Q_BLOCK // LANES):
            lanes = slice(lt * LANES, (lt + 1) * LANES)
            live = []
            for kc in range(k_rows // CHUNK):
                gap = lt * q_chunks_per_tile + lead - kc
                if _gap_kind(gap) == "masked" and _gap_kind(gap + 1) == "masked":
                    p_scr[0, kc * CHUNK:(kc + 1) * CHUNK, lanes] = zero_tile
                    p_scr[1, kc * CHUNK:(kc + 1) * CHUNK, lanes] = zero_tile
                else:
                    live.append((kc, gap))
            plan.append((lanes, live))

        def scores(h):
            pair = slice((h // 2) * LANES, (h // 2 + 1) * LANES)
            qm = jnp.where(head_lanes[h % 2], q_ref[0, q_rows, pair], jnp.zeros((), BF16))
            for r0 in range(0, k_rows, k_rows // 2):
                kw = k_ref[0, pl.ds(k_start + r0, k_rows // 2), pair]
                s_scr[h % 2, r0:r0 + k_rows // 2, :] = lax.dot_general(kw, qm, NT_DIMS,
                                                                       preferred_element_type=F32)

        def softmax(h):
            slot = h % 2
            inv_l = []
            for lanes, live in plan:
                def tile(kc, gap):
                    t = s_scr[slot, kc * CHUNK:(kc + 1) * CHUNK, lanes]
                    if gap in TILE_GAPS:
                        t = t + bt_ref[h, TILE_GAPS.index(gap)]
                    return t

                m = None
                for kc, gap in live:
                    t = tile(kc, gap)
                    m = t if m is None else jnp.maximum(m, t)
                m = jnp.max(m, axis=0, keepdims=True)
                l = None
                for kc, gap in live:
                    p = jnp.exp(tile(kc, gap) - m)
                    l = p if l is None else l + p
                    p_scr[slot, kc * CHUNK:(kc + 1) * CHUNK, lanes] = p.astype(BF16)
                inv_l.append(1.0 / jnp.sum(l, axis=0, keepdims=True))
            return jnp.concatenate(inv_l, axis=1)

        def weighted_values(h, inv_l):
            rows = slice(h * HEAD_DIM, (h + 1) * HEAD_DIM)
            acc = None
            for blk in range(k_rows // KEY_BLOCK):
                part = jnp.dot(vt_ref[vt_start + blk, rows, :],
                               p_scr[h % 2, blk * KEY_BLOCK:(blk + 1) * KEY_BLOCK, :],
                               preferred_element_type=F32)
                acc = part if acc is None else acc + part
            ot_scr[rows, :] = acc * inv_l

        scores(0)
        inv_prev = None
        for h in range(n_heads):
            if h + 1 < n_heads:
                scores(h + 1)
            if h > 0:
                weighted_values(h - 1, inv_prev)
            inv_prev = softmax(h)
        weighted_values(n_heads - 1, inv_prev)
        o_ref[0, q_rows, :] = ot_scr[...].T

    ramp = PAD_ROWS // Q_BLOCK
    for jj in range(min(ramp, n_blocks)):
        block(jj * (Q_BLOCK // CHUNK), jj * Q_BLOCK, 0, 0)

    def full_block(j, carry):
        block(LEFT_CHUNKS, pl.multiple_of(j * Q_BLOCK, Q_BLOCK),
              pl.multiple_of((j - ramp) * Q_BLOCK, Q_BLOCK), j - ramp)
        return carry

    lax.fori_loop(ramp, n_blocks, full_block, 0)


def _rel_bias_matrix(table, rows, cols, dist00):
    n = rows + cols - 1
    dist = np.arange(n) - (cols - 1) + dist00
    g = table[:, np.clip(dist, -MAX_REL, MAX_REL) + MAX_REL].astype(F32)
    period = rows + cols
    g = jnp.pad(g, ((0, 0), (0, period - n)))
    flat = jnp.tile(g, (1, rows + 1))[:, :rows * (period + 1)]
    hankel = flat.reshape(-1, rows, period + 1)[:, :, :cols]
    return hankel[:, :, ::-1]


def _prompt_bias_tiles(table):
    far = table[:, 2 * MAX_REL].astype(F32)[:, None, None]

    def chunk_pair(gap):
        kind = _gap_kind(gap)
        if kind == "near":
            return jnp.swapaxes(_rel_bias_matrix(table, CHUNK, CHUNK, gap * CHUNK), 1, 2) - far
        fill = NEG_BIG if kind == "masked" else 0.0
        return jnp.full((table.shape[0], CHUNK, CHUNK), fill, F32)

    tiles = [jnp.concatenate([chunk_pair(g), chunk_pair(g + 1)], axis=2) for g in TILE_GAPS]
    return jnp.stack(tiles, axis=1)


def _attend_prompt(q, k, v_t, table):
    b, t, w = q.shape
    n_heads = w // HEAD_DIM
    assert t % Q_BLOCK == 0 and PAD_ROWS % Q_BLOCK == 0 and LANES // CHUNK == 2 and Q_BLOCK % LANES == 0
    n_blocks = t // Q_BLOCK
    k_rows = PAD_ROWS + Q_BLOCK
    tiles = _prompt_bias_tiles(table)
    seq = pl.BlockSpec((1, t, w), lambda i: (i, 0, 0))
    return pl.pallas_call(
        functools.partial(_attn_prompt_kernel, n_heads=n_heads, n_blocks=n_blocks),
        grid=(b,),
        in_specs=[seq, seq, pl.BlockSpec((n_blocks, w, KEY_BLOCK), lambda i: (i, 0, 0)),
                  _const_spec(tiles.shape)],
        out_specs=seq,
        out_shape=jax.ShapeDtypeStruct((b, t, w), F32),
        scratch_shapes=[pltpu.VMEM((2, k_rows, Q_BLOCK), F32), pltpu.VMEM((2, k_rows, Q_BLOCK), BF16),
                        pltpu.VMEM((w, Q_BLOCK), F32)],
        compiler_params=_params(("arbitrary",)),
        name="attn_prompt",
    )(q, k, v_t, tiles)


def _attn_sample_kernel(q_ref, k_ref, v_ref, ckt_ref, cvt_ref, bc_ref, bn_ref, o_ref, *, n_heads):
    for h in range(n_heads):
        hs = slice(h * HEAD_DIM, (h + 1) * HEAD_DIM)
        qh = q_ref[0, :, hs]
        s_c = jnp.dot(qh, ckt_ref[0, hs, :].astype(BF16), preferred_element_type=F32) + bc_ref[h]
        s_n = lax.dot_general(qh, k_ref[0, :, hs], NT_DIMS, preferred_element_type=F32) + bn_ref[h]
        m = jnp.maximum(jnp.max(s_c, axis=-1, keepdims=True), jnp.max(s_n, axis=-1, keepdims=True))
        p_c = jnp.exp(s_c - m)
        p_n = jnp.exp(s_n - m)
        l = jnp.sum(p_c, axis=-1, keepdims=True) + jnp.sum(p_n, axis=-1, keepdims=True)
        o = lax.dot_general(p_c.astype(BF16), cvt_ref[0, hs, :].astype(BF16), NT_DIMS,
                            preferred_element_type=F32)
        o = o + jnp.dot(p_n.astype(BF16), v_ref[0, :, hs], preferred_element_type=F32)
        o_ref[0, :, hs] = o / l


def _attend_sample(q, k, v, k_cache, v_cache, table):
    b, t, w = q.shape
    n_heads = w // HEAD_DIM
    r = k_cache.shape[1]
    bias_c = _rel_bias_matrix(table, t, r, r)
    bias_n = _rel_bias_matrix(table, t, t, 0)
    channel_major = lambda c: jnp.transpose(c, (0, 2, 3, 1)).reshape(b, w, r)
    new = pl.BlockSpec((1, t, w), lambda i: (i, 0, 0))
    old = pl.BlockSpec((1, w, r), lambda i: (i, 0, 0))
    return pl.pallas_call(
        functools.partial(_attn_sample_kernel, n_heads=n_heads),
        grid=(b,),
        in_specs=[new, new, new, old, old, _const_spec(bias_c.shape), _const_spec(bias_n.shape)],
        out_specs=new,
        out_shape=jax.ShapeDtypeStruct((b, t, w), F32),
        compiler_params=_params(("arbitrary",)),
        name="attn_sample",
    )(q, k, v, channel_major(k_cache), channel_major(v_cache), bias_c, bias_n)


def _out_ffn_steps(y_lru, x_ref, ya_ref, gl_ref, ga_ref, wo_ref, gf_ref, wg_ref, wu_ref, wd_ref, gfin_ref,
                   o_ref, xn_s, hmid_s, acc_s, final_norm, side_work=()):
    lru_w = y_lru.shape[1]
    d_ff = wg_ref.shape[1]
    n_tiles = d_ff // FFN_TILE
    nl = _rms(y_lru, gl_ref[...]).astype(BF16)
    na = _rms(ya_ref[...], ga_ref[...]).astype(BF16)
    mix = jnp.dot(nl, wo_ref[:lru_w, :], preferred_element_type=F32)
    mix = mix + jnp.dot(na, wo_ref[lru_w:, :], preferred_element_type=F32)
    x1 = x_ref[...] + mix
    acc_s[...] = x1
    xn_s[...] = _rms(x1, gf_ref[...]).astype(BF16)

    def gate_up(c):
        cols = slice(c * FFN_TILE, (c + 1) * FFN_TILE)
        xn = xn_s[...]
        return (jnp.dot(xn, wg_ref[:, cols], preferred_element_type=F32),
                jnp.dot(xn, wu_ref[:, cols], preferred_element_type=F32))

    side_work = list(side_work)
    pending = gate_up(0)
    for c in range(n_tiles):
        upcoming = gate_up(c + 1) if c + 1 < n_tiles else None
        g, up = pending
        for _ in range(-(-len(side_work) // (n_tiles - c))):
            side_work.pop(0)(_zero_token(g))
        hmid_s[:, c * FFN_TILE:(c + 1) * FFN_TILE] = (_silu_from_half(g) * up).astype(BF16)
        if (c + 1) % DOWN_GROUP == 0 or c + 1 == n_tiles:
            k0 = (c // DOWN_GROUP) * DOWN_GROUP * FFN_TILE
            ks = slice(k0, (c + 1) * FFN_TILE)
            acc_s[...] += jnp.dot(hmid_s[:, ks], wd_ref[ks, :], preferred_element_type=F32)
        pending = upcoming
    assert not side_work
    o_ref[...] = _rms(acc_s[...], gfin_ref[...]) if final_norm else acc_s[...]


def _out_ffn_kernel(x_ref, yl_ref, ya_ref, gl_ref, ga_ref, wo_ref, gf_ref, wg_ref, wu_ref, wd_ref,
                    gfin_ref, o_ref, xn_s, hmid_s, acc_s, *, final_norm):
    _out_ffn_steps(yl_ref[...], x_ref, ya_ref, gl_ref, ga_ref, wo_ref, gf_ref, wg_ref, wu_ref, wd_ref,
                   gfin_ref, o_ref, xn_s, hmid_s, acc_s, final_norm)


def _ffn_scratch(tm, d, d_ff):
    return [pltpu.VMEM((tm, d), BF16), pltpu.VMEM((tm, d_ff), BF16), pltpu.VMEM((tm, d), F32)]


def _out_ffn(x2d, y_lru, y_att, gl, ga, wo, gf, wg, wu, wd, gfin, final_norm):
    rows, d = x2d.shape
    tm = min(ROW_TILE, rows)
    assert rows % tm == 0
    lru_w, attn_w, d_ff = y_lru.shape[1], y_att.shape[1], wg.shape[1]
    assert d_ff % FFN_TILE == 0
    row_spec = lambda w: pl.BlockSpec((tm, w), lambda i: (i, 0))
    return pl.pallas_call(
        functools.partial(_out_ffn_kernel, final_norm=final_norm),
        grid=(rows // tm,),
        in_specs=[row_spec(d), row_spec(lru_w), row_spec(attn_w), _const_spec((1, lru_w)),
                  _const_spec((1, attn_w)), _const_spec(wo.shape), _const_spec((1, d)),
                  _const_spec(wg.shape), _const_spec(wu.shape), _const_spec(wd.shape), _const_spec((1, d))],
        out_specs=row_spec(d),
        out_shape=jax.ShapeDtypeStruct((rows, d), F32),
        scratch_shapes=_ffn_scratch(tm, d, d_ff),
        compiler_params=_params(("arbitrary",)),
        name="out_ffn",
    )(x2d, y_lru, y_att, gl, ga, wo, gf, wg, wu, wd, gfin)


def _lru_out_ffn_kernel(x_ref, ya_ref, u_ref, gate_ref, cw_ref, cb_ref, wbd_ref, ba_ref, bx_ref, lam_ref,
                        gl_ref, ga_ref, wo_ref, gf_ref, wg_ref, wu_ref, wd_ref, gfin_ref,
                        o_ref, convn_ref, hlast_ref,
                        ext_s, uc_s, ucb_s, pre_s, y_s, hc_s, xn_s, hmid_s, acc_s,
                        *, tiles_per_seq, n_row_tiles, final_norm):
    s = pl.program_id(0)
    tm, lru_w = u_ref.shape
    hist = CONV_WIDTH - 1
    base = SUBLANES
    n_halves, half = wbd_ref.shape[0], wbd_ref.shape[1]

    @pl.when(s == 0)
    def _():
        y_s[1] = jnp.zeros((tm, lru_w), F32)
        ext_s[base - hist:base, :] = jnp.zeros((hist, lru_w), F32)
        hc_s[...] = jnp.zeros((1, lru_w), F32)

    seq_start = (s % tiles_per_seq) == 0
    ext_s[base - hist:base, :] = jnp.where(seq_start, 0.0, ext_s[base - hist:base, :])
    ext_s[base:base + tm, :] = u_ref[...]
    h_start = jnp.where(seq_start, 0.0, hc_s[...])
    carry = [h_start[:, hf * half:(hf + 1) * half] for hf in range(n_halves)]
    half_c_lsl = (0.5 * LRU_C) * _log_sigmoid(lam_ref[...])
    row = lax.broadcasted_iota(jnp.int32, (SUBLANES, half), 0)
    slot = s % 2

    def conv(pb, token):
        rows = slice(pb * LRU_PIECE_ROWS, (pb + 1) * LRU_PIECE_ROWS)
        bias = cb_ref[...] + jnp.tile(token, (1, lru_w // LANES))
        uc = _conv_rows(ext_s, base + pb * LRU_PIECE_ROWS, LRU_PIECE_ROWS, cw_ref[...], bias)
        uc_s[rows, :] = uc
        ucb_s[rows, :] = uc.astype(BF16)

    def gate_matmul(gb, token):
        del token
        rows = slice(gb * LRU_GATE_ROWS, (gb + 1) * LRU_GATE_ROWS)
        for hf in range(n_halves):
            pre_s[rows, hf * 2 * half:(hf + 1) * 2 * half] = jnp.dot(
                ucb_s[rows, hf * half:(hf + 1) * half], wbd_ref[hf], preferred_element_type=F32)

    def recur(pb, hf, token):
        rows = slice(pb * LRU_PIECE_ROWS, (pb + 1) * LRU_PIECE_ROWS)
        cols = slice(hf * half, (hf + 1) * half)
        zero = jnp.tile(token, (1, half // LANES))
        a, b = _gate_terms(pre_s[rows, hf * 2 * half:hf * 2 * half + half],
                           pre_s[rows, hf * 2 * half + half:(hf + 1) * 2 * half],
                           uc_s[rows, cols], ba_ref[:, cols] + zero, bx_ref[:, cols] + zero,
                           half_c_lsl[:, cols])
        h_prev = carry[hf]
        hs = []
        for g in range(LRU_PIECE_ROWS // SUBLANES):
            grp = slice(g * SUBLANES, (g + 1) * SUBLANES)
            h = _scan_group(a[grp], b[grp], h_prev, row)
            h_prev = h[SUBLANES - 1:SUBLANES, :]
            hs.append(h)
        carry[hf] = h_prev
        y_s[slot, rows, cols] = _gelu_tanh_from_half(gate_ref[rows, cols]) * jnp.concatenate(hs, axis=0)

    per = LRU_GATE_ROWS // LRU_PIECE_ROWS
    n_gate_blocks = tm // LRU_GATE_ROWS
    side_work = [functools.partial(conv, pb) for pb in range(per)]
    for gb in range(n_gate_blocks):
        side_work.append(functools.partial(gate_matmul, gb))
        for i in range(per):
            if gb + 1 < n_gate_blocks:
                side_work.append(functools.partial(conv, (gb + 1) * per + i))
            side_work += [functools.partial(recur, gb * per + i, hf) for hf in range(n_halves)]

    _out_ffn_steps(y_s[1 - slot], x_ref, ya_ref, gl_ref, ga_ref, wo_ref, gf_ref, wg_ref, wu_ref, wd_ref,
                   gfin_ref, o_ref, xn_s, hmid_s, acc_s, final_norm, side_work)

    tail = ext_s[base + tm - hist:base + tm, :]
    ext_s[base - hist:base, :] = tail
    h_end = jnp.concatenate(carry, axis=1)
    hc_s[...] = h_end

    @pl.when(s < n_row_tiles)
    def _():
        convn_ref[0] = tail
        hlast_ref[0] = h_end


def _lru_out_ffn(x2d, y_att, u, gate, seq_len, p, gfin, final_norm):
    rows, d = x2d.shape
    tm = ROW_TILE
    lru_w, attn_w, d_ff = u.shape[1], y_att.shape[1], p["w_gate"].shape[1]
    assert rows % tm == 0 and seq_len % tm == 0 and d_ff % FFN_TILE == 0
    assert tm % LRU_GATE_ROWS == 0 and LRU_GATE_ROWS % LRU_PIECE_ROWS == 0
    n = rows // tm
    tiles_per_seq = seq_len // tm
    hist = CONV_WIDTH - 1
    n_halves = p["wbd"].shape[0]
    ffn_rows = lambda w: pl.BlockSpec((tm, w), lambda s: (jnp.maximum(s - 1, 0), 0))
    lru_rows = lambda w: pl.BlockSpec((tm, w), lambda s: (jnp.minimum(s, n - 1), 0))
    per_seq = lambda r: pl.BlockSpec((1, r, lru_w), lambda s: (jnp.minimum(s, n - 1) // tiles_per_seq, 0, 0))
    consts = [p["conv_w"], p["conv_b"], p["wbd"], p["lru_ba"], p["lru_bx"], p["lru_lambda"],
              p["norm_lru_out"], p["norm_attn_out"], p["w_out"], p["norm_ffn"], p["w_gate"], p["w_up"],
              p["w_down"], gfin]
    return pl.pallas_call(
        functools.partial(_lru_out_ffn_kernel, tiles_per_seq=tiles_per_seq, n_row_tiles=n, final_norm=final_norm),
        grid=(n + 1,),
        in_specs=[ffn_rows(d), ffn_rows(attn_w), lru_rows(lru_w), lru_rows(lru_w)]
                 + [_const_spec(c.shape) for c in consts],
        out_specs=[ffn_rows(d), per_seq(hist), per_seq(1)],
        out_shape=[jax.ShapeDtypeStruct((rows, d), F32),
                   jax.ShapeDtypeStruct((rows // seq_len, hist, lru_w), F32),
                   jax.ShapeDtypeStruct((rows // seq_len, 1, lru_w), F32)],
        scratch_shapes=[pltpu.VMEM((SUBLANES + tm, lru_w), F32), pltpu.VMEM((tm, lru_w), F32),
                        pltpu.VMEM((tm, lru_w), BF16), pltpu.VMEM((tm, 2 * n_halves * p["wbd"].shape[1]), F32),
                        pltpu.VMEM((2, tm, lru_w), F32), pltpu.VMEM((1, lru_w), F32)]
                       + _ffn_scratch(tm, d, d_ff),
        compiler_params=_params(("arbitrary",)),
        name="lru_out_ffn",
    )(x2d, y_att, u, gate, *consts)


def _block_diag_gates(wa, wx):
    n, c, _ = wa.shape
    per = MXU_DIM // c
    assert n % per == 0
    eye = jnp.eye(per, dtype=wa.dtype)

    def bd(w):
        w = w.reshape(n // per, per, c, c)
        return (eye[None, :, None, :, None] * w[:, :, :, None, :]).reshape(n // per, per * c, per * c)

    return jnp.concatenate([bd(wa), bd(wx)], axis=-1).astype(BF16)


def _layer(x, conv0, h0, prompt, attend, final_gain, p):
    b, t, d = x.shape
    lru_w = p["conv_w"].shape[1]
    attn_w = p["norm_attn_out"].shape[1]
    n_heads = attn_w // HEAD_DIM
    rows = b * t
    x2d = x.reshape(rows, d)
    u, gate, q, k, v, k_tail, v_tail = _in_projection(x2d, p["norm_mix"], p["w_in"], lru_w, attn_w, t, prompt)
    y_att = attend(q.reshape(b, t, attn_w), k.reshape(b, t, attn_w), v if prompt else v.reshape(b, t, attn_w))
    y_att = y_att.reshape(rows, attn_w)
    final_norm = final_gain is not None
    gfin = final_gain if final_norm else p["norm_ffn"]
    if prompt:
        y, conv_new, h_last = _lru_out_ffn(x2d, y_att, u, gate, t, p, gfin, final_norm)
    else:
        y_lru, conv_new, h_last = _lru_group(u.reshape(b, t, lru_w), gate.reshape(b, t, lru_w), conv0, h0,
                                             p["conv_w"], p["conv_b"], p["wbd"], p["lru_ba"], p["lru_bx"],
                                             p["lru_lambda"])
        y = _out_ffn(x2d, y_lru.reshape(rows, lru_w), y_att, p["norm_lru_out"], p["norm_attn_out"],
                     p["w_out"], p["norm_ffn"], p["w_gate"], p["w_up"], p["w_down"], gfin, final_norm)
    if prompt:
        keep = k_tail.shape[2]
        k_tail = k_tail.reshape(b, n_heads, HEAD_DIM, keep).transpose(0, 3, 1, 2)
        v_tail = v_tail.reshape(b, n_heads, HEAD_DIM, keep).transpose(0, 3, 1, 2)
    else:
        k_tail = k_tail.reshape(b, t, n_heads, HEAD_DIM)
        v_tail = v_tail.reshape(b, t, n_heads, HEAD_DIM)
    return y.reshape(b, t, d), conv_new, h_last.reshape(b, lru_w), k_tail, v_tail


def kernel(x_prompt, x_sample, state_conv, state_lru, cache_k, cache_v, norm_mix, w_in, conv_w, conv_b, lru_wa, lru_ba, lru_wx, lru_bx, lru_lambda, rel_bias, norm_lru_out, norm_attn_out, w_out, norm_ffn, w_gate, w_up, w_down, norm_final):
    depth = w_in.shape[0]
    xp, xs = x_prompt, x_sample
    row = lambda a: a.reshape(1, -1)
    outs = [[] for _ in range(8)]
    for l in range(depth):
        lru_w, attn_w = conv_w.shape[-1], norm_attn_out.shape[-1]
        p = dict(
            norm_mix=row(norm_mix[l]), w_in=_in_weights(w_in[l], lru_w, attn_w), conv_w=0.5 * conv_w[l],
            conv_b=row(0.5 * conv_b[l]), wbd=_block_diag_gates(lru_wa[l], lru_wx[l]),
            lru_ba=row(0.5 * lru_ba[l]), lru_bx=row(0.5 * lru_bx[l]),
            lru_lambda=row(lru_lambda[l]), norm_lru_out=row(norm_lru_out[l]),
            norm_attn_out=row(norm_attn_out[l]), w_out=w_out[l].astype(BF16), norm_ffn=row(norm_ffn[l]),
            w_gate=(0.5 * w_gate[l]).astype(BF16), w_up=w_up[l].astype(BF16), w_down=w_down[l].astype(BF16))
        final_gain = row(norm_final) if l == depth - 1 else None
        att_p = functools.partial(_attend_prompt, table=rel_bias[l])
        xp, cp, hp, kp, vp = _layer(xp, None, None, True, att_p, final_gain, p)
        att_s = functools.partial(_attend_sample, k_cache=cache_k[l], v_cache=cache_v[l], table=rel_bias[l])
        xs, cs, hs, ksn, vsn = _layer(xs, state_conv[l], state_lru[l], False, att_s, final_gain, p)
        for lst, val in zip(outs, (cp, hp, kp, vp, cs, hs, ksn, vsn)):
            lst.append(val)
    return (xp, xs) + tuple(jnp.stack(o) for o in outs)
```

```python
import functools
import math

import jax
import jax.numpy as jnp
import numpy as np
from jax import lax
from jax.experimental import pallas as pl
from jax.experimental.pallas import tpu as pltpu

F32 = jnp.float32
BF16 = jnp.bfloat16

EPS = 1e-6
CHUNK = 64
LEFT_CHUNKS = 8
PAD_ROWS = LEFT_CHUNKS * CHUNK
MAX_REL = 128
HEAD_DIM = 64
N_LRU_BLOCKS = 8
CONV_WIDTH = 4
LRU_C = 8.0
NEG_BIG = -1e30

ROW_TILE = 512
LRU_TIME_TILE = 512
Q_BLOCK = 256
KEY_BLOCK = Q_BLOCK
FFN_TILE = 256
DOWN_GROUP = 4
LRU_GATE_ROWS = 256
LRU_PIECE_ROWS = 64
SUBLANES = 8
MXU_DIM = 256
LANES = 128
VMEM_LIMIT = 56 * 1024 * 1024

NT_DIMS = (((1,), (1,)), ((), ()))

NEAR_GAPS = -(-(MAX_REL + CHUNK - 1) // CHUNK)
TILE_GAPS = (-1,) + tuple(range(NEAR_GAPS)) + (LEFT_CHUNKS,)


def _params(sem):
    return pltpu.CompilerParams(dimension_semantics=sem, vmem_limit_bytes=VMEM_LIMIT)


def _const_spec(shape):
    nd = len(shape)
    return pl.BlockSpec(shape, lambda *_: (0,) * nd, pipeline_mode=pl.Buffered(1))


def _rms(x, gain):
    return x * lax.rsqrt(jnp.mean(x * x, axis=-1, keepdims=True) + EPS) * gain


def _zero_token(v):
    bits = pltpu.bitcast(v[:SUBLANES, :LANES], jnp.uint32)
    zero = lax.shift_right_logical(lax.shift_right_logical(bits, jnp.uint32(31)), jnp.uint32(1))
    return pltpu.bitcast(zero, F32)[0:1, :]


def _silu_from_half(hx):
    return hx * (jnp.tanh(hx) + 1.0)


def _gelu_tanh_from_half(hx):
    c = math.sqrt(2.0 / math.pi)
    return hx * (1.0 + jnp.tanh(hx * (2.0 * c + (8.0 * 0.044715 * c) * (hx * hx))))


def _in_weights(w, lru_w, attn_w):
    u, gate, q, k, v = jnp.split(w, np.cumsum([lru_w, lru_w, attn_w, attn_w]), axis=1)
    return jnp.concatenate([v, k, q * (HEAD_DIM ** -0.5), gate * 0.5, u], axis=1).astype(BF16)


def _split_columns(proj, lru_w, attn_w):
    c0, c1, c2, c3 = attn_w, 2 * attn_w, 3 * attn_w, 3 * attn_w + lru_w
    return proj[:, :c0], proj[:, c0:c1], proj[:, c1:c2], proj[:, c2:c3], proj[:, c3:]


def _project(x_ref, gain_ref, w_ref, u_ref, gate_ref, q_ref, lru_w, attn_w):
    xn = _rms(x_ref[...], gain_ref[...]).astype(BF16)
    proj = jnp.dot(xn, w_ref[...], preferred_element_type=F32)
    v, k, q, gate, u = _split_columns(proj, lru_w, attn_w)
    u_ref[...] = u
    gate_ref[...] = gate
    q_ref[...] = q.astype(BF16)
    return k, v


def _inproj_prompt_kernel(x_ref, gain_ref, w_ref, u_ref, gate_ref, q_ref, k_ref, vt_ref, ktail_ref,
                          vtail_ref, *, lru_w, attn_w, tiles_per_seq):
    k, v = _project(x_ref, gain_ref, w_ref, u_ref, gate_ref, q_ref, lru_w, attn_w)
    k_ref[...] = k.astype(BF16)
    v_t = v.T
    for blk in range(vt_ref.shape[0]):
        vt_ref[blk] = v_t[:, blk * KEY_BLOCK:(blk + 1) * KEY_BLOCK].astype(BF16)

    @pl.when(pl.program_id(0) % tiles_per_seq == tiles_per_seq - 1)
    def _():
        ktail_ref[0] = k.T
        vtail_ref[0] = v_t


def _inproj_sample_kernel(x_ref, gain_ref, w_ref, u_ref, gate_ref, q_ref, k_ref, v_ref, kt_ref, vt_ref,
                          *, lru_w, attn_w):
    k, v = _project(x_ref, gain_ref, w_ref, u_ref, gate_ref, q_ref, lru_w, attn_w)
    k_ref[...] = k.astype(BF16)
    v_ref[...] = v.astype(BF16)
    tm = k.shape[0]
    n_heads = attn_w // HEAD_DIM
    for h in range(n_heads):
        hs = slice(h * HEAD_DIM, (h + 1) * HEAD_DIM)
        kt_ref[pl.ds(h, tm, stride=n_heads), :] = k[:, hs]
        vt_ref[pl.ds(h, tm, stride=n_heads), :] = v[:, hs]


def _in_projection(x2d, gain, w_bf16, lru_w, attn_w, seq_len, prompt):
    rows, d = x2d.shape
    tm = min(ROW_TILE, rows)
    assert rows % tm == 0
    n_tiles = rows // tm
    n_cols = w_bf16.shape[1]
    row_spec = lambda w: pl.BlockSpec((tm, w), lambda i: (i, 0))
    in_specs = [row_spec(d), _const_spec((1, d)), _const_spec((d, n_cols))]
    common_specs = [row_spec(lru_w), row_spec(lru_w), row_spec(attn_w), row_spec(attn_w)]
    common_shapes = [jax.ShapeDtypeStruct((rows, lru_w), F32), jax.ShapeDtypeStruct((rows, lru_w), F32),
                     jax.ShapeDtypeStruct((rows, attn_w), BF16), jax.ShapeDtypeStruct((rows, attn_w), BF16)]
    if prompt:
        assert seq_len % tm == 0 and tm % KEY_BLOCK == 0 and tm == min(PAD_ROWS, seq_len)
        tiles_per_seq = seq_len // tm
        blocks = tm // KEY_BLOCK
        tail_spec = pl.BlockSpec((1, attn_w, tm), lambda i: (i // tiles_per_seq, 0, 0))
        tail_shape = jax.ShapeDtypeStruct((rows // seq_len, attn_w, tm), F32)
        body = functools.partial(_inproj_prompt_kernel, lru_w=lru_w, attn_w=attn_w, tiles_per_seq=tiles_per_seq)
        out_specs = common_specs + [pl.BlockSpec((blocks, attn_w, KEY_BLOCK), lambda i: (i, 0, 0)),
                                    tail_spec, tail_spec]
        out_shape = common_shapes + [jax.ShapeDtypeStruct((rows // KEY_BLOCK, attn_w, KEY_BLOCK), BF16),
                                     tail_shape, tail_shape]
    else:
        n_heads = attn_w // HEAD_DIM
        tail_spec = pl.BlockSpec((tm * n_heads, HEAD_DIM), lambda i: (i, 0))
        tail_shape = jax.ShapeDtypeStruct((rows * n_heads, HEAD_DIM), F32)
        body = functools.partial(_inproj_sample_kernel, lru_w=lru_w, attn_w=attn_w)
        out_specs = common_specs + [row_spec(attn_w), tail_spec, tail_spec]
        out_shape = common_shapes + [jax.ShapeDtypeStruct((rows, attn_w), BF16), tail_shape, tail_shape]
    return pl.pallas_call(
        body, grid=(n_tiles,), in_specs=in_specs, out_specs=out_specs, out_shape=out_shape,
        compiler_params=_params(("arbitrary",)), name="in_projection",
    )(x2d, gain, w_bf16)


def _log_sigmoid(x):
    return jnp.minimum(x, 0.0) - jnp.log1p(jnp.exp(-jnp.abs(x)))


def _conv_rows(ext_s, r0, n, w, bias):
    hist = CONV_WIDTH - 1
    uc = bias + ext_s[r0:r0 + n, :] * w[hist:hist + 1]
    for kk in range(hist):
        uc = uc + ext_s[r0 - hist + kk:r0 - hist + kk + n, :] * w[kk:kk + 1]
    return uc


def _gate_terms(half_pre_a, half_pre_x, half_uc, half_ba, half_bx, half_c_log_sig_lam):
    t_r = jnp.tanh(half_pre_a + half_ba)
    t_i = jnp.tanh(half_pre_x + half_bx)
    a = jnp.exp(half_c_log_sig_lam * t_r + half_c_log_sig_lam)
    gain = jnp.sqrt(1.0 - a * a)
    return a, gain * (half_uc * (t_i + 1.0))


def _scan_group(a, b, h_prev, row):
    sh = 1
    while sh < SUBLANES:
        keep = row >= sh
        a_sh = jnp.where(keep, pltpu.roll(a, sh, 0), 1.0)
        b_sh = jnp.where(keep, pltpu.roll(b, sh, 0), 0.0)
        b = a * b_sh + b
        a = a * a_sh
        sh *= 2
    return a * h_prev + b


def _lru_kernel(u_ref, gate_ref, conv0_ref, h0_ref, cw_ref, cb_ref, wbd_ref, ba_ref, bx_ref, lam_ref,
                y_ref, convn_ref, hlast_ref, ext_s, a_s, b_s, h_s, hc_s, *, tc):
    c = pl.program_id(1)
    hist = CONV_WIDTH - 1
    base = SUBLANES

    @pl.when(c == 0)
    def _():
        ext_s[base - hist:base, :] = conv0_ref[0]
        hc_s[...] = h0_ref[0]

    ext_s[base:base + tc, :] = u_ref[0]
    uc = _conv_rows(ext_s, base, tc, cw_ref[...], cb_ref[...])
    tail = ext_s[base + tc - hist:base + tc, :]
    ext_s[base - hist:base, :] = tail
    convn_ref[0] = tail

    half_c_lsl = (0.5 * LRU_C) * _log_sigmoid(lam_ref[...])
    ucb = uc.astype(BF16)
    half = wbd_ref.shape[1]
    for hf in range(wbd_ref.shape[0]):
        cols = slice(hf * half, (hf + 1) * half)
        pre = jnp.dot(ucb[:, cols], wbd_ref[hf], preferred_element_type=F32)
        a, b = _gate_terms(pre[:, :half], pre[:, half:], uc[:, cols], ba_ref[:, cols], bx_ref[:, cols],
                           half_c_lsl[:, cols])
        a_s[:, cols] = a
        b_s[:, cols] = b

    row = lax.broadcasted_iota(jnp.int32, (SUBLANES, a_s.shape[1]), 0)

    def group(gi, h_prev):
        r0 = pl.multiple_of(gi * SUBLANES, SUBLANES)
        h = _scan_group(a_s[pl.ds(r0, SUBLANES), :], b_s[pl.ds(r0, SUBLANES), :], h_prev, row)
        h_s[pl.ds(r0, SUBLANES), :] = h
        return h[SUBLANES - 1:SUBLANES, :]

    h_last = lax.fori_loop(0, tc // SUBLANES, group, hc_s[...])
    hc_s[...] = h_last
    hlast_ref[0] = h_last
    y_ref[0] = _gelu_tanh_from_half(gate_ref[0]) * h_s[...]


def _lru_group(u, gate, conv0, h0, conv_w, conv_b, wbd, ba, bx, lam):
    b, t, w = u.shape
    tc = min(LRU_TIME_TILE, t)
    assert t % tc == 0 and tc % SUBLANES == 0 and tc >= CONV_WIDTH - 1
    hist = CONV_WIDTH - 1
    seq_spec = pl.BlockSpec((1, tc, w), lambda i, c: (i, c, 0))
    per_batch = lambda r: pl.BlockSpec((1, r, w), lambda i, c: (i, 0, 0))
    return pl.pallas_call(
        functools.partial(_lru_kernel, tc=tc),
        grid=(b, t // tc),
        in_specs=[seq_spec, seq_spec, per_batch(hist), per_batch(1),
                  _const_spec(conv_w.shape), _const_spec((1, w)), _const_spec(wbd.shape),
                  _const_spec((1, w)), _const_spec((1, w)), _const_spec((1, w))],
        out_specs=[seq_spec, per_batch(hist), per_batch(1)],
        out_shape=[jax.ShapeDtypeStruct((b, t, w), F32), jax.ShapeDtypeStruct((b, hist, w), F32),
                   jax.ShapeDtypeStruct((b, 1, w), F32)],
        scratch_shapes=[pltpu.VMEM((SUBLANES + tc, w), F32), pltpu.VMEM((tc, w), F32),
                        pltpu.VMEM((tc, w), F32), pltpu.VMEM((tc, w), F32), pltpu.VMEM((1, w), F32)],
        compiler_params=_params(("arbitrary", "arbitrary")),
        name="rg_lru",
    )(u, gate, conv0, h0.reshape(b, 1, w), conv_w, conv_b, wbd, ba, bx, lam)


def _gap_kind(gap):
    if gap < 0 or gap > LEFT_CHUNKS:
        return "masked"
    return "near" if gap < NEAR_GAPS else "far"


def _attn_prompt_kernel(q_ref, k_ref, vt_ref, bt_ref, o_ref, s_scr, p_scr, ot_scr, *, n_heads, n_blocks):
    lane = lax.broadcasted_iota(jnp.int32, (1, LANES), 1)
    head_lanes = (lane < HEAD_DIM, lane >= HEAD_DIM)
    q_chunks_per_tile = LANES // CHUNK
    zero_tile = jnp.zeros((CHUNK, LANES), BF16)

    def block(lead, q_start, k_start, vt_start):
        k_rows = lead * CHUNK + Q_BLOCK
        q_rows = pl.ds(q_start, Q_BLOCK)
        n_lt = Q_BLOCK // LANES
        plan = []
        for kc in range(k_rows // CHUNK):
            gaps = [lt * q_chunks_per_tile + lead - kc for lt in range(n_lt)]
            plan.append([None if _gap_kind(g) == "masked" and _gap_kind(g + 1) == "masked" else g
                         for g in gaps])

        def scores(h):
            pair = slice((h // 2) * LANES, (h // 2 + 1) * LANES)
            qm = jnp.where(head_lanes[h % 2], q_ref[0, q_rows, pair], jnp.zeros((), BF16))
            for r0 in range(0, k_rows, k_rows // 2):
                kw = k_ref[0, pl.ds(k_start + r0, k_rows // 2), pair]
                s_scr[h % 2, r0:r0 + k_rows // 2, :] = lax.dot_general(kw, qm, NT_DIMS,
                                                                       preferred_element_type=F32)

        def softmax(h):
            slot = h % 2

            def tiles(kc):
                t = s_scr[slot, kc * CHUNK:(kc + 1) * CHUNK, :]
                out = []
                for lt, gap in enumerate(plan[kc]):
                    x = None
                    if gap is not None:
                        x = t[:, lt * LANES:(lt + 1) * LANES]
                        if gap in TILE_GAPS:
                            x = x + bt_ref[h, TILE_GAPS.index(gap)]
                    out.append(x)
                return out

            m = [None] * n_lt
            for kc in range(len(plan)):
                for lt, x in enumerate(tiles(kc)):
                    if x is not None:
                        m[lt] = x if m[lt] is None else jnp.maximum(m[lt], x)
            m = [jnp.max(x, axis=0, keepdims=True) for x in m]
            l = [None] * n_lt
            for kc in range(len(plan)):
                ps = []
                for lt, x in enumerate(tiles(kc)):
                    if x is None:
                        ps.append(zero_tile)
                    else:
                        p = jnp.exp(x - m[lt])
                        l[lt] = p if l[lt] is None else l[lt] + p
                        ps.append(p.astype(BF16))
                p_scr[slot, kc * CHUNK:(kc + 1) * CHUNK, :] = jnp.concatenate(ps, axis=1)
            return jnp.concatenate([1.0 / jnp.sum(x, axis=0, keepdims=True) for x in l], axis=1)

        def weighted_values(h, inv_l):
            rows = slice(h * HEAD_DIM, (h + 1) * HEAD_DIM)
            acc = None
            for blk in range(k_rows // KEY_BLOCK):
                part = jnp.dot(vt_ref[vt_start + blk, rows, :],
                               p_scr[h % 2, blk * KEY_BLOCK:(blk + 1) * KEY_BLOCK, :],
                               preferred_element_type=F32)
                acc = part if acc is None else acc + part
            ot_scr[rows, :] = acc * inv_l

        scores(0)
        inv_prev = None
        for h in range(n_heads):
            if h + 1 < n_heads:
                scores(h + 1)
            if h > 0:
                weighted_values(h - 1, inv_prev)
            inv_prev = softmax(h)
        weighted_values(n_heads - 1, inv_prev)
        o_ref[0, q_rows, :] = ot_scr[...].T

    ramp = PAD_ROWS // Q_BLOCK
    for jj in range(min(ramp, n_blocks)):
        block(jj * (Q_BLOCK // CHUNK), jj * Q_BLOCK, 0, 0)

    def full_block(j, carry):
        block(LEFT_CHUNKS, pl.multiple_of(j * Q_BLOCK, Q_BLOCK),
              pl.multiple_of((j - ramp) * Q_BLOCK, Q_BLOCK), j - ramp)
        return carry

    lax.fori_loop(ramp, n_blocks, full_block, 0)


def _rel_bias_matrix(table, rows, cols, dist00):
    n = rows + cols - 1
    dist = np.arange(n) - (cols - 1) + dist00
    g = table[:, np.clip(dist, -MAX_REL, MAX_REL) + MAX_REL].astype(F32)
    period = rows + cols
    g = jnp.pad(g, ((0, 0), (0, period - n)))
    flat = jnp.tile(g, (1, rows + 1))[:, :rows * (period + 1)]
    hankel = flat.reshape(-1, rows, period + 1)[:, :, :cols]
    return hankel[:, :, ::-1]


def _prompt_bias_tiles(table):
    far = table[:, 2 * MAX_REL].astype(F32)[:, None, None]

    def chunk_pair(gap):
        kind = _gap_kind(gap)
        if kind == "near":
            return jnp.swapaxes(_rel_bias_matrix(table, CHUNK, CHUNK, gap * CHUNK), 1, 2) - far
        fill = NEG_BIG if kind == "masked" else 0.0
        return jnp.full((table.shape[0], CHUNK, CHUNK), fill, F32)

    tiles = [jnp.concatenate([chunk_pair(g), chunk_pair(g + 1)], axis=2) for g in TILE_GAPS]
    return jnp.stack(tiles, axis=1)


def _attend_prompt(q, k, v_t, table):
    b, t, w = q.shape
    n_heads = w // HEAD_DIM
    assert t % Q_BLOCK == 0 and PAD_ROWS % Q_BLOCK == 0 and LANES // CHUNK == 2 and Q_BLOCK % LANES == 0
    n_blocks = t // Q_BLOCK
    k_rows = PAD_ROWS + Q_BLOCK
    tiles = _prompt_bias_tiles(table)
    seq = pl.BlockSpec((1, t, w), lambda i: (i, 0, 0))
    return pl.pallas_call(
        functools.partial(_attn_prompt_kernel, n_heads=n_heads, n_blocks=n_blocks),
        grid=(b,),
        in_specs=[seq, seq, pl.BlockSpec((n_blocks, w, KEY_BLOCK), lambda i: (i, 0, 0)),
                  _const_spec(tiles.shape)],
        out_specs=seq,
        out_shape=jax.ShapeDtypeStruct((b, t, w), F32),
        scratch_shapes=[pltpu.VMEM((2, k_rows, Q_BLOCK), F32), pltpu.VMEM((2, k_rows, Q_BLOCK), BF16),
                        pltpu.VMEM((w, Q_BLOCK), F32)],
        compiler_params=_params(("arbitrary",)),
        name="attn_prompt",
    )(q, k, v_t, tiles)


def _attn_sample_kernel(q_ref, k_ref, v_ref, ckt_ref, cvt_ref, bc_ref, bn_ref, o_ref, *, n_heads):
    for h in range(n_heads):
        hs = slice(h * HEAD_DIM, (h + 1) * HEAD_DIM)
        qh = q_ref[0, :, hs]
        s_c = jnp.dot(qh, ckt_ref[0, hs, :].astype(BF16), preferred_element_type=F32) + bc_ref[h]
        s_n = lax.dot_general(qh, k_ref[0, :, hs], NT_DIMS, preferred_element_type=F32) + bn_ref[h]
        m = jnp.maximum(jnp.max(s_c, axis=-1, keepdims=True), jnp.max(s_n, axis=-1, keepdims=True))
        p_c = jnp.exp(s_c - m)
        p_n = jnp.exp(s_n - m)
        l = jnp.sum(p_c, axis=-1, keepdims=True) + jnp.sum(p_n, axis=-1, keepdims=True)
        o = lax.dot_general(p_c.astype(BF16), cvt_ref[0, hs, :].astype(BF16), NT_DIMS,
                            preferred_element_type=F32)
        o = o + jnp.dot(p_n.astype(BF16), v_ref[0, :, hs], preferred_element_type=F32)
        o_ref[0, :, hs] = o / l


def _attend_sample(q, k, v, k_cache, v_cache, table):
    b, t, w = q.shape
    n_heads = w // HEAD_DIM
    r = k_cache.shape[1]
    bias_c = _rel_bias_matrix(table, t, r, r)
    bias_n = _rel_bias_matrix(table, t, t, 0)
    channel_major = lambda c: jnp.transpose(c, (0, 2, 3, 1)).reshape(b, w, r)
    new = pl.BlockSpec((1, t, w), lambda i: (i, 0, 0))
    old = pl.BlockSpec((1, w, r), lambda i: (i, 0, 0))
    return pl.pallas_call(
        functools.partial(_attn_sample_kernel, n_heads=n_heads),
        grid=(b,),
        in_specs=[new, new, new, old, old, _const_spec(bias_c.shape), _const_spec(bias_n.shape)],
        out_specs=new,
        out_shape=jax.ShapeDtypeStruct((b, t, w), F32),
        compiler_params=_params(("arbitrary",)),
        name="attn_sample",
    )(q, k, v, channel_major(k_cache), channel_major(v_cache), bias_c, bias_n)


def _out_ffn_steps(y_lru, x_ref, ya_ref, gl_ref, ga_ref, wo_ref, gf_ref, wg_ref, wu_ref, wd_ref, gfin_ref,
                   o_ref, xn_s, hmid_s, acc_s, final_norm, side_work=()):
    lru_w = y_lru.shape[1]
    d_ff = wg_ref.shape[1]
    n_tiles = d_ff // FFN_TILE
    nl = _rms(y_lru, gl_ref[...]).astype(BF16)
    na = _rms(ya_ref[...], ga_ref[...]).astype(BF16)
    mix = jnp.dot(nl, wo_ref[:lru_w, :], preferred_element_type=F32)
    mix = mix + jnp.dot(na, wo_ref[lru_w:, :], preferred_element_type=F32)
    x1 = x_ref[...] + mix
    acc_s[...] = x1
    xn_s[...] = _rms(x1, gf_ref[...]).astype(BF16)

    def gate_up(c):
        cols = slice(c * FFN_TILE, (c + 1) * FFN_TILE)
        xn = xn_s[...]
        return (jnp.dot(xn, wg_ref[:, cols], preferred_element_type=F32),
                jnp.dot(xn, wu_ref[:, cols], preferred_element_type=F32))

    side_work = list(side_work)
    pending = gate_up(0)
    for c in range(n_tiles):
        upcoming = gate_up(c + 1) if c + 1 < n_tiles else None
        g, up = pending
        for _ in range(-(-len(side_work) // (n_tiles - c))):
            side_work.pop(0)(_zero_token(g))
        hmid_s[:, c * FFN_TILE:(c + 1) * FFN_TILE] = (_silu_from_half(g) * up).astype(BF16)
        if (c + 1) % DOWN_GROUP == 0 or c + 1 == n_tiles:
            k0 = (c // DOWN_GROUP) * DOWN_GROUP * FFN_TILE
            ks = slice(k0, (c + 1) * FFN_TILE)
            acc_s[...] += jnp.dot(hmid_s[:, ks], wd_ref[ks, :], preferred_element_type=F32)
        pending = upcoming
    assert not side_work
    o_ref[...] = _rms(acc_s[...], gfin_ref[...]) if final_norm else acc_s[...]


def _out_ffn_kernel(x_ref, yl_ref, ya_ref, gl_ref, ga_ref, wo_ref, gf_ref, wg_ref, wu_ref, wd_ref,
                    gfin_ref, o_ref, xn_s, hmid_s, acc_s, *, final_norm):
    _out_ffn_steps(yl_ref[...], x_ref, ya_ref, gl_ref, ga_ref, wo_ref, gf_ref, wg_ref, wu_ref, wd_ref,
                   gfin_ref, o_ref, xn_s, hmid_s, acc_s, final_norm)


def _ffn_scratch(tm, d, d_ff):
    return [pltpu.VMEM((tm, d), BF16), pltpu.VMEM((tm, d_ff), BF16), pltpu.VMEM((tm, d), F32)]


def _out_ffn(x2d, y_lru, y_att, gl, ga, wo, gf, wg, wu, wd, gfin, final_norm):
    rows, d = x2d.shape
    tm = min(ROW_TILE, rows)
    assert rows % tm == 0
    lru_w, attn_w, d_ff = y_lru.shape[1], y_att.shape[1], wg.shape[1]
    assert d_ff % FFN_TILE == 0
    row_spec = lambda w: pl.BlockSpec((tm, w), lambda i: (i, 0))
    return pl.pallas_call(
        functools.partial(_out_ffn_kernel, final_norm=final_norm),
        grid=(rows // tm,),
        in_specs=[row_spec(d), row_spec(lru_w), row_spec(attn_w), _const_spec((1, lru_w)),
                  _const_spec((1, attn_w)), _const_spec(wo.shape), _const_spec((1, d)),
                  _const_spec(wg.shape), _const_spec(wu.shape), _const_spec(wd.shape), _const_spec((1, d))],
        out_specs=row_spec(d),
        out_shape=jax.ShapeDtypeStruct((rows, d), F32),
        scratch_shapes=_ffn_scratch(tm, d, d_ff),
        compiler_params=_params(("arbitrary",)),
        name="out_ffn",
    )(x2d, y_lru, y_att, gl, ga, wo, gf, wg, wu, wd, gfin)


def _lru_out_ffn_kernel(x_ref, ya_ref, u_ref, gate_ref, cw_ref, cb_ref, wbd_ref, ba_ref, bx_ref, lam_ref,
                        gl_ref, ga_ref, wo_ref, gf_ref, wg_ref, wu_ref, wd_ref, gfin_ref,
                        o_ref, convn_ref, hlast_ref,
                        ext_s, uc_s, ucb_s, pre_s, y_s, hc_s, xn_s, hmid_s, acc_s,
                        *, tiles_per_seq, n_row_tiles, final_norm):
    s = pl.program_id(0)
    tm, lru_w = u_ref.shape
    hist = CONV_WIDTH - 1
    base = SUBLANES
    n_halves, half = wbd_ref.shape[0], wbd_ref.shape[1]

    @pl.when(s == 0)
    def _():
        y_s[1] = jnp.zeros((tm, lru_w), F32)
        ext_s[base - hist:base, :] = jnp.zeros((hist, lru_w), F32)
        hc_s[...] = jnp.zeros((1, lru_w), F32)

    seq_start = (s % tiles_per_seq) == 0
    ext_s[base - hist:base, :] = jnp.where(seq_start, 0.0, ext_s[base - hist:base, :])
    ext_s[base:base + tm, :] = u_ref[...]
    h_start = jnp.where(seq_start, 0.0, hc_s[...])
    carry = [h_start[:, hf * half:(hf + 1) * half] for hf in range(n_halves)]
    half_c_lsl = (0.5 * LRU_C) * _log_sigmoid(lam_ref[...])
    row = lax.broadcasted_iota(jnp.int32, (SUBLANES, half), 0)
    slot = s % 2

    def conv(pb, token):
        rows = slice(pb * LRU_PIECE_ROWS, (pb + 1) * LRU_PIECE_ROWS)
        bias = cb_ref[...] + jnp.tile(token, (1, lru_w // LANES))
        uc = _conv_rows(ext_s, base + pb * LRU_PIECE_ROWS, LRU_PIECE_ROWS, cw_ref[...], bias)
        uc_s[rows, :] = uc
        ucb_s[rows, :] = uc.astype(BF16)

    def gate_matmul(gb, token):
        del token
        rows = slice(gb * LRU_GATE_ROWS, (gb + 1) * LRU_GATE_ROWS)
        for hf in range(n_halves):
            pre_s[rows, hf * 2 * half:(hf + 1) * 2 * half] = jnp.dot(
                ucb_s[rows, hf * half:(hf + 1) * half], wbd_ref[hf], preferred_element_type=F32)

    def recur(pb, hf, token):
        rows = slice(pb * LRU_PIECE_ROWS, (pb + 1) * LRU_PIECE_ROWS)
        cols = slice(hf * half, (hf + 1) * half)
        zero = jnp.tile(token, (1, half // LANES))
        a, b = _gate_terms(pre_s[rows, hf * 2 * half:hf * 2 * half + half],
                           pre_s[rows, hf * 2 * half + half:(hf + 1) * 2 * half],
                           uc_s[rows, cols], ba_ref[:, cols] + zero, bx_ref[:, cols] + zero,
                           half_c_lsl[:, cols])
        h_prev = carry[hf]
        hs = []
        for g in range(LRU_PIECE_ROWS // SUBLANES):
            grp = slice(g * SUBLANES, (g + 1) * SUBLANES)
            h = _scan_group(a[grp], b[grp], h_prev, row)
            h_prev = h[SUBLANES - 1:SUBLANES, :]
            hs.append(h)
        carry[hf] = h_prev
        y_s[slot, rows, cols] = _gelu_tanh_from_half(gate_ref[rows, cols]) * jnp.concatenate(hs, axis=0)

    per = LRU_GATE_ROWS // LRU_PIECE_ROWS
    n_gate_blocks = tm // LRU_GATE_ROWS
    side_work = [functools.partial(conv, pb) for pb in range(per)]
    for gb in range(n_gate_blocks):
        side_work.append(functools.partial(gate_matmul, gb))
        for i in range(per):
            if gb + 1 < n_gate_blocks:
                side_work.append(functools.partial(conv, (gb + 1) * per + i))
            side_work += [functools.partial(recur, gb * per + i, hf) for hf in range(n_halves)]

    _out_ffn_steps(y_s[1 - slot], x_ref, ya_ref, gl_ref, ga_ref, wo_ref, gf_ref, wg_ref, wu_ref, wd_ref,
                   gfin_ref, o_ref, xn_s, hmid_s, acc_s, final_norm, side_work)

    tail = ext_s[base + tm - hist:base + tm, :]
    ext_s[base - hist:base, :] = tail
    h_end = jnp.concatenate(carry, axis=1)
    hc_s[...] = h_end

    @pl.when(s < n_row_tiles)
    def _():
        convn_ref[0] = tail
        hlast_ref[0] = h_end


def _lru_out_ffn(x2d, y_att, u, gate, seq_len, p, gfin, final_norm):
    rows, d = x2d.shape
    tm = ROW_TILE
    lru_w, attn_w, d_ff = u.shape[1], y_att.shape[1], p["w_gate"].shape[1]
    assert rows % tm == 0 and seq_len % tm == 0 and d_ff % FFN_TILE == 0
    assert tm % LRU_GATE_ROWS == 0 and LRU_GATE_ROWS % LRU_PIECE_ROWS == 0
    n = rows // tm
    tiles_per_seq = seq_len // tm
    hist = CONV_WIDTH - 1
    n_halves = p["wbd"].shape[0]
    ffn_rows = lambda w: pl.BlockSpec((tm, w), lambda s: (jnp.maximum(s - 1, 0), 0))
    lru_rows = lambda w: pl.BlockSpec((tm, w), lambda s: (jnp.minimum(s, n - 1), 0))
    per_seq = lambda r: pl.BlockSpec((1, r, lru_w), lambda s: (jnp.minimum(s, n - 1) // tiles_per_seq, 0, 0))
    consts = [p["conv_w"], p["conv_b"], p["wbd"], p["lru_ba"], p["lru_bx"], p["lru_lambda"],
              p["norm_lru_out"], p["norm_attn_out"], p["w_out"], p["norm_ffn"], p["w_gate"], p["w_up"],
              p["w_down"], gfin]
    return pl.pallas_call(
        functools.partial(_lru_out_ffn_kernel, tiles_per_seq=tiles_per_seq, n_row_tiles=n, final_norm=final_norm),
        grid=(n + 1,),
        in_specs=[ffn_rows(d), ffn_rows(attn_w), lru_rows(lru_w), lru_rows(lru_w)]
                 + [_const_spec(c.shape) for c in consts],
        out_specs=[ffn_rows(d), per_seq(hist), per_seq(1)],
        out_shape=[jax.ShapeDtypeStruct((rows, d), F32),
                   jax.ShapeDtypeStruct((rows // seq_len, hist, lru_w), F32),
                   jax.ShapeDtypeStruct((rows // seq_len, 1, lru_w), F32)],
        scratch_shapes=[pltpu.VMEM((SUBLANES + tm, lru_w), F32), pltpu.VMEM((tm, lru_w), F32),
                        pltpu.VMEM((tm, lru_w), BF16), pltpu.VMEM((tm, 2 * n_halves * p["wbd"].shape[1]), F32),
                        pltpu.VMEM((2, tm, lru_w), F32), pltpu.VMEM((1, lru_w), F32)]
                       + _ffn_scratch(tm, d, d_ff),
        compiler_params=_params(("arbitrary",)),
        name="lru_out_ffn",
    )(x2d, y_att, u, gate, *consts)


def _block_diag_gates(wa, wx):
    n, c, _ = wa.shape
    per = MXU_DIM // c
    assert n % per == 0
    eye = jnp.eye(per, dtype=wa.dtype)

    def bd(w):
        w = w.reshape(n // per, per, c, c)
        return (eye[None, :, None, :, None] * w[:, :, :, None, :]).reshape(n // per, per * c, per * c)

    return jnp.concatenate([bd(wa), bd(wx)], axis=-1).astype(BF16)


def _layer(x, conv0, h0, prompt, attend, final_gain, p):
    b, t, d = x.shape
    lru_w = p["conv_w"].shape[1]
    attn_w = p["norm_attn_out"].shape[1]
    n_heads = attn_w // HEAD_DIM
    rows = b * t
    x2d = x.reshape(rows, d)
    u, gate, q, k, v, k_tail, v_tail = _in_projection(x2d, p["norm_mix"], p["w_in"], lru_w, attn_w, t, prompt)
    y_att = attend(q.reshape(b, t, attn_w), k.reshape(b, t, attn_w), v if prompt else v.reshape(b, t, attn_w))
    y_att = y_att.reshape(rows, attn_w)
    final_norm = final_gain is not None
    gfin = final_gain if final_norm else p["norm_ffn"]
    if prompt:
        y, conv_new, h_last = _lru_out_ffn(x2d, y_att, u, gate, t, p, gfin, final_norm)
    else:
        y_lru, conv_new, h_last = _lru_group(u.reshape(b, t, lru_w), gate.reshape(b, t, lru_w), conv0, h0,
                                             p["conv_w"], p["conv_b"], p["wbd"], p["lru_ba"], p["lru_bx"],
                                             p["lru_lambda"])
        y = _out_ffn(x2d, y_lru.reshape(rows, lru_w), y_att, p["norm_lru_out"], p["norm_attn_out"],
                     p["w_out"], p["norm_ffn"], p["w_gate"], p["w_up"], p["w_down"], gfin, final_norm)
    if prompt:
        keep = k_tail.shape[2]
        k_tail = k_tail.reshape(b, n_heads, HEAD_DIM, keep).transpose(0, 3, 1, 2)
        v_tail = v_tail.reshape(b, n_heads, HEAD_DIM, keep).transpose(0, 3, 1, 2)
    else:
        k_tail = k_tail.reshape(b, t, n_heads, HEAD_DIM)
        v_tail = v_tail.reshape(b, t, n_heads, HEAD_DIM)
    return y.reshape(b, t, d), conv_new, h_last.reshape(b, lru_w), k_tail, v_tail


def kernel(x_prompt, x_sample, state_conv, state_lru, cache_k, cache_v, norm_mix, w_in, conv_w, conv_b, lru_wa, lru_ba, lru_wx, lru_bx, lru_lambda, rel_bias, norm_lru_out, norm_attn_out, w_out, norm_ffn, w_gate, w_up, w_down, norm_final):
    depth = w_in.shape[0]
    xp, xs = x_prompt, x_sample
    row = lambda a: a.reshape(1, -1)
    outs = [[] for _ in range(8)]
    for l in range(depth):
        lru_w, attn_w = conv_w.shape[-1], norm_attn_out.shape[-1]
        p = dict(
            norm_mix=row(norm_mix[l]), w_in=_in_weights(w_in[l], lru_w, attn_w), conv_w=0.5 * conv_w[l],
            conv_b=row(0.5 * conv_b[l]), wbd=_block_diag_gates(lru_wa[l], lru_wx[l]),
            lru_ba=row(0.5 * lru_ba[l]), lru_bx=row(0.5 * lru_bx[l]),
            lru_lambda=row(lru_lambda[l]), norm_lru_out=row(norm_lru_out[l]),
            norm_attn_out=row(norm_attn_out[l]), w_out=w_out[l].astype(BF16), norm_ffn=row(norm_ffn[l]),
            w_gate=(0.5 * w_gate[l]).astype(BF16), w_up=w_up[l].astype(BF16), w_down=w_down[l].astype(BF16))
        final_gain = row(norm_final) if l == depth - 1 else None
        att_p = functools.partial(_attend_prompt, table=rel_bias[l])
        xp, cp, hp, kp, vp = _layer(xp, None, None, True, att_p, final_gain, p)
        att_s = functools.partial(_attend_sample, k_cache=cache_k[l], v_cache=cache_v[l], table=rel_bias[l])
        xs, cs, hs, ksn, vsn = _layer(xs, state_conv[l], state_lru[l], False, att_s, final_gain, p)
        for lst, val in zip(outs, (cp, hp, kp, vp, cs, hs, ksn, vsn)):
            lst.append(val)
    return (xp, xs) + tuple(jnp.stack(o) for o in outs)
```

```python
import functools
import math

import jax
import jax.numpy as jnp
import numpy as np
from jax import lax
from jax.experimental import pallas as pl
from jax.experimental.pallas import tpu as pltpu

F32 = jnp.float32
BF16 = jnp.bfloat16

EPS = 1e-6
CHUNK = 64
LEFT_CHUNKS = 8
PAD_ROWS = LEFT_CHUNKS * CHUNK
MAX_REL = 128
HEAD_DIM = 64
N_LRU_BLOCKS = 8
CONV_WIDTH = 4
LRU_C = 8.0
NEG_BIG = -1e30

ROW_TILE = 512
LRU_TIME_TILE = 512
Q_BLOCK = 256
KEY_BLOCK = Q_BLOCK
FFN_TILE = 256
DOWN_GROUP = 4
LRU_GATE_ROWS = 256
LRU_PIECE_ROWS = 64
SUBLANES = 8
MXU_DIM = 256
LANES = 128
VMEM_LIMIT = 56 * 1024 * 1024

NT_DIMS = (((1,), (1,)), ((), ()))

NEAR_GAPS = -(-(MAX_REL + CHUNK - 1) // CHUNK)
TILE_GAPS = (-1,) + tuple(range(NEAR_GAPS)) + (LEFT_CHUNKS,)


def _params(sem):
    return pltpu.CompilerParams(dimension_semantics=sem, vmem_limit_bytes=VMEM_LIMIT)


def _const_spec(shape):
    nd = len(shape)
    return pl.BlockSpec(shape, lambda *_: (0,) * nd, pipeline_mode=pl.Buffered(1))


def _rms(x, gain):
    return x * lax.rsqrt(jnp.mean(x * x, axis=-1, keepdims=True) + EPS) * gain


def _zero_token(v):
    bits = pltpu.bitcast(v[:SUBLANES, :LANES], jnp.uint32)
    zero = lax.shift_right_logical(lax.shift_right_logical(bits, jnp.uint32(31)), jnp.uint32(1))
    return pltpu.bitcast(zero, F32)[0:1, :]


def _silu_from_half(hx):
    return hx * (jnp.tanh(hx) + 1.0)


def _gelu_tanh_from_half(hx):
    c = math.sqrt(2.0 / math.pi)
    return hx * (1.0 + jnp.tanh(hx * (2.0 * c + (8.0 * 0.044715 * c) * (hx * hx))))


def _in_weights(w, lru_w, attn_w):
    u, gate, q, k, v = jnp.split(w, np.cumsum([lru_w, lru_w, attn_w, attn_w]), axis=1)
    return jnp.concatenate([v, k, q * (HEAD_DIM ** -0.5), gate * 0.5, u], axis=1).astype(BF16)


def _split_columns(proj, lru_w, attn_w):
    c0, c1, c2, c3 = attn_w, 2 * attn_w, 3 * attn_w, 3 * attn_w + lru_w
    return proj[:, :c0], proj[:, c0:c1], proj[:, c1:c2], proj[:, c2:c3], proj[:, c3:]


def _project(x_ref, gain_ref, w_ref, lru_w, attn_w):
    xn = _rms(x_ref[...], gain_ref[...]).astype(BF16)
    proj = jnp.dot(xn, w_ref[...], preferred_element_type=F32)
    return _split_columns(proj, lru_w, attn_w)


def _inproj_prompt_kernel(x_ref, gain_ref, w_ref, u_ref, gate_ref, q_ref, k_ref, vt_ref, ktail_ref,
                          vtail_ref, *, lru_w, attn_w, tiles_per_seq):
    v, k, q, gate, u = _project(x_ref, gain_ref, w_ref, lru_w, attn_w)
    u_ref[...] = u
    plane = gate_ref.shape[2]
    for hf in range(gate_ref.shape[0]):
        gate_ref[hf] = gate[:, hf * plane:(hf + 1) * plane]
    q_ref[...] = q.astype(BF16)
    k_ref[...] = k.astype(BF16)
    v_t = v.T
    for blk in range(vt_ref.shape[0]):
        vt_ref[blk] = v_t[:, blk * KEY_BLOCK:(blk + 1) * KEY_BLOCK].astype(BF16)

    @pl.when(pl.program_id(0) % tiles_per_seq == tiles_per_seq - 1)
    def _():
        ktail_ref[0] = k.T
        vtail_ref[0] = v_t


def _inproj_sample_kernel(x_ref, gain_ref, w_ref, u_ref, gate_ref, q_ref, k_ref, v_ref, kt_ref, vt_ref,
                          *, lru_w, attn_w):
    v, k, q, gate, u = _project(x_ref, gain_ref, w_ref, lru_w, attn_w)
    u_ref[...] = u
    gate_ref[...] = gate
    q_ref[...] = q.astype(BF16)
    k_ref[...] = k.astype(BF16)
    v_ref[...] = v.astype(BF16)
    tm = k.shape[0]
    n_heads = attn_w // HEAD_DIM
    for h in range(n_heads):
        hs = slice(h * HEAD_DIM, (h + 1) * HEAD_DIM)
        kt_ref[pl.ds(h, tm, stride=n_heads), :] = k[:, hs]
        vt_ref[pl.ds(h, tm, stride=n_heads), :] = v[:, hs]


def _in_projection(x2d, gain, w_bf16, lru_w, attn_w, seq_len, prompt):
    rows, d = x2d.shape
    tm = min(ROW_TILE, rows)
    assert rows % tm == 0
    n_tiles = rows // tm
    n_cols = w_bf16.shape[1]
    row_spec = lambda w: pl.BlockSpec((tm, w), lambda i: (i, 0))
    in_specs = [row_spec(d), _const_spec((1, d)), _const_spec((d, n_cols))]
    common_specs = [row_spec(lru_w), row_spec(lru_w), row_spec(attn_w), row_spec(attn_w)]
    common_shapes = [jax.ShapeDtypeStruct((rows, lru_w), F32), jax.ShapeDtypeStruct((rows, lru_w), F32),
                     jax.ShapeDtypeStruct((rows, attn_w), BF16), jax.ShapeDtypeStruct((rows, attn_w), BF16)]
    if prompt:
        assert seq_len % tm == 0 and tm % KEY_BLOCK == 0 and tm == min(PAD_ROWS, seq_len)
        planes = lru_w // MXU_DIM
        common_specs[1] = pl.BlockSpec((planes, tm, MXU_DIM), lambda i: (0, i, 0))
        common_shapes[1] = jax.ShapeDtypeStruct((planes, rows, MXU_DIM), F32)
        tiles_per_seq = seq_len // tm
        blocks = tm // KEY_BLOCK
        tail_spec = pl.BlockSpec((1, attn_w, tm), lambda i: (i // tiles_per_seq, 0, 0))
        tail_shape = jax.ShapeDtypeStruct((rows // seq_len, attn_w, tm), F32)
        body = functools.partial(_inproj_prompt_kernel, lru_w=lru_w, attn_w=attn_w, tiles_per_seq=tiles_per_seq)
        out_specs = common_specs + [pl.BlockSpec((blocks, attn_w, KEY_BLOCK), lambda i: (i, 0, 0)),
                                    tail_spec, tail_spec]
        out_shape = common_shapes + [jax.ShapeDtypeStruct((rows // KEY_BLOCK, attn_w, KEY_BLOCK), BF16),
                                     tail_shape, tail_shape]
    else:
        n_heads = attn_w // HEAD_DIM
        tail_spec = pl.BlockSpec((tm * n_heads, HEAD_DIM), lambda i: (i, 0))
        tail_shape = jax.ShapeDtypeStruct((rows * n_heads, HEAD_DIM), F32)
        body = functools.partial(_inproj_sample_kernel, lru_w=lru_w, attn_w=attn_w)
        out_specs = common_specs + [row_spec(attn_w), tail_spec, tail_spec]
        out_shape = common_shapes + [jax.ShapeDtypeStruct((rows, attn_w), BF16), tail_shape, tail_shape]
    return pl.pallas_call(
        body, grid=(n_tiles,), in_specs=in_specs, out_specs=out_specs, out_shape=out_shape,
        compiler_params=_params(("arbitrary",)), name="in_projection",
    )(x2d, gain, w_bf16)


def _log_sigmoid(x):
    return jnp.minimum(x, 0.0) - jnp.log1p(jnp.exp(-jnp.abs(x)))


def _conv_rows(ext_s, r0, n, w, bias):
    hist = CONV_WIDTH - 1
    uc = bias + ext_s[r0:r0 + n, :] * w[hist:hist + 1]
    for kk in range(hist):
        uc = uc + ext_s[r0 - hist + kk:r0 - hist + kk + n, :] * w[kk:kk + 1]
    return uc


def _gate_terms(half_pre_a, half_pre_x, half_uc, half_ba, half_bx, half_c_log_sig_lam):
    t_r = jnp.tanh(half_pre_a + half_ba)
    t_i = jnp.tanh(half_pre_x + half_bx)
    a = jnp.exp(half_c_log_sig_lam * t_r + half_c_log_sig_lam)
    gain = jnp.sqrt(1.0 - a * a)
    return a, gain * (half_uc * (t_i + 1.0))


def _scan_group(a, b, h_prev, row):
    sh = 1
    while sh < SUBLANES:
        keep = row >= sh
        a_sh = jnp.where(keep, pltpu.roll(a, sh, 0), 1.0)
        b_sh = jnp.where(keep, pltpu.roll(b, sh, 0), 0.0)
        b = a * b_sh + b
        a = a * a_sh
        sh *= 2
    return a * h_prev + b


def _lru_kernel(u_ref, gate_ref, conv0_ref, h0_ref, cw_ref, cb_ref, wbd_ref, ba_ref, bx_ref, lam_ref,
                y_ref, convn_ref, hlast_ref, ext_s, a_s, b_s, h_s, hc_s, *, tc):
    c = pl.program_id(1)
    hist = CONV_WIDTH - 1
    base = SUBLANES

    @pl.when(c == 0)
    def _():
        ext_s[base - hist:base, :] = conv0_ref[0]
        hc_s[...] = h0_ref[0]

    ext_s[base:base + tc, :] = u_ref[0]
    uc = _conv_rows(ext_s, base, tc, cw_ref[...], cb_ref[...])
    tail = ext_s[base + tc - hist:base + tc, :]
    ext_s[base - hist:base, :] = tail
    convn_ref[0] = tail

    half_c_lsl = (0.5 * LRU_C) * _log_sigmoid(lam_ref[...])
    ucb = uc.astype(BF16)
    half = wbd_ref.shape[1]
    for hf in range(wbd_ref.shape[0]):
        cols = slice(hf * half, (hf + 1) * half)
        pre = jnp.dot(ucb[:, cols], wbd_ref[hf], preferred_element_type=F32)
        a, b = _gate_terms(pre[:, :half], pre[:, half:], uc[:, cols], ba_ref[:, cols], bx_ref[:, cols],
                           half_c_lsl[:, cols])
        a_s[:, cols] = a
        b_s[:, cols] = b

    row = lax.broadcasted_iota(jnp.int32, (SUBLANES, a_s.shape[1]), 0)

    def group(gi, h_prev):
        r0 = pl.multiple_of(gi * SUBLANES, SUBLANES)
        h = _scan_group(a_s[pl.ds(r0, SUBLANES), :], b_s[pl.ds(r0, SUBLANES), :], h_prev, row)
        h_s[pl.ds(r0, SUBLANES), :] = h
        return h[SUBLANES - 1:SUBLANES, :]

    h_last = lax.fori_loop(0, tc // SUBLANES, group, hc_s[...])
    hc_s[...] = h_last
    hlast_ref[0] = h_last
    y_ref[0] = _gelu_tanh_from_half(gate_ref[0]) * h_s[...]


def _lru_group(u, gate, conv0, h0, conv_w, conv_b, wbd, ba, bx, lam):
    b, t, w = u.shape
    tc = min(LRU_TIME_TILE, t)
    assert t % tc == 0 and tc % SUBLANES == 0 and tc >= CONV_WIDTH - 1
    hist = CONV_WIDTH - 1
    seq_spec = pl.BlockSpec((1, tc, w), lambda i, c: (i, c, 0))
    per_batch = lambda r: pl.BlockSpec((1, r, w), lambda i, c: (i, 0, 0))
    return pl.pallas_call(
        functools.partial(_lru_kernel, tc=tc),
        grid=(b, t // tc),
        in_specs=[seq_spec, seq_spec, per_batch(hist), per_batch(1),
                  _const_spec(conv_w.shape), _const_spec((1, w)), _const_spec(wbd.shape),
                  _const_spec((1, w)), _const_spec((1, w)), _const_spec((1, w))],
        out_specs=[seq_spec, per_batch(hist), per_batch(1)],
        out_shape=[jax.ShapeDtypeStruct((b, t, w), F32), jax.ShapeDtypeStruct((b, hist, w), F32),
                   jax.ShapeDtypeStruct((b, 1, w), F32)],
        scratch_shapes=[pltpu.VMEM((SUBLANES + tc, w), F32), pltpu.VMEM((tc, w), F32),
                        pltpu.VMEM((tc, w), F32), pltpu.VMEM((tc, w), F32), pltpu.VMEM((1, w), F32)],
        compiler_params=_params(("arbitrary", "arbitrary")),
        name="rg_lru",
    )(u, gate, conv0, h0.reshape(b, 1, w), conv_w, conv_b, wbd, ba, bx, lam)


def _gap_kind(gap):
    if gap < 0 or gap > LEFT_CHUNKS:
        return "masked"
    return "near" if gap < NEAR_GAPS else "far"


def _attn_prompt_kernel(q_ref, k_ref, vt_ref, bt_ref, o_ref, s_scr, p_scr, ot_scr, *, n_heads, n_blocks):
    lane = lax.broadcasted_iota(jnp.int32, (1, LANES), 1)
    head_lanes = (lane < HEAD_DIM, lane >= HEAD_DIM)
    q_chunks_per_tile = LANES // CHUNK
    zero_tile = jnp.zeros((CHUNK, LANES), BF16)

    def block(lead, q_start, k_start, vt_start):
        k_rows = lead * CHUNK + Q_BLOCK
        q_rows = pl.ds(q_start, Q_BLOCK)
        n_lt = Q_BLOCK // LANES
        plan = []
        for kc in range(k_rows // CHUNK):
            gaps = [lt * q_chunks_per_tile + lead - kc for lt in range(n_lt)]
            plan.append([None if _gap_kind(g) == "masked" and _gap_kind(g + 1) == "masked" else g
                         for g in gaps])

        def scores(h):
            pair = slice((h // 2) * LANES, (h // 2 + 1) * LANES)
            qm = jnp.where(head_lanes[h % 2], q_ref[0, q_rows, pair], jnp.zeros((), BF16))
            for r0 in range(0, k_rows, k_rows // 2):
                kw = k_ref[0, pl.ds(k_start + r0, k_rows // 2), pair]
                s_scr[h % 2, r0:r0 + k_rows // 2, :] = lax.dot_general(kw, qm, NT_DIMS,
                                                                       preferred_element_type=F32)

        def softmax(h):
            slot = h % 2

            def tiles(kc):
                t = s_scr[slot, kc * CHUNK:(kc + 1) * CHUNK, :]
                out = []
                for lt, gap in enumerate(plan[kc]):
                    x = None
                    if gap is not None:
                        x = t[:, lt * LANES:(lt + 1) * LANES]
                        if gap in TILE_GAPS:
                            x = x + bt_ref[h, TILE_GAPS.index(gap)]
                    out.append(x)
                return out

            m = [None] * n_lt
            for kc in range(len(plan)):
                for lt, x in enumerate(tiles(kc)):
                    if x is not None:
                        m[lt] = x if m[lt] is None else jnp.maximum(m[lt], x)
            m = [jnp.max(x, axis=0, keepdims=True) for x in m]
            l = [None] * n_lt
            for kc in range(len(plan)):
                ps = []
                for lt, x in enumerate(tiles(kc)):
                    if x is None:
                        ps.append(zero_tile)
                    else:
                        p = jnp.exp(x - m[lt])
                        l[lt] = p if l[lt] is None else l[lt] + p
                        ps.append(p.astype(BF16))
                p_scr[slot, kc * CHUNK:(kc + 1) * CHUNK, :] = jnp.concatenate(ps, axis=1)
            return jnp.concatenate([1.0 / jnp.sum(x, axis=0, keepdims=True) for x in l], axis=1)

        def weighted_values(h, inv_l):
            rows = slice(h * HEAD_DIM, (h + 1) * HEAD_DIM)
            acc = None
            for blk in range(k_rows // KEY_BLOCK):
                part = jnp.dot(vt_ref[vt_start + blk, rows, :],
                               p_scr[h % 2, blk * KEY_BLOCK:(blk + 1) * KEY_BLOCK, :],
                               preferred_element_type=F32)
                acc = part if acc is None else acc + part
            ot_scr[rows, :] = acc * inv_l

        scores(0)
        inv_prev = None
        for h in range(n_heads):
            if h + 1 < n_heads:
                scores(h + 1)
            if h > 0:
                weighted_values(h - 1, inv_prev)
            inv_prev = softmax(h)
        weighted_values(n_heads - 1, inv_prev)
        o_ref[0, q_rows, :] = ot_scr[...].T

    ramp = PAD_ROWS // Q_BLOCK
    for jj in range(min(ramp, n_blocks)):
        block(jj * (Q_BLOCK // CHUNK), jj * Q_BLOCK, 0, 0)

    def full_block(j, carry):
        block(LEFT_CHUNKS, pl.multiple_of(j * Q_BLOCK, Q_BLOCK),
              pl.multiple_of((j - ramp) * Q_BLOCK, Q_BLOCK), j - ramp)
        return carry

    lax.fori_loop(ramp, n_blocks, full_block, 0)


def _rel_bias_matrix(table, rows, cols, dist00):
    n = rows + cols - 1
    dist = np.arange(n) - (cols - 1) + dist00
    g = table[:, np.clip(dist, -MAX_REL, MAX_REL) + MAX_REL].astype(F32)
    period = rows + cols
    g = jnp.pad(g, ((0, 0), (0, period - n)))
    flat = jnp.tile(g, (1, rows + 1))[:, :rows * (period + 1)]
    hankel = flat.reshape(-1, rows, period + 1)[:, :, :cols]
    return hankel[:, :, ::-1]


def _prompt_bias_tiles(table):
    far = table[:, 2 * MAX_REL].astype(F32)[:, None, None]

    def chunk_pair(gap):
        kind = _gap_kind(gap)
        if kind == "near":
            return jnp.swapaxes(_rel_bias_matrix(table, CHUNK, CHUNK, gap * CHUNK), 1, 2) - far
        fill = NEG_BIG if kind == "masked" else 0.0
        return jnp.full((table.shape[0], CHUNK, CHUNK), fill, F32)

    tiles = [jnp.concatenate([chunk_pair(g), chunk_pair(g + 1)], axis=2) for g in TILE_GAPS]
    return jnp.stack(tiles, axis=1)


def _attend_prompt(q, k, v_t, table):
    b, t, w = q.shape
    n_heads = w // HEAD_DIM
    assert t % Q_BLOCK == 0 and PAD_ROWS % Q_BLOCK == 0 and LANES // CHUNK == 2 and Q_BLOCK % LANES == 0
    n_blocks = t // Q_BLOCK
    k_rows = PAD_ROWS + Q_BLOCK
    tiles = _prompt_bias_tiles(table)
    seq = pl.BlockSpec((1, t, w), lambda i: (i, 0, 0))
    return pl.pallas_call(
        functools.partial(_attn_prompt_kernel, n_heads=n_heads, n_blocks=n_blocks),
        grid=(b,),
        in_specs=[seq, seq, pl.BlockSpec((n_blocks, w, KEY_BLOCK), lambda i: (i, 0, 0)),
                  _const_spec(tiles.shape)],
        out_specs=seq,
        out_shape=jax.ShapeDtypeStruct((b, t, w), F32),
        scratch_shapes=[pltpu.VMEM((2, k_rows, Q_BLOCK), F32), pltpu.VMEM((2, k_rows, Q_BLOCK), BF16),
                        pltpu.VMEM((w, Q_BLOCK), F32)],
        compiler_params=_params(("arbitrary",)),
        name="attn_prompt",
    )(q, k, v_t, tiles)


def _attn_sample_kernel(q_ref, k_ref, v_ref, ckt_ref, cvt_ref, bc_ref, bn_ref, o_ref, *, n_heads):
    for h in range(n_heads):
        hs = slice(h * HEAD_DIM, (h + 1) * HEAD_DIM)
        qh = q_ref[0, :, hs]
        s_c = jnp.dot(qh, ckt_ref[0, hs, :].astype(BF16), preferred_element_type=F32) + bc_ref[h]
        s_n = lax.dot_general(qh, k_ref[0, :, hs], NT_DIMS, preferred_element_type=F32) + bn_ref[h]
        m = jnp.maximum(jnp.max(s_c, axis=-1, keepdims=True), jnp.max(s_n, axis=-1, keepdims=True))
        p_c = jnp.exp(s_c - m)
        p_n = jnp.exp(s_n - m)
        l = jnp.sum(p_c, axis=-1, keepdims=True) + jnp.sum(p_n, axis=-1, keepdims=True)
        o = lax.dot_general(p_c.astype(BF16), cvt_ref[0, hs, :].astype(BF16), NT_DIMS,
                            preferred_element_type=F32)
        o = o + jnp.dot(p_n.astype(BF16), v_ref[0, :, hs], preferred_element_type=F32)
        o_ref[0, :, hs] = o / l


def _attend_sample(q, k, v, k_cache, v_cache, table):
    b, t, w = q.shape
    n_heads = w // HEAD_DIM
    r = k_cache.shape[1]
    bias_c = _rel_bias_matrix(table, t, r, r)
    bias_n = _rel_bias_matrix(table, t, t, 0)
    channel_major = lambda c: jnp.transpose(c, (0, 2, 3, 1)).reshape(b, w, r)
    new = pl.BlockSpec((1, t, w), lambda i: (i, 0, 0))
    old = pl.BlockSpec((1, w, r), lambda i: (i, 0, 0))
    return pl.pallas_call(
        functools.partial(_attn_sample_kernel, n_heads=n_heads),
        grid=(b,),
        in_specs=[new, new, new, old, old, _const_spec(bias_c.shape), _const_spec(bias_n.shape)],
        out_specs=new,
        out_shape=jax.ShapeDtypeStruct((b, t, w), F32),
        compiler_params=_params(("arbitrary",)),
        name="attn_sample",
    )(q, k, v, channel_major(k_cache), channel_major(v_cache), bias_c, bias_n)


def _out_ffn_steps(y_lru_planes, x_ref, ya_ref, gl_ref, ga_ref, wo_ref, gf_ref, wg_ref, wu_ref, wd_ref,
                   gfin_ref, o_ref, xn_s, hmid_s, acc_s, final_norm, side_work=()):
    n_tiles = wg_ref.shape[0]
    lru_w = sum(y.shape[1] for y in y_lru_planes)
    sum_sq = sum(jnp.sum(y * y, axis=-1, keepdims=True) for y in y_lru_planes)
    inv_rms = lax.rsqrt(sum_sq * (1.0 / lru_w) + EPS)
    mix = jnp.dot(_rms(ya_ref[...], ga_ref[...]).astype(BF16), wo_ref[lru_w:, :], preferred_element_type=F32)
    c0 = 0
    for y in y_lru_planes:
        cols = slice(c0, c0 + y.shape[1])
        mix = mix + jnp.dot((y * inv_rms * gl_ref[:, cols]).astype(BF16), wo_ref[cols, :],
                            preferred_element_type=F32)
        c0 += y.shape[1]
    x1 = x_ref[...] + mix
    acc_s[...] = x1
    xn_s[...] = _rms(x1, gf_ref[...]).astype(BF16)

    def gate_up(c):
        xn = xn_s[...]
        return (jnp.dot(xn, wg_ref[c], preferred_element_type=F32),
                jnp.dot(xn, wu_ref[c], preferred_element_type=F32))

    side_work = list(side_work)
    pending = gate_up(0)
    for c in range(n_tiles):
        upcoming = gate_up(c + 1) if c + 1 < n_tiles else None
        g, up = pending
        for _ in range(-(-len(side_work) // (n_tiles - c))):
            side_work.pop(0)(_zero_token(g))
        hmid_s[c] = (_silu_from_half(g) * up).astype(BF16)
        if (c + 1) % DOWN_GROUP == 0 or c + 1 == n_tiles:
            first = (c // DOWN_GROUP) * DOWN_GROUP
            hmid = jnp.concatenate([hmid_s[t] for t in range(first, c + 1)], axis=1)
            acc_s[...] += jnp.dot(hmid, wd_ref[first * FFN_TILE:(c + 1) * FFN_TILE, :],
                                  preferred_element_type=F32)
        pending = upcoming
    assert not side_work
    o_ref[...] = _rms(acc_s[...], gfin_ref[...]) if final_norm else acc_s[...]


def _out_ffn_kernel(x_ref, yl_ref, ya_ref, gl_ref, ga_ref, wo_ref, gf_ref, wg_ref, wu_ref, wd_ref,
                    gfin_ref, o_ref, xn_s, hmid_s, acc_s, *, final_norm):
    _out_ffn_steps([yl_ref[...]], x_ref, ya_ref, gl_ref, ga_ref, wo_ref, gf_ref, wg_ref, wu_ref, wd_ref,
                   gfin_ref, o_ref, xn_s, hmid_s, acc_s, final_norm)


def _ffn_scratch(tm, d, d_ff):
    return [pltpu.VMEM((tm, d), BF16), pltpu.VMEM((d_ff // FFN_TILE, tm, FFN_TILE), BF16),
            pltpu.VMEM((tm, d), F32)]


def _ffn_column_tiles(w):
    d, d_ff = w.shape
    assert d_ff % FFN_TILE == 0
    return jnp.transpose(w.reshape(d, d_ff // FFN_TILE, FFN_TILE), (1, 0, 2)).astype(BF16)


def _out_ffn(x2d, y_lru, y_att, gl, ga, wo, gf, wg, wu, wd, gfin, final_norm):
    rows, d = x2d.shape
    tm = min(ROW_TILE, rows)
    assert rows % tm == 0
    lru_w, attn_w, d_ff = y_lru.shape[1], y_att.shape[1], wg.shape[0] * wg.shape[2]
    row_spec = lambda w: pl.BlockSpec((tm, w), lambda i: (i, 0))
    return pl.pallas_call(
        functools.partial(_out_ffn_kernel, final_norm=final_norm),
        grid=(rows // tm,),
        in_specs=[row_spec(d), row_spec(lru_w), row_spec(attn_w), _const_spec((1, lru_w)),
                  _const_spec((1, attn_w)), _const_spec(wo.shape), _const_spec((1, d)),
                  _const_spec(wg.shape), _const_spec(wu.shape), _const_spec(wd.shape), _const_spec((1, d))],
        out_specs=row_spec(d),
        out_shape=jax.ShapeDtypeStruct((rows, d), F32),
        scratch_shapes=_ffn_scratch(tm, d, d_ff),
        compiler_params=_params(("arbitrary",)),
        name="out_ffn",
    )(x2d, y_lru, y_att, gl, ga, wo, gf, wg, wu, wd, gfin)


def _lru_out_ffn_kernel(x_ref, ya_ref, u_ref, gate_ref, cw_ref, cb_ref, wbd_ref, ba_ref, bx_ref, lam_ref,
                        gl_ref, ga_ref, wo_ref, gf_ref, wg_ref, wu_ref, wd_ref, gfin_ref,
                        o_ref, convn_ref, hlast_ref,
                        ext_s, uc_s, ucb_s, pre_s, y_s, hc_s, xn_s, hmid_s, acc_s,
                        *, tiles_per_seq, n_row_tiles, final_norm):
    s = pl.program_id(0)
    tm, lru_w = u_ref.shape
    hist = CONV_WIDTH - 1
    base = SUBLANES
    n_halves, half = wbd_ref.shape[0], wbd_ref.shape[1]

    @pl.when(s == 0)
    def _():
        y_s[1] = jnp.zeros((n_halves, tm, half), F32)
        ext_s[base - hist:base, :] = jnp.zeros((hist, lru_w), F32)
        hc_s[...] = jnp.zeros((1, lru_w), F32)

    seq_start = (s % tiles_per_seq) == 0
    ext_s[base - hist:base, :] = jnp.where(seq_start, 0.0, ext_s[base - hist:base, :])
    ext_s[base:base + tm, :] = u_ref[...]
    h_start = jnp.where(seq_start, 0.0, hc_s[...])
    carry = [h_start[:, hf * half:(hf + 1) * half] for hf in range(n_halves)]
    half_c_lsl = (0.5 * LRU_C) * _log_sigmoid(lam_ref[...])
    row = lax.broadcasted_iota(jnp.int32, (SUBLANES, half), 0)
    slot = s % 2

    def conv(pb, token):
        rows = slice(pb * LRU_PIECE_ROWS, (pb + 1) * LRU_PIECE_ROWS)
        bias = cb_ref[...] + jnp.tile(token, (1, lru_w // LANES))
        uc = _conv_rows(ext_s, base + pb * LRU_PIECE_ROWS, LRU_PIECE_ROWS, cw_ref[...], bias)
        for hf in range(n_halves):
            plane = uc[:, hf * half:(hf + 1) * half]
            uc_s[hf, rows, :] = plane
            ucb_s[hf, rows, :] = plane.astype(BF16)

    def gate_matmul(gb, token):
        del token
        rows = slice(gb * LRU_GATE_ROWS, (gb + 1) * LRU_GATE_ROWS)
        for hf in range(n_halves):
            pre = jnp.dot(ucb_s[hf, rows, :], wbd_ref[hf], preferred_element_type=F32)
            pre_s[2 * hf, rows, :] = pre[:, :half]
            pre_s[2 * hf + 1, rows, :] = pre[:, half:]

    def recur(pb, hf, token):
        rows = slice(pb * LRU_PIECE_ROWS, (pb + 1) * LRU_PIECE_ROWS)
        cols = slice(hf * half, (hf + 1) * half)
        zero = jnp.tile(token, (1, half // LANES))
        a, b = _gate_terms(pre_s[2 * hf, rows, :], pre_s[2 * hf + 1, rows, :],
                           uc_s[hf, rows, :], ba_ref[:, cols] + zero, bx_ref[:, cols] + zero,
                           half_c_lsl[:, cols])
        h_prev = carry[hf]
        hs = []
        for g in range(LRU_PIECE_ROWS // SUBLANES):
            grp = slice(g * SUBLANES, (g + 1) * SUBLANES)
            h = _scan_group(a[grp], b[grp], h_prev, row)
            h_prev = h[SUBLANES - 1:SUBLANES, :]
            hs.append(h)
        carry[hf] = h_prev
        y_s[slot, hf, rows, :] = _gelu_tanh_from_half(gate_ref[hf, rows, :]) * jnp.concatenate(hs, axis=0)

    per = LRU_GATE_ROWS // LRU_PIECE_ROWS
    n_gate_blocks = tm // LRU_GATE_ROWS
    side_work = [functools.partial(conv, pb) for pb in range(per)]
    for gb in range(n_gate_blocks):
        side_work.append(functools.partial(gate_matmul, gb))
        for i in range(per):
            if gb + 1 < n_gate_blocks:
                side_work.append(functools.partial(conv, (gb + 1) * per + i))
            side_work += [functools.partial(recur, gb * per + i, hf) for hf in range(n_halves)]

    _out_ffn_steps([y_s[1 - slot, hf] for hf in range(n_halves)], x_ref, ya_ref, gl_ref, ga_ref, wo_ref,
                   gf_ref, wg_ref, wu_ref, wd_ref, gfin_ref, o_ref, xn_s, hmid_s, acc_s, final_norm,
                   side_work)

    tail = ext_s[base + tm - hist:base + tm, :]
    ext_s[base - hist:base, :] = tail
    h_end = jnp.concatenate(carry, axis=1)
    hc_s[...] = h_end

    @pl.when(s < n_row_tiles)
    def _():
        convn_ref[0] = tail
        hlast_ref[0] = h_end


def _lru_out_ffn(x2d, y_att, u, gate, seq_len, p, gfin, final_norm):
    rows, d = x2d.shape
    tm = ROW_TILE
    lru_w, attn_w = u.shape[1], y_att.shape[1]
    d_ff = p["w_gate"].shape[0] * p["w_gate"].shape[2]
    n_halves, half = p["wbd"].shape[0], p["wbd"].shape[1]
    assert rows % tm == 0 and seq_len % tm == 0 and gate.shape == (n_halves, rows, half)
    assert tm % LRU_GATE_ROWS == 0 and LRU_GATE_ROWS % LRU_PIECE_ROWS == 0
    n = rows // tm
    tiles_per_seq = seq_len // tm
    hist = CONV_WIDTH - 1
    ffn_rows = lambda w: pl.BlockSpec((tm, w), lambda s: (jnp.maximum(s - 1, 0), 0))
    lru_rows = lambda w: pl.BlockSpec((tm, w), lambda s: (jnp.minimum(s, n - 1), 0))
    per_seq = lambda r: pl.BlockSpec((1, r, lru_w), lambda s: (jnp.minimum(s, n - 1) // tiles_per_seq, 0, 0))
    consts = [p["conv_w"], p["conv_b"], p["wbd"], p["lru_ba"], p["lru_bx"], p["lru_lambda"],
              p["norm_lru_out"], p["norm_attn_out"], p["w_out"], p["norm_ffn"], p["w_gate"], p["w_up"],
              p["w_down"], gfin]
    return pl.pallas_call(
        functools.partial(_lru_out_ffn_kernel, tiles_per_seq=tiles_per_seq, n_row_tiles=n, final_norm=final_norm),
        grid=(n + 1,),
        in_specs=[ffn_rows(d), ffn_rows(attn_w), lru_rows(lru_w),
                  pl.BlockSpec((n_halves, tm, half), lambda s: (0, jnp.minimum(s, n - 1), 0))]
                 + [_const_spec(c.shape) for c in consts],
        out_specs=[ffn_rows(d), per_seq(hist), per_seq(1)],
        out_shape=[jax.ShapeDtypeStruct((rows, d), F32),
                   jax.ShapeDtypeStruct((rows // seq_len, hist, lru_w), F32),
                   jax.ShapeDtypeStruct((rows // seq_len, 1, lru_w), F32)],
        scratch_shapes=[pltpu.VMEM((SUBLANES + tm, lru_w), F32), pltpu.VMEM((n_halves, tm, half), F32),
                        pltpu.VMEM((n_halves, tm, half), BF16), pltpu.VMEM((2 * n_halves, tm, half), F32),
                        pltpu.VMEM((2, n_halves, tm, half), F32), pltpu.VMEM((1, lru_w), F32)]
                       + _ffn_scratch(tm, d, d_ff),
        compiler_params=_params(("arbitrary",)),
        name="lru_out_ffn",
    )(x2d, y_att, u, gate, *consts)


def _block_diag_gates(wa, wx):
    n, c, _ = wa.shape
    per = MXU_DIM // c
    assert n % per == 0
    eye = jnp.eye(per, dtype=wa.dtype)

    def bd(w):
        w = w.reshape(n // per, per, c, c)
        return (eye[None, :, None, :, None] * w[:, :, :, None, :]).reshape(n // per, per * c, per * c)

    return jnp.concatenate([bd(wa), bd(wx)], axis=-1).astype(BF16)


def _layer(x, conv0, h0, prompt, attend, final_gain, p):
    b, t, d = x.shape
    lru_w = p["conv_w"].shape[1]
    attn_w = p["norm_attn_out"].shape[1]
    n_heads = attn_w // HEAD_DIM
    rows = b * t
    x2d = x.reshape(rows, d)
    u, gate, q, k, v, k_tail, v_tail = _in_projection(x2d, p["norm_mix"], p["w_in"], lru_w, attn_w, t, prompt)
    y_att = attend(q.reshape(b, t, attn_w), k.reshape(b, t, attn_w), v if prompt else v.reshape(b, t, attn_w))
    y_att = y_att.reshape(rows, attn_w)
    final_norm = final_gain is not None
    gfin = final_gain if final_norm else p["norm_ffn"]
    if prompt:
        y, conv_new, h_last = _lru_out_ffn(x2d, y_att, u, gate, t, p, gfin, final_norm)
    else:
        y_lru, conv_new, h_last = _lru_group(u.reshape(b, t, lru_w), gate.reshape(b, t, lru_w), conv0, h0,
                                             p["conv_w"], p["conv_b"], p["wbd"], p["lru_ba"], p["lru_bx"],
                                             p["lru_lambda"])
        y = _out_ffn(x2d, y_lru.reshape(rows, lru_w), y_att, p["norm_lru_out"], p["norm_attn_out"],
                     p["w_out"], p["norm_ffn"], p["w_gate"], p["w_up"], p["w_down"], gfin, final_norm)
    if prompt:
        keep = k_tail.shape[2]
        k_tail = k_tail.reshape(b, n_heads, HEAD_DIM, keep).transpose(0, 3, 1, 2)
        v_tail = v_tail.reshape(b, n_heads, HEAD_DIM, keep).transpose(0, 3, 1, 2)
    else:
        k_tail = k_tail.reshape(b, t, n_heads, HEAD_DIM)
        v_tail = v_tail.reshape(b, t, n_heads, HEAD_DIM)
    return y.reshape(b, t, d), conv_new, h_last.reshape(b, lru_w), k_tail, v_tail


def kernel(x_prompt, x_sample, state_conv, state_lru, cache_k, cache_v, norm_mix, w_in, conv_w, conv_b, lru_wa, lru_ba, lru_wx, lru_bx, lru_lambda, rel_bias, norm_lru_out, norm_attn_out, w_out, norm_ffn, w_gate, w_up, w_down, norm_final):
    depth = w_in.shape[0]
    xp, xs = x_prompt, x_sample
    row = lambda a: a.reshape(1, -1)
    outs = [[] for _ in range(8)]
    for l in range(depth):
        lru_w, attn_w = conv_w.shape[-1], norm_attn_out.shape[-1]
        p = dict(
            norm_mix=row(norm_mix[l]), w_in=_in_weights(w_in[l], lru_w, attn_w), conv_w=0.5 * conv_w[l],
            conv_b=row(0.5 * conv_b[l]), wbd=_block_diag_gates(lru_wa[l], lru_wx[l]),
            lru_ba=row(0.5 * lru_ba[l]), lru_bx=row(0.5 * lru_bx[l]),
            lru_lambda=row(lru_lambda[l]), norm_lru_out=row(norm_lru_out[l]),
            norm_attn_out=row(norm_attn_out[l]), w_out=w_out[l].astype(BF16), norm_ffn=row(norm_ffn[l]),
            w_gate=_ffn_column_tiles(0.5 * w_gate[l]), w_up=_ffn_column_tiles(w_up[l]),
            w_down=w_down[l].astype(BF16))
        final_gain = row(norm_final) if l == depth - 1 else None
        att_p = functools.partial(_attend_prompt, table=rel_bias[l])
        xp, cp, hp, kp, vp = _layer(xp, None, None, True, att_p, final_gain, p)
        att_s = functools.partial(_attend_sample, k_cache=cache_k[l], v_cache=cache_v[l], table=rel_bias[l])
        xs, cs, hs, ksn, vsn = _layer(xs, state_conv[l], state_lru[l], False, att_s, final_gain, p)
        for lst, val in zip(outs, (cp, hp, kp, vp, cs, hs, ksn, vsn)):
            lst.append(val)
    return (xp, xs) + tuple(jnp.stack(o) for o in outs)
```

```python
import functools
import math

import jax
import jax.numpy as jnp
import numpy as np
from jax import lax
from jax.experimental import pallas as pl
from jax.experimental.pallas import tpu as pltpu

F32 = jnp.float32
BF16 = jnp.bfloat16

EPS = 1e-6
CHUNK = 64
LEFT_CHUNKS = 8
PAD_ROWS = LEFT_CHUNKS * CHUNK
MAX_REL = 128
HEAD_DIM = 64
N_LRU_BLOCKS = 8
CONV_WIDTH = 4
LRU_C = 8.0
NEG_BIG = -1e30

ROW_TILE = 512
LRU_TIME_TILE = 512
LRU_ROWS_PER_STEP = 128
LRU_UNROLL_GROUPS = 8
Q_BLOCK = 256
KEY_BLOCK = Q_BLOCK
FFN_TILE = 256
DOWN_GROUP = 4
SAMPLE_SEQS_PER_STEP = 4
LRU_GATE_ROWS = 256
LRU_PIECE_ROWS = 64
SUBLANES = 8
MXU_DIM = 256
LANES = 128
VMEM_LIMIT = 56 * 1024 * 1024

NT_DIMS = (((1,), (1,)), ((), ()))

NEAR_GAPS = -(-(MAX_REL + CHUNK - 1) // CHUNK)
TILE_GAPS = (-1,) + tuple(range(NEAR_GAPS)) + (LEFT_CHUNKS,)


def _params(sem):
    return pltpu.CompilerParams(dimension_semantics=sem, vmem_limit_bytes=VMEM_LIMIT)


def _const_spec(shape):
    nd = len(shape)
    return pl.BlockSpec(shape, lambda *_: (0,) * nd, pipeline_mode=pl.Buffered(1))


def _rms(x, gain):
    return x * lax.rsqrt(jnp.mean(x * x, axis=-1, keepdims=True) + EPS) * gain


def _zero_token(v):
    bits = pltpu.bitcast(v[:SUBLANES, :LANES], jnp.uint32)
    zero = lax.shift_right_logical(lax.shift_right_logical(bits, jnp.uint32(31)), jnp.uint32(1))
    return pltpu.bitcast(zero, F32)[0:1, :]


def _silu_from_half(hx):
    return hx * (jnp.tanh(hx) + 1.0)


def _gelu_tanh_from_half(hx):
    c = math.sqrt(2.0 / math.pi)
    return hx * (1.0 + jnp.tanh(hx * (2.0 * c + (8.0 * 0.044715 * c) * (hx * hx))))


def _in_weights(w, lru_w, attn_w):
    u, gate, q, k, v = jnp.split(w, np.cumsum([lru_w, lru_w, attn_w, attn_w]), axis=1)
    return jnp.concatenate([v, k, q * (HEAD_DIM ** -0.5), gate * 0.5, u], axis=1).astype(BF16)


def _split_columns(proj, lru_w, attn_w):
    c0, c1, c2, c3 = attn_w, 2 * attn_w, 3 * attn_w, 3 * attn_w + lru_w
    return proj[:, :c0], proj[:, c0:c1], proj[:, c1:c2], proj[:, c2:c3], proj[:, c3:]


def _project(x_ref, gain_ref, w_ref, lru_w, attn_w):
    xn = _rms(x_ref[...], gain_ref[...]).astype(BF16)
    proj = jnp.dot(xn, w_ref[...], preferred_element_type=F32)
    return _split_columns(proj, lru_w, attn_w)


def _inproj_prompt_kernel(x_ref, gain_ref, w_ref, u_ref, gate_ref, q_ref, k_ref, vt_ref, ktail_ref,
                          vtail_ref, *, lru_w, attn_w, tiles_per_seq):
    v, k, q, gate, u = _project(x_ref, gain_ref, w_ref, lru_w, attn_w)
    u_ref[...] = u
    plane = gate_ref.shape[2]
    for hf in range(gate_ref.shape[0]):
        gate_ref[hf] = gate[:, hf * plane:(hf + 1) * plane]
    q_ref[...] = q.astype(BF16)
    k_ref[...] = k.astype(BF16)
    v_t = v.T
    for blk in range(vt_ref.shape[0]):
        vt_ref[blk] = v_t[:, blk * KEY_BLOCK:(blk + 1) * KEY_BLOCK].astype(BF16)

    @pl.when(pl.program_id(0) % tiles_per_seq == tiles_per_seq - 1)
    def _():
        ktail_ref[0] = k.T
        vtail_ref[0] = v_t


def _inproj_sample_kernel(x_ref, gain_ref, w_ref, u_ref, gate_ref, q_ref, k_ref, v_ref, kt_ref, vt_ref,
                          *, lru_w, attn_w):
    v, k, q, gate, u = _project(x_ref, gain_ref, w_ref, lru_w, attn_w)
    u_ref[...] = u
    gate_ref[...] = gate
    q_ref[...] = q.astype(BF16)
    k_ref[...] = k.astype(BF16)
    v_ref[...] = v.astype(BF16)
    tm = k.shape[0]
    n_heads = attn_w // HEAD_DIM
    for h in range(n_heads):
        hs = slice(h * HEAD_DIM, (h + 1) * HEAD_DIM)
        kt_ref[pl.ds(h, tm, stride=n_heads), :] = k[:, hs]
        vt_ref[pl.ds(h, tm, stride=n_heads), :] = v[:, hs]


def _in_projection(x2d, gain, w_bf16, lru_w, attn_w, seq_len, prompt):
    rows, d = x2d.shape
    tm = min(ROW_TILE, rows)
    assert rows % tm == 0
    n_tiles = rows // tm
    n_cols = w_bf16.shape[1]
    row_spec = lambda w: pl.BlockSpec((tm, w), lambda i: (i, 0))
    in_specs = [row_spec(d), _const_spec((1, d)), _const_spec((d, n_cols))]
    common_specs = [row_spec(lru_w), row_spec(lru_w), row_spec(attn_w), row_spec(attn_w)]
    common_shapes = [jax.ShapeDtypeStruct((rows, lru_w), F32), jax.ShapeDtypeStruct((rows, lru_w), F32),
                     jax.ShapeDtypeStruct((rows, attn_w), BF16), jax.ShapeDtypeStruct((rows, attn_w), BF16)]
    if prompt:
        assert seq_len % tm == 0 and tm % KEY_BLOCK == 0 and tm == min(PAD_ROWS, seq_len)
        planes = lru_w // MXU_DIM
        common_specs[1] = pl.BlockSpec((planes, tm, MXU_DIM), lambda i: (0, i, 0))
        common_shapes[1] = jax.ShapeDtypeStruct((planes, rows, MXU_DIM), F32)
        tiles_per_seq = seq_len // tm
        blocks = tm // KEY_BLOCK
        tail_spec = pl.BlockSpec((1, attn_w, tm), lambda i: (i // tiles_per_seq, 0, 0))
        tail_shape = jax.ShapeDtypeStruct((rows // seq_len, attn_w, tm), F32)
        body = functools.partial(_inproj_prompt_kernel, lru_w=lru_w, attn_w=attn_w, tiles_per_seq=tiles_per_seq)
        out_specs = common_specs + [pl.BlockSpec((blocks, attn_w, KEY_BLOCK), lambda i: (i, 0, 0)),
                                    tail_spec, tail_spec]
        out_shape = common_shapes + [jax.ShapeDtypeStruct((rows // KEY_BLOCK, attn_w, KEY_BLOCK), BF16),
                                     tail_shape, tail_shape]
    else:
        n_heads = attn_w // HEAD_DIM
        tail_spec = pl.BlockSpec((tm * n_heads, HEAD_DIM), lambda i: (i, 0))
        tail_shape = jax.ShapeDtypeStruct((rows * n_heads, HEAD_DIM), F32)
        body = functools.partial(_inproj_sample_kernel, lru_w=lru_w, attn_w=attn_w)
        out_specs = common_specs + [row_spec(attn_w), tail_spec, tail_spec]
        out_shape = common_shapes + [jax.ShapeDtypeStruct((rows, attn_w), BF16), tail_shape, tail_shape]
    return pl.pallas_call(
        body, grid=(n_tiles,), in_specs=in_specs, out_specs=out_specs, out_shape=out_shape,
        compiler_params=_params(("arbitrary",)), name="in_projection",
    )(x2d, gain, w_bf16)


def _log_sigmoid(x):
    return jnp.minimum(x, 0.0) - jnp.log1p(jnp.exp(-jnp.abs(x)))


def _conv_rows(ext_s, r0, n, w, bias):
    hist = CONV_WIDTH - 1
    uc = bias + ext_s[r0:r0 + n, :] * w[hist:hist + 1]
    for kk in range(hist):
        uc = uc + ext_s[r0 - hist + kk:r0 - hist + kk + n, :] * w[kk:kk + 1]
    return uc


def _gate_terms(half_pre_a, half_pre_x, half_uc, half_ba, half_bx, half_c_log_sig_lam):
    t_r = jnp.tanh(half_pre_a + half_ba)
    t_i = jnp.tanh(half_pre_x + half_bx)
    a = jnp.exp(half_c_log_sig_lam * t_r + half_c_log_sig_lam)
    y = 1.0 - a * a
    gain = jnp.where(y > 0.0, y * lax.rsqrt(y), 0.0)
    return a, gain * (half_uc * (t_i + 1.0))


def _scan_masks(width):
    row = lax.broadcasted_iota(jnp.int32, (SUBLANES, width), 0)
    return [row >= (1 << i) for i in range(SUBLANES.bit_length() - 1)]


def _scan_group(a, b, h_prev, masks):
    for i, keep in enumerate(masks):
        a_sh = jnp.where(keep, pltpu.roll(a, 1 << i, 0), 1.0)
        b_sh = jnp.where(keep, pltpu.roll(b, 1 << i, 0), 0.0)
        b = a * b_sh + b
        a = a * a_sh
    return a * h_prev + b


def _lru_kernel(u_ref, gate_ref, conv0_ref, h0_ref, cw_ref, cb_ref, wbd_ref, ba_ref, bx_ref, lam_ref,
                y_ref, convn_ref, hlast_ref, ext_s, a_s, b_s, h_s, hc_s, *, tc):
    c = pl.program_id(1)
    hist = CONV_WIDTH - 1
    base = SUBLANES
    half_c_lsl = (0.5 * LRU_C) * _log_sigmoid(lam_ref[...])
    half = wbd_ref.shape[1]
    masks = _scan_masks(a_s.shape[2])
    n_seqs = u_ref.shape[0]

    @pl.when(c == 0)
    def _():
        for i in range(n_seqs):
            ext_s[i, base - hist:base, :] = conv0_ref[i]
            hc_s[i] = h0_ref[i]

    for i in range(n_seqs):
        ext_s[i, base:base + tc, :] = u_ref[i]
        uc = _conv_rows(ext_s.at[i], base, tc, cw_ref[...], cb_ref[...])
        tail = ext_s[i, base + tc - hist:base + tc, :]
        ext_s[i, base - hist:base, :] = tail
        convn_ref[i] = tail

        ucb = uc.astype(BF16)
        for hf in range(wbd_ref.shape[0]):
            cols = slice(hf * half, (hf + 1) * half)
            pre = jnp.dot(ucb[:, cols], wbd_ref[hf], preferred_element_type=F32)
            a, b = _gate_terms(pre[:, :half], pre[:, half:], uc[:, cols], ba_ref[:, cols], bx_ref[:, cols],
                               half_c_lsl[:, cols])
            a_s[i, :, cols] = a
            b_s[i, :, cols] = b

        def group(gi, h_prev):
            r0 = pl.multiple_of(gi * SUBLANES, SUBLANES)
            h = _scan_group(a_s[i, pl.ds(r0, SUBLANES), :], b_s[i, pl.ds(r0, SUBLANES), :], h_prev, masks)
            h_s[i, pl.ds(r0, SUBLANES), :] = h
            return h[SUBLANES - 1:SUBLANES, :]

        h_last = lax.fori_loop(0, tc // SUBLANES, group, hc_s[i], unroll=tc // SUBLANES <= LRU_UNROLL_GROUPS)
        hc_s[i] = h_last
        hlast_ref[i] = h_last
        y_ref[i] = _gelu_tanh_from_half(gate_ref[i]) * h_s[i]


def _lru_group(u, gate, conv0, h0, conv_w, conv_b, wbd, ba, bx, lam):
    b, t, w = u.shape
    tc = min(LRU_TIME_TILE, t)
    assert t % tc == 0 and tc % SUBLANES == 0 and tc >= CONV_WIDTH - 1
    hist = CONV_WIDTH - 1
    per_step = math.gcd(b, max(1, LRU_ROWS_PER_STEP // tc))
    seq_spec = pl.BlockSpec((per_step, tc, w), lambda i, c: (i, c, 0))
    per_batch = lambda r: pl.BlockSpec((per_step, r, w), lambda i, c: (i, 0, 0))
    return pl.pallas_call(
        functools.partial(_lru_kernel, tc=tc),
        grid=(b // per_step, t // tc),
        in_specs=[seq_spec, seq_spec, per_batch(hist), per_batch(1),
                  _const_spec(conv_w.shape), _const_spec((1, w)), _const_spec(wbd.shape),
                  _const_spec((1, w)), _const_spec((1, w)), _const_spec((1, w))],
        out_specs=[seq_spec, per_batch(hist), per_batch(1)],
        out_shape=[jax.ShapeDtypeStruct((b, t, w), F32), jax.ShapeDtypeStruct((b, hist, w), F32),
                   jax.ShapeDtypeStruct((b, 1, w), F32)],
        scratch_shapes=[pltpu.VMEM((per_step, SUBLANES + tc, w), F32), pltpu.VMEM((per_step, tc, w), F32),
                        pltpu.VMEM((per_step, tc, w), F32), pltpu.VMEM((per_step, tc, w), F32),
                        pltpu.VMEM((per_step, 1, w), F32)],
        compiler_params=_params(("arbitrary", "arbitrary")),
        name="rg_lru",
    )(u, gate, conv0, h0.reshape(b, 1, w), conv_w, conv_b, wbd, ba, bx, lam)


def _gap_kind(gap):
    if gap < 0 or gap > LEFT_CHUNKS:
        return "masked"
    return "near" if gap < NEAR_GAPS else "far"


def _attn_prompt_kernel(q_ref, k_ref, vt_ref, bt_ref, o_ref, s_scr, p_scr, ot_scr, *, n_heads, n_blocks):
    lane = lax.broadcasted_iota(jnp.int32, (1, LANES), 1)
    head_lanes = (lane < HEAD_DIM, lane >= HEAD_DIM)
    q_chunks_per_tile = LANES // CHUNK
    zero_tile = jnp.zeros((CHUNK, LANES), BF16)

    def block(lead, q_start, k_start, vt_start):
        k_rows = lead * CHUNK + Q_BLOCK
        q_rows = pl.ds(q_start, Q_BLOCK)
        n_lt = Q_BLOCK // LANES
        plan = []
        for kc in range(k_rows // CHUNK):
            gaps = [lt * q_chunks_per_tile + lead - kc for lt in range(n_lt)]
            plan.append([None if _gap_kind(g) == "masked" and _gap_kind(g + 1) == "masked" else g
                         for g in gaps])

        def scores(h):
            pair = slice((h // 2) * LANES, (h // 2 + 1) * LANES)
            qm = jnp.where(head_lanes[h % 2], q_ref[0, q_rows, pair], jnp.zeros((), BF16))
            for r0 in range(0, k_rows, k_rows // 2):
                kw = k_ref[0, pl.ds(k_start + r0, k_rows // 2), pair]
                s_scr[h % 2, r0:r0 + k_rows // 2, :] = lax.dot_general(kw, qm, NT_DIMS,
                                                                       preferred_element_type=F32)

        def softmax(h):
            slot = h % 2

            def tiles(kc):
                t = s_scr[slot, kc * CHUNK:(kc + 1) * CHUNK, :]
                out = []
                for lt, gap in enumerate(plan[kc]):
                    x = None
                    if gap is not None:
                        x = t[:, lt * LANES:(lt + 1) * LANES]
                        if gap in TILE_GAPS:
                            x = x + bt_ref[h, TILE_GAPS.index(gap)]
                    out.append(x)
                return out

            m = [None] * n_lt
            for kc in range(len(plan)):
                for lt, x in enumerate(tiles(kc)):
                    if x is not None:
                        m[lt] = x if m[lt] is None else jnp.maximum(m[lt], x)
            m = [jnp.max(x, axis=0, keepdims=True) for x in m]
            l = [None] * n_lt
            for kc in range(len(plan)):
                ps = []
                for lt, x in enumerate(tiles(kc)):
                    if x is None:
                        ps.append(zero_tile)
                    else:
                        p = jnp.exp(x - m[lt])
                        l[lt] = p if l[lt] is None else l[lt] + p
                        ps.append(p.astype(BF16))
                p_scr[slot, kc * CHUNK:(kc + 1) * CHUNK, :] = jnp.concatenate(ps, axis=1)
            return jnp.concatenate([1.0 / jnp.sum(x, axis=0, keepdims=True) for x in l], axis=1)

        def weighted_values(h, inv_l):
            rows = slice(h * HEAD_DIM, (h + 1) * HEAD_DIM)
            acc = None
            for blk in range(k_rows // KEY_BLOCK):
                part = jnp.dot(vt_ref[vt_start + blk, rows, :],
                               p_scr[h % 2, blk * KEY_BLOCK:(blk + 1) * KEY_BLOCK, :],
                               preferred_element_type=F32)
                acc = part if acc is None else acc + part
            ot_scr[rows, :] = acc * inv_l

        scores(0)
        inv_prev = None
        for h in range(n_heads):
            if h + 1 < n_heads:
                scores(h + 1)
            if h > 0:
                weighted_values(h - 1, inv_prev)
            inv_prev = softmax(h)
        weighted_values(n_heads - 1, inv_prev)
        o_ref[0, q_rows, :] = ot_scr[...].T

    ramp = PAD_ROWS // Q_BLOCK
    for jj in range(min(ramp, n_blocks)):
        block(jj * (Q_BLOCK // CHUNK), jj * Q_BLOCK, 0, 0)

    def full_block(j, carry):
        block(LEFT_CHUNKS, pl.multiple_of(j * Q_BLOCK, Q_BLOCK),
              pl.multiple_of((j - ramp) * Q_BLOCK, Q_BLOCK), j - ramp)
        return carry

    lax.fori_loop(ramp, n_blocks, full_block, 0)


def _rel_bias_matrix(table, rows, cols, dist00):
    n = rows + cols - 1
    dist = np.arange(n) - (cols - 1) + dist00
    g = table[:, np.clip(dist, -MAX_REL, MAX_REL) + MAX_REL].astype(F32)
    period = rows + cols
    g = jnp.pad(g, ((0, 0), (0, period - n)))
    flat = jnp.tile(g, (1, rows + 1))[:, :rows * (period + 1)]
    hankel = flat.reshape(-1, rows, period + 1)[:, :, :cols]
    return hankel[:, :, ::-1]


def _prompt_bias_tiles(table):
    far = table[:, 2 * MAX_REL].astype(F32)[:, None, None]

    def chunk_pair(gap):
        kind = _gap_kind(gap)
        if kind == "near":
            return jnp.swapaxes(_rel_bias_matrix(table, CHUNK, CHUNK, gap * CHUNK), 1, 2) - far
        fill = NEG_BIG if kind == "masked" else 0.0
        return jnp.full((table.shape[0], CHUNK, CHUNK), fill, F32)

    tiles = [jnp.concatenate([chunk_pair(g), chunk_pair(g + 1)], axis=2) for g in TILE_GAPS]
    return jnp.stack(tiles, axis=1)


def _attend_prompt(q, k, v_t, table):
    b, t, w = q.shape
    n_heads = w // HEAD_DIM
    assert t % Q_BLOCK == 0 and PAD_ROWS % Q_BLOCK == 0 and LANES // CHUNK == 2 and Q_BLOCK % LANES == 0
    n_blocks = t // Q_BLOCK
    k_rows = PAD_ROWS + Q_BLOCK
    tiles = _prompt_bias_tiles(table)
    seq = pl.BlockSpec((1, t, w), lambda i: (i, 0, 0))
    return pl.pallas_call(
        functools.partial(_attn_prompt_kernel, n_heads=n_heads, n_blocks=n_blocks),
        grid=(b,),
        in_specs=[seq, seq, pl.BlockSpec((n_blocks, w, KEY_BLOCK), lambda i: (i, 0, 0)),
                  _const_spec(tiles.shape)],
        out_specs=seq,
        out_shape=jax.ShapeDtypeStruct((b, t, w), F32),
        scratch_shapes=[pltpu.VMEM((2, k_rows, Q_BLOCK), F32), pltpu.VMEM((2, k_rows, Q_BLOCK), BF16),
                        pltpu.VMEM((w, Q_BLOCK), F32)],
        compiler_params=_params(("arbitrary",)),
        name="attn_prompt",
    )(q, k, v_t, tiles)


def _attn_sample_kernel(q_ref, k_ref, v_ref, ckt_ref, cvt_ref, bc_ref, bn_ref, o_ref, *, n_heads):
    for i in range(q_ref.shape[0]):
        for h in range(n_heads):
            hs = slice(h * HEAD_DIM, (h + 1) * HEAD_DIM)
            qh = q_ref[i, :, hs]
            s_c = jnp.dot(qh, ckt_ref[i, hs, :].astype(BF16), preferred_element_type=F32) + bc_ref[h]
            s_n = lax.dot_general(qh, k_ref[i, :, hs], NT_DIMS, preferred_element_type=F32) + bn_ref[h]
            m = jnp.maximum(jnp.max(s_c, axis=-1, keepdims=True), jnp.max(s_n, axis=-1, keepdims=True))
            p_c = jnp.exp(s_c - m)
            p_n = jnp.exp(s_n - m)
            l = jnp.sum(p_c, axis=-1, keepdims=True) + jnp.sum(p_n, axis=-1, keepdims=True)
            o = lax.dot_general(p_c.astype(BF16), cvt_ref[i, hs, :].astype(BF16), NT_DIMS,
                                preferred_element_type=F32)
            o = o + jnp.dot(p_n.astype(BF16), v_ref[i, :, hs], preferred_element_type=F32)
            o_ref[i, :, hs] = o / l


def _attend_sample(q, k, v, k_cache, v_cache, table):
    b, t, w = q.shape
    n_heads = w // HEAD_DIM
    r = k_cache.shape[1]
    bias_c = _rel_bias_matrix(table, t, r, r)
    bias_n = _rel_bias_matrix(table, t, t, 0)
    channel_major = lambda c: jnp.transpose(c, (0, 2, 3, 1)).reshape(b, w, r)
    per_step = math.gcd(b, SAMPLE_SEQS_PER_STEP)
    new = pl.BlockSpec((per_step, t, w), lambda i: (i, 0, 0))
    old = pl.BlockSpec((per_step, w, r), lambda i: (i, 0, 0))
    return pl.pallas_call(
        functools.partial(_attn_sample_kernel, n_heads=n_heads),
        grid=(b // per_step,),
        in_specs=[new, new, new, old, old, _const_spec(bias_c.shape), _const_spec(bias_n.shape)],
        out_specs=new,
        out_shape=jax.ShapeDtypeStruct((b, t, w), F32),
        compiler_params=_params(("arbitrary",)),
        name="attn_sample",
    )(q, k, v, channel_major(k_cache), channel_major(v_cache), bias_c, bias_n)


def _out_projection(y_lru_planes, x, ya, gl_ref, ga_ref, wo_ref, gf_ref, token=None):
    lru_w = sum(y.shape[1] for y in y_lru_planes)
    attn_w = ya.shape[1]
    gl, ga = gl_ref[...], ga_ref[...]
    if token is not None:
        gl = gl + jnp.tile(token, (1, lru_w // LANES))
        ga = ga + jnp.tile(token, (1, attn_w // LANES))
    sum_sq = sum(jnp.sum(y * y, axis=-1, keepdims=True) for y in y_lru_planes)
    inv_rms = lax.rsqrt(sum_sq * (1.0 / lru_w) + EPS)
    mix = jnp.dot(_rms(ya, ga).astype(BF16), wo_ref[lru_w:, :], preferred_element_type=F32)
    c0 = 0
    for y in y_lru_planes:
        cols = slice(c0, c0 + y.shape[1])
        mix = mix + jnp.dot((y * inv_rms * gl[:, cols]).astype(BF16), wo_ref[cols, :],
                            preferred_element_type=F32)
        c0 += y.shape[1]
    x1 = x + mix
    return x1, _rms(x1, gf_ref[...]).astype(BF16)


def _ffn_column_pipeline(xn_s, acc_s, buf, hmid_s, wg_ref, wu_ref, wd_ref, side_work=()):
    n_tiles = wg_ref.shape[1] // FFN_TILE

    def gate_up(c):
        cols = slice(c * FFN_TILE, (c + 1) * FFN_TILE)
        xn = xn_s[buf]
        return (jnp.dot(xn, wg_ref[:, cols], preferred_element_type=F32),
                jnp.dot(xn, wu_ref[:, cols], preferred_element_type=F32))

    side_work = list(side_work)
    pending = gate_up(0)
    for c in range(n_tiles):
        upcoming = gate_up(c + 1) if c + 1 < n_tiles else None
        g, up = pending
        hmid = _silu_from_half(g) * up
        hmid_s[:, c * FFN_TILE:(c + 1) * FFN_TILE] = hmid.astype(BF16)
        if (c + 1) % DOWN_GROUP == 0 or c + 1 == n_tiles:
            ks = slice((c // DOWN_GROUP) * DOWN_GROUP * FFN_TILE, (c + 1) * FFN_TILE)
            acc_s[buf] = acc_s[buf] + jnp.dot(hmid_s[:, ks], wd_ref[ks, :], preferred_element_type=F32)
        token = _zero_token(hmid[-SUBLANES:, -LANES:])
        for _ in range(-(-len(side_work) // (n_tiles - c))):
            side_work.pop(0)(token)
        pending = upcoming
    assert not side_work


def _out_ffn_kernel(x_ref, yl_ref, ya_ref, gl_ref, ga_ref, wo_ref, gf_ref, wg_ref, wu_ref, wd_ref,
                    gfin_ref, o_ref, xn_s, hmid_s, acc_s, *, final_norm):
    x1, xn = _out_projection([yl_ref[...]], x_ref[...], ya_ref[...], gl_ref, ga_ref, wo_ref, gf_ref)
    acc_s[0] = x1
    xn_s[0] = xn
    _ffn_column_pipeline(xn_s, acc_s, 0, hmid_s, wg_ref, wu_ref, wd_ref)
    o_ref[...] = _rms(acc_s[0], gfin_ref[...]) if final_norm else acc_s[0]


def _ffn_scratch(tm, d, d_ff, bufs):
    assert d_ff % FFN_TILE == 0
    return [pltpu.VMEM((bufs, tm, d), BF16), pltpu.VMEM((tm, d_ff), BF16), pltpu.VMEM((bufs, tm, d), F32)]


def _out_ffn(x2d, y_lru, y_att, gl, ga, wo, gf, wg, wu, wd, gfin, final_norm):
    rows, d = x2d.shape
    tm = min(ROW_TILE, rows)
    assert rows % tm == 0
    lru_w, attn_w, d_ff = y_lru.shape[1], y_att.shape[1], wg.shape[1]
    row_spec = lambda w: pl.BlockSpec((tm, w), lambda i: (i, 0))
    return pl.pallas_call(
        functools.partial(_out_ffn_kernel, final_norm=final_norm),
        grid=(rows // tm,),
        in_specs=[row_spec(d), row_spec(lru_w), row_spec(attn_w), _const_spec((1, lru_w)),
                  _const_spec((1, attn_w)), _const_spec(wo.shape), _const_spec((1, d)),
                  _const_spec(wg.shape), _const_spec(wu.shape), _const_spec(wd.shape), _const_spec((1, d))],
        out_specs=row_spec(d),
        out_shape=jax.ShapeDtypeStruct((rows, d), F32),
        scratch_shapes=_ffn_scratch(tm, d, d_ff, 1),
        compiler_params=_params(("arbitrary",)),
        name="out_ffn",
    )(x2d, y_lru, y_att, gl, ga, wo, gf, wg, wu, wd, gfin)


def _lru_out_ffn_kernel(x_ref, ya_ref, u_ref, gate_ref, cw_ref, cb_ref, wbd_ref, ba_ref, bx_ref, lam_ref,
                        gl_ref, ga_ref, wo_ref, gf_ref, wg_ref, wu_ref, wd_ref, gfin_ref,
                        o_ref, convn_ref, hlast_ref,
                        ext_s, uc_s, ucb_s, pre_s, y_s, hc_s, xn_s, hmid_s, acc_s,
                        *, tiles_per_seq, n_row_tiles, final_norm):
    s = pl.program_id(0)
    tm, lru_w = u_ref.shape
    hist = CONV_WIDTH - 1
    base = SUBLANES
    n_halves, half = wbd_ref.shape[0], wbd_ref.shape[1]
    slot = s % 2

    @pl.when(s == 0)
    def _():
        y_s[1] = jnp.zeros((n_halves, tm, half), F32)
        ext_s[base - hist:base, :] = jnp.zeros((hist, lru_w), F32)
        hc_s[...] = jnp.zeros((1, lru_w), F32)

    seq_start = (s % tiles_per_seq) == 0
    ext_s[base - hist:base, :] = jnp.where(seq_start, 0.0, ext_s[base - hist:base, :])
    ext_s[base:base + tm, :] = u_ref[...]
    h_start = jnp.where(seq_start, 0.0, hc_s[...])
    carry = [h_start[:, hf * half:(hf + 1) * half] for hf in range(n_halves)]
    half_c_lsl = (0.5 * LRU_C) * _log_sigmoid(lam_ref[...])
    masks = _scan_masks(half)

    def conv(pb, token):
        rows = slice(pb * LRU_PIECE_ROWS, (pb + 1) * LRU_PIECE_ROWS)
        bias = cb_ref[...] + jnp.tile(token, (1, lru_w // LANES))
        uc = _conv_rows(ext_s, base + pb * LRU_PIECE_ROWS, LRU_PIECE_ROWS, cw_ref[...], bias)
        for hf in range(n_halves):
            plane = uc[:, hf * half:(hf + 1) * half]
            uc_s[hf, rows, :] = plane
            ucb_s[hf, rows, :] = plane.astype(BF16)

    def gate_matmul(gb, token):
        del token
        rows = slice(gb * LRU_GATE_ROWS, (gb + 1) * LRU_GATE_ROWS)
        for hf in range(n_halves):
            pre = jnp.dot(ucb_s[hf, rows, :], wbd_ref[hf], preferred_element_type=F32)
            pre_s[2 * hf, rows, :] = pre[:, :half]
            pre_s[2 * hf + 1, rows, :] = pre[:, half:]

    def recur(pb, hf, token):
        rows = slice(pb * LRU_PIECE_ROWS, (pb + 1) * LRU_PIECE_ROWS)
        cols = slice(hf * half, (hf + 1) * half)
        zero = jnp.tile(token, (1, half // LANES))
        a, b = _gate_terms(pre_s[2 * hf, rows, :], pre_s[2 * hf + 1, rows, :],
                           uc_s[hf, rows, :], ba_ref[:, cols] + zero, bx_ref[:, cols] + zero,
                           half_c_lsl[:, cols])
        h_prev = carry[hf]
        hs = []
        for g in range(LRU_PIECE_ROWS // SUBLANES):
            grp = slice(g * SUBLANES, (g + 1) * SUBLANES)
            h = _scan_group(a[grp], b[grp], h_prev, masks)
            h_prev = h[SUBLANES - 1:SUBLANES, :]
            hs.append(h)
        carry[hf] = h_prev
        y_s[slot, hf, rows, :] = _gelu_tanh_from_half(gate_ref[hf, rows, :]) * jnp.concatenate(hs, axis=0)

    per = LRU_GATE_ROWS // LRU_PIECE_ROWS
    n_gate_blocks = tm // LRU_GATE_ROWS
    side_work = [functools.partial(conv, pb) for pb in range(per)]
    for gb in range(n_gate_blocks):
        side_work.append(functools.partial(gate_matmul, gb))
        for i in range(per):
            if gb + 1 < n_gate_blocks:
                side_work.append(functools.partial(conv, (gb + 1) * per + i))
            side_work += [functools.partial(recur, gb * per + i, hf) for hf in range(n_halves)]

    x1, xn = _out_projection([y_s[1 - slot, hf] for hf in range(n_halves)], x_ref[...], ya_ref[...],
                             gl_ref, ga_ref, wo_ref, gf_ref)
    acc_s[0] = x1
    xn_s[0] = xn
    _ffn_column_pipeline(xn_s, acc_s, 0, hmid_s, wg_ref, wu_ref, wd_ref, side_work)
    o_ref[...] = _rms(acc_s[0], gfin_ref[...]) if final_norm else acc_s[0]

    tail = ext_s[base + tm - hist:base + tm, :]
    ext_s[base - hist:base, :] = tail
    h_end = jnp.concatenate(carry, axis=1)
    hc_s[...] = h_end

    @pl.when(s < n_row_tiles)
    def _():
        convn_ref[0] = tail
        hlast_ref[0] = h_end


def _lru_out_ffn(x2d, y_att, u, gate, seq_len, p, gfin, final_norm):
    rows, d = x2d.shape
    tm = ROW_TILE
    lru_w, attn_w = u.shape[1], y_att.shape[1]
    d_ff = p["w_gate"].shape[1]
    n_halves, half = p["wbd"].shape[0], p["wbd"].shape[1]
    assert rows % tm == 0 and seq_len % tm == 0 and gate.shape == (n_halves, rows, half)
    assert tm % LRU_GATE_ROWS == 0 and LRU_GATE_ROWS % LRU_PIECE_ROWS == 0
    n = rows // tm
    tiles_per_seq = seq_len // tm
    hist = CONV_WIDTH - 1
    tile = lambda s, lag: jnp.clip(s - lag, 0, n - 1)
    rows_of = lambda w, lag: pl.BlockSpec((tm, w), lambda s: (tile(s, lag), 0))
    per_seq = lambda r: pl.BlockSpec((1, r, lru_w), lambda s: (tile(s, 0) // tiles_per_seq, 0, 0))
    consts = [p["conv_w"], p["conv_b"], p["wbd"], p["lru_ba"], p["lru_bx"], p["lru_lambda"],
              p["norm_lru_out"], p["norm_attn_out"], p["w_out"], p["norm_ffn"], p["w_gate"], p["w_up"],
              p["w_down"], gfin]
    return pl.pallas_call(
        functools.partial(_lru_out_ffn_kernel, tiles_per_seq=tiles_per_seq, n_row_tiles=n, final_norm=final_norm),
        grid=(n + 1,),
        in_specs=[rows_of(d, 1), rows_of(attn_w, 1), rows_of(lru_w, 0),
                  pl.BlockSpec((n_halves, tm, half), lambda s: (0, tile(s, 0), 0))]
                 + [_const_spec(c.shape) for c in consts],
        out_specs=[rows_of(d, 1), per_seq(hist), per_seq(1)],
        out_shape=[jax.ShapeDtypeStruct((rows, d), F32),
                   jax.ShapeDtypeStruct((rows // seq_len, hist, lru_w), F32),
                   jax.ShapeDtypeStruct((rows // seq_len, 1, lru_w), F32)],
        scratch_shapes=[pltpu.VMEM((SUBLANES + tm, lru_w), F32), pltpu.VMEM((n_halves, tm, half), F32),
                        pltpu.VMEM((n_halves, tm, half), BF16), pltpu.VMEM((2 * n_halves, tm, half), F32),
                        pltpu.VMEM((2, n_halves, tm, half), F32), pltpu.VMEM((1, lru_w), F32)]
                       + _ffn_scratch(tm, d, d_ff, 1),
        compiler_params=_params(("arbitrary",)),
        name="lru_out_ffn",
    )(x2d, y_att, u, gate, *consts)


def _block_diag_gates(wa, wx):
    n, c, _ = wa.shape
    per = MXU_DIM // c
    assert n % per == 0
    eye = jnp.eye(per, dtype=wa.dtype)

    def bd(w):
        w = w.reshape(n // per, per, c, c)
        return (eye[None, :, None, :, None] * w[:, :, :, None, :]).reshape(n // per, per * c, per * c)

    return jnp.concatenate([bd(wa), bd(wx)], axis=-1).astype(BF16)


def _layer(x, conv0, h0, prompt, attend, final_gain, p):
    b, t, d = x.shape
    lru_w = p["conv_w"].shape[1]
    attn_w = p["norm_attn_out"].shape[1]
    n_heads = attn_w // HEAD_DIM
    rows = b * t
    x2d = x.reshape(rows, d)
    u, gate, q, k, v, k_tail, v_tail = _in_projection(x2d, p["norm_mix"], p["w_in"], lru_w, attn_w, t, prompt)
    y_att = attend(q.reshape(b, t, attn_w), k.reshape(b, t, attn_w), v if prompt else v.reshape(b, t, attn_w))
    y_att = y_att.reshape(rows, attn_w)
    final_norm = final_gain is not None
    gfin = final_gain if final_norm else p["norm_ffn"]
    if prompt:
        y, conv_new, h_last = _lru_out_ffn(x2d, y_att, u, gate, t, p, gfin, final_norm)
    else:
        y_lru, conv_new, h_last = _lru_group(u.reshape(b, t, lru_w), gate.reshape(b, t, lru_w), conv0, h0,
                                             p["conv_w"], p["conv_b"], p["wbd"], p["lru_ba"], p["lru_bx"],
                                             p["lru_lambda"])
        y = _out_ffn(x2d, y_lru.reshape(rows, lru_w), y_att, p["norm_lru_out"], p["norm_attn_out"],
                     p["w_out"], p["norm_ffn"], p["w_gate"], p["w_up"], p["w_down"], gfin, final_norm)
    if prompt:
        keep = k_tail.shape[2]
        k_tail = k_tail.reshape(b, n_heads, HEAD_DIM, keep).transpose(0, 3, 1, 2)
        v_tail = v_tail.reshape(b, n_heads, HEAD_DIM, keep).transpose(0, 3, 1, 2)
    else:
        k_tail = k_tail.reshape(b, t, n_heads, HEAD_DIM)
        v_tail = v_tail.reshape(b, t, n_heads, HEAD_DIM)
    return y.reshape(b, t, d), conv_new, h_last.reshape(b, lru_w), k_tail, v_tail


def kernel(x_prompt, x_sample, state_conv, state_lru, cache_k, cache_v, norm_mix, w_in, conv_w, conv_b, lru_wa, lru_ba, lru_wx, lru_bx, lru_lambda, rel_bias, norm_lru_out, norm_attn_out, w_out, norm_ffn, w_gate, w_up, w_down, norm_final):
    depth = w_in.shape[0]
    xp, xs = x_prompt, x_sample
    row = lambda a: a.reshape(1, -1)
    outs = [[] for _ in range(8)]
    for l in range(depth):
        lru_w, attn_w = conv_w.shape[-1], norm_attn_out.shape[-1]
        p = dict(
            norm_mix=row(norm_mix[l]), w_in=_in_weights(w_in[l], lru_w, attn_w), conv_w=0.5 * conv_w[l],
            conv_b=row(0.5 * conv_b[l]), wbd=_block_diag_gates(lru_wa[l], lru_wx[l]),
            lru_ba=row(0.5 * lru_ba[l]), lru_bx=row(0.5 * lru_bx[l]),
            lru_lambda=row(lru_lambda[l]), norm_lru_out=row(norm_lru_out[l]),
            norm_attn_out=row(norm_attn_out[l]), w_out=w_out[l].astype(BF16), norm_ffn=row(norm_ffn[l]),
            w_gate=(0.5 * w_gate[l]).astype(BF16), w_up=w_up[l].astype(BF16), w_down=w_down[l].astype(BF16))
        final_gain = row(norm_final) if l == depth - 1 else None
        att_p = functools.partial(_attend_prompt, table=rel_bias[l])
        xp, cp, hp, kp, vp = _layer(xp, None, None, True, att_p, final_gain, p)
        att_s = functools.partial(_attend_sample, k_cache=cache_k[l], v_cache=cache_v[l], table=rel_bias[l])
        xs, cs, hs, ksn, vsn = _layer(xs, state_conv[l], state_lru[l], False, att_s, final_gain, p)
        for lst, val in zip(outs, (cp, hp, kp, vp, cs, hs, ksn, vsn)):
            lst.append(val)
    return (xp, xs) + tuple(jnp.stack(o) for o in outs)
```

```python
import functools
import math

import jax
import jax.numpy as jnp
import numpy as np
from jax import lax
from jax.experimental import pallas as pl
from jax.experimental.pallas import tpu as pltpu

F32 = jnp.float32
BF16 = jnp.bfloat16

EPS = 1e-6
CHUNK = 64
LEFT_CHUNKS = 8
PAD_ROWS = LEFT_CHUNKS * CHUNK
MAX_REL = 128
HEAD_DIM = 64
N_LRU_BLOCKS = 8
CONV_WIDTH = 4
LRU_C = 8.0
NEG_BIG = -1e30

ROW_TILE = 512
LRU_TIME_TILE = 512
LRU_ROWS_PER_STEP = 128
LRU_UNROLL_GROUPS = 8
Q_BLOCK = 256
KEY_BLOCK = Q_BLOCK
FFN_TILE = 256
DOWN_GROUP = 4
SAMPLE_SEQS_PER_STEP = 4
LRU_GATE_ROWS = 256
LRU_PIECE_ROWS = 64
SUBLANES = 8
MXU_DIM = 256
LANES = 128
VMEM_LIMIT = 56 * 1024 * 1024

NT_DIMS = (((1,), (1,)), ((), ()))

NEAR_GAPS = -(-(MAX_REL + CHUNK - 1) // CHUNK)
TILE_GAPS = (-1,) + tuple(range(NEAR_GAPS)) + (LEFT_CHUNKS,)


def _params(sem):
    return pltpu.CompilerParams(dimension_semantics=sem, vmem_limit_bytes=VMEM_LIMIT)


def _const_spec(shape):
    nd = len(shape)
    return pl.BlockSpec(shape, lambda *_: (0,) * nd, pipeline_mode=pl.Buffered(1))


def _rms(x, gain):
    return x * lax.rsqrt(jnp.mean(x * x, axis=-1, keepdims=True) + EPS) * gain


def _zero_token(v):
    bits = pltpu.bitcast(v[:SUBLANES, :LANES], jnp.uint32)
    zero = lax.shift_right_logical(lax.shift_right_logical(bits, jnp.uint32(31)), jnp.uint32(1))
    return pltpu.bitcast(zero, F32)[0:1, :]


def _silu_from_half(hx):
    return hx * (jnp.tanh(hx) + 1.0)


def _gelu_tanh_from_half(hx):
    c = math.sqrt(2.0 / math.pi)
    return hx * (1.0 + jnp.tanh(hx * (2.0 * c + (8.0 * 0.044715 * c) * (hx * hx))))


def _in_weights(w, lru_w, attn_w):
    u, gate, q, k, v = jnp.split(w, np.cumsum([lru_w, lru_w, attn_w, attn_w]), axis=1)
    return jnp.concatenate([v, k, q * (HEAD_DIM ** -0.5), gate * 0.5, u], axis=1).astype(BF16)


def _split_columns(proj, lru_w, attn_w):
    c0, c1, c2, c3 = attn_w, 2 * attn_w, 3 * attn_w, 3 * attn_w + lru_w
    return proj[:, :c0], proj[:, c0:c1], proj[:, c1:c2], proj[:, c2:c3], proj[:, c3:]


def _project(x_ref, gain_ref, w_ref, lru_w, attn_w):
    xn = _rms(x_ref[...], gain_ref[...]).astype(BF16)
    proj = jnp.dot(xn, w_ref[...], preferred_element_type=F32)
    return _split_columns(proj, lru_w, attn_w)


def _inproj_prompt_kernel(x_ref, gain_ref, w_ref, u_ref, gate_ref, q_ref, k_ref, vt_ref, ktail_ref,
                          vtail_ref, *, lru_w, attn_w, tiles_per_seq):
    v, k, q, gate, u = _project(x_ref, gain_ref, w_ref, lru_w, attn_w)
    u_ref[...] = u
    plane = gate_ref.shape[2]
    for hf in range(gate_ref.shape[0]):
        gate_ref[hf] = gate[:, hf * plane:(hf + 1) * plane]
    q_ref[...] = q.astype(BF16)
    k_ref[...] = k.astype(BF16)
    v_t = v.T
    for blk in range(vt_ref.shape[0]):
        vt_ref[blk] = v_t[:, blk * KEY_BLOCK:(blk + 1) * KEY_BLOCK].astype(BF16)

    @pl.when(pl.program_id(0) % tiles_per_seq == tiles_per_seq - 1)
    def _():
        ktail_ref[0] = k.T
        vtail_ref[0] = v_t


def _inproj_sample_kernel(x_ref, gain_ref, w_ref, u_ref, gate_ref, q_ref, k_ref, v_ref, kt_ref, vt_ref,
                          *, lru_w, attn_w):
    v, k, q, gate, u = _project(x_ref, gain_ref, w_ref, lru_w, attn_w)
    u_ref[...] = u
    gate_ref[...] = gate
    q_ref[...] = q.astype(BF16)
    k_ref[...] = k.astype(BF16)
    v_ref[...] = v.astype(BF16)
    tm = k.shape[0]
    n_heads = attn_w // HEAD_DIM
    for h in range(n_heads):
        hs = slice(h * HEAD_DIM, (h + 1) * HEAD_DIM)
        kt_ref[pl.ds(h, tm, stride=n_heads), :] = k[:, hs]
        vt_ref[pl.ds(h, tm, stride=n_heads), :] = v[:, hs]


def _in_projection(x2d, gain, w_bf16, lru_w, attn_w, seq_len, prompt):
    rows, d = x2d.shape
    tm = min(ROW_TILE, rows)
    assert rows % tm == 0
    n_tiles = rows // tm
    n_cols = w_bf16.shape[1]
    row_spec = lambda w: pl.BlockSpec((tm, w), lambda i: (i, 0))
    in_specs = [row_spec(d), _const_spec((1, d)), _const_spec((d, n_cols))]
    common_specs = [row_spec(lru_w), row_spec(lru_w), row_spec(attn_w), row_spec(attn_w)]
    common_shapes = [jax.ShapeDtypeStruct((rows, lru_w), F32), jax.ShapeDtypeStruct((rows, lru_w), F32),
                     jax.ShapeDtypeStruct((rows, attn_w), BF16), jax.ShapeDtypeStruct((rows, attn_w), BF16)]
    if prompt:
        assert seq_len % tm == 0 and tm % KEY_BLOCK == 0 and tm == min(PAD_ROWS, seq_len)
        planes = lru_w // MXU_DIM
        common_specs[1] = pl.BlockSpec((planes, tm, MXU_DIM), lambda i: (0, i, 0))
        common_shapes[1] = jax.ShapeDtypeStruct((planes, rows, MXU_DIM), F32)
        tiles_per_seq = seq_len // tm
        blocks = tm // KEY_BLOCK
        tail_spec = pl.BlockSpec((1, attn_w, tm), lambda i: (i // tiles_per_seq, 0, 0))
        tail_shape = jax.ShapeDtypeStruct((rows // seq_len, attn_w, tm), F32)
        body = functools.partial(_inproj_prompt_kernel, lru_w=lru_w, attn_w=attn_w, tiles_per_seq=tiles_per_seq)
        out_specs = common_specs + [pl.BlockSpec((blocks, attn_w, KEY_BLOCK), lambda i: (i, 0, 0)),
                                    tail_spec, tail_spec]
        out_shape = common_shapes + [jax.ShapeDtypeStruct((rows // KEY_BLOCK, attn_w, KEY_BLOCK), BF16),
                                     tail_shape, tail_shape]
    else:
        n_heads = attn_w // HEAD_DIM
        tail_spec = pl.BlockSpec((tm * n_heads, HEAD_DIM), lambda i: (i, 0))
        tail_shape = jax.ShapeDtypeStruct((rows * n_heads, HEAD_DIM), F32)
        body = functools.partial(_inproj_sample_kernel, lru_w=lru_w, attn_w=attn_w)
        out_specs = common_specs + [row_spec(attn_w), tail_spec, tail_spec]
        out_shape = common_shapes + [jax.ShapeDtypeStruct((rows, attn_w), BF16), tail_shape, tail_shape]
    return pl.pallas_call(
        body, grid=(n_tiles,), in_specs=in_specs, out_specs=out_specs, out_shape=out_shape,
        compiler_params=_params(("arbitrary",)), name="in_projection",
    )(x2d, gain, w_bf16)


def _log_sigmoid(x):
    return jnp.minimum(x, 0.0) - jnp.log1p(jnp.exp(-jnp.abs(x)))


def _conv_rows(ext_s, r0, n, w, bias):
    hist = CONV_WIDTH - 1
    uc = bias + ext_s[r0:r0 + n, :] * w[hist:hist + 1]
    for kk in range(hist):
        uc = uc + ext_s[r0 - hist + kk:r0 - hist + kk + n, :] * w[kk:kk + 1]
    return uc


def _gate_terms(half_pre_a, half_pre_x, half_uc, half_ba, half_bx, half_c_log_sig_lam):
    t_r = jnp.tanh(half_pre_a + half_ba)
    t_i = jnp.tanh(half_pre_x + half_bx)
    log_a = half_c_log_sig_lam * t_r + half_c_log_sig_lam
    a = jnp.exp(log_a)
    s = -jnp.tanh(log_a)
    y = (2.0 * s) / (1.0 + s)
    gain = jnp.where(y > 0.0, y * lax.rsqrt(y), 0.0)
    return a, gain * (half_uc * (t_i + 1.0))


def _scan_masks(width):
    row = lax.broadcasted_iota(jnp.int32, (SUBLANES, width), 0)
    return [row >= (1 << i) for i in range(SUBLANES.bit_length() - 1)]


def _scan_group(a, b, h_prev, masks):
    for i, keep in enumerate(masks):
        a_sh = jnp.where(keep, pltpu.roll(a, 1 << i, 0), 1.0)
        b_sh = jnp.where(keep, pltpu.roll(b, 1 << i, 0), 0.0)
        b = a * b_sh + b
        a = a * a_sh
    return a * h_prev + b


def _lru_kernel(u_ref, gate_ref, conv0_ref, h0_ref, cw_ref, cb_ref, wbd_ref, ba_ref, bx_ref, lam_ref,
                y_ref, convn_ref, hlast_ref, ext_s, a_s, b_s, h_s, hc_s, *, tc):
    c = pl.program_id(1)
    hist = CONV_WIDTH - 1
    base = SUBLANES
    half_c_lsl = (0.5 * LRU_C) * _log_sigmoid(lam_ref[...])
    half = wbd_ref.shape[1]
    masks = _scan_masks(a_s.shape[2])
    n_seqs = u_ref.shape[0]

    @pl.when(c == 0)
    def _():
        for i in range(n_seqs):
            ext_s[i, base - hist:base, :] = conv0_ref[i]
            hc_s[i] = h0_ref[i]

    for i in range(n_seqs):
        ext_s[i, base:base + tc, :] = u_ref[i]
        uc = _conv_rows(ext_s.at[i], base, tc, cw_ref[...], cb_ref[...])
        tail = ext_s[i, base + tc - hist:base + tc, :]
        ext_s[i, base - hist:base, :] = tail
        convn_ref[i] = tail

        ucb = uc.astype(BF16)
        for hf in range(wbd_ref.shape[0]):
            cols = slice(hf * half, (hf + 1) * half)
            pre = jnp.dot(ucb[:, cols], wbd_ref[hf], preferred_element_type=F32)
            a, b = _gate_terms(pre[:, :half], pre[:, half:], uc[:, cols], ba_ref[:, cols], bx_ref[:, cols],
                               half_c_lsl[:, cols])
            a_s[i, :, cols] = a
            b_s[i, :, cols] = b

        def group(gi, h_prev):
            r0 = pl.multiple_of(gi * SUBLANES, SUBLANES)
            h = _scan_group(a_s[i, pl.ds(r0, SUBLANES), :], b_s[i, pl.ds(r0, SUBLANES), :], h_prev, masks)
            h_s[i, pl.ds(r0, SUBLANES), :] = h
            return h[SUBLANES - 1:SUBLANES, :]

        h_last = lax.fori_loop(0, tc // SUBLANES, group, hc_s[i], unroll=tc // SUBLANES <= LRU_UNROLL_GROUPS)
        hc_s[i] = h_last
        hlast_ref[i] = h_last
        y_ref[i] = _gelu_tanh_from_half(gate_ref[i]) * h_s[i]


def _lru_group(u, gate, conv0, h0, conv_w, conv_b, wbd, ba, bx, lam):
    b, t, w = u.shape
    tc = min(LRU_TIME_TILE, t)
    assert t % tc == 0 and tc % SUBLANES == 0 and tc >= CONV_WIDTH - 1
    hist = CONV_WIDTH - 1
    per_step = math.gcd(b, max(1, LRU_ROWS_PER_STEP // tc))
    seq_spec = pl.BlockSpec((per_step, tc, w), lambda i, c: (i, c, 0))
    per_batch = lambda r: pl.BlockSpec((per_step, r, w), lambda i, c: (i, 0, 0))
    return pl.pallas_call(
        functools.partial(_lru_kernel, tc=tc),
        grid=(b // per_step, t // tc),
        in_specs=[seq_spec, seq_spec, per_batch(hist), per_batch(1),
                  _const_spec(conv_w.shape), _const_spec((1, w)), _const_spec(wbd.shape),
                  _const_spec((1, w)), _const_spec((1, w)), _const_spec((1, w))],
        out_specs=[seq_spec, per_batch(hist), per_batch(1)],
        out_shape=[jax.ShapeDtypeStruct((b, t, w), F32), jax.ShapeDtypeStruct((b, hist, w), F32),
                   jax.ShapeDtypeStruct((b, 1, w), F32)],
        scratch_shapes=[pltpu.VMEM((per_step, SUBLANES + tc, w), F32), pltpu.VMEM((per_step, tc, w), F32),
                        pltpu.VMEM((per_step, tc, w), F32), pltpu.VMEM((per_step, tc, w), F32),
                        pltpu.VMEM((per_step, 1, w), F32)],
        compiler_params=_params(("arbitrary", "arbitrary")),
        name="rg_lru",
    )(u, gate, conv0, h0.reshape(b, 1, w), conv_w, conv_b, wbd, ba, bx, lam)


def _gap_kind(gap):
    if gap < 0 or gap > LEFT_CHUNKS:
        return "masked"
    return "near" if gap < NEAR_GAPS else "far"


def _attn_prompt_kernel(q_ref, k_ref, vt_ref, bt_ref, o_ref, s_scr, p_scr, ot_scr, *, n_heads, n_blocks):
    lane = lax.broadcasted_iota(jnp.int32, (1, LANES), 1)
    head_lanes = (lane < HEAD_DIM, lane >= HEAD_DIM)
    q_chunks_per_tile = LANES // CHUNK
    zero_tile = jnp.zeros((CHUNK, LANES), BF16)

    def block(lead, q_start, k_start, vt_start):
        k_rows = lead * CHUNK + Q_BLOCK
        q_rows = pl.ds(q_start, Q_BLOCK)
        n_lt = Q_BLOCK // LANES
        plan = []
        for kc in range(k_rows // CHUNK):
            gaps = [lt * q_chunks_per_tile + lead - kc for lt in range(n_lt)]
            plan.append([None if _gap_kind(g) == "masked" and _gap_kind(g + 1) == "masked" else g
                         for g in gaps])

        def scores(h):
            pair = slice((h // 2) * LANES, (h // 2 + 1) * LANES)
            qm = jnp.where(head_lanes[h % 2], q_ref[0, q_rows, pair], jnp.zeros((), BF16))
            for r0 in range(0, k_rows, k_rows // 2):
                kw = k_ref[0, pl.ds(k_start + r0, k_rows // 2), pair]
                s_scr[h % 2, r0:r0 + k_rows // 2, :] = lax.dot_general(kw, qm, NT_DIMS,
                                                                       preferred_element_type=F32)

        def softmax(h):
            slot = h % 2

            def tiles(kc):
                t = s_scr[slot, kc * CHUNK:(kc + 1) * CHUNK, :]
                out = []
                for lt, gap in enumerate(plan[kc]):
                    x = None
                    if gap is not None:
                        x = t[:, lt * LANES:(lt + 1) * LANES]
                        if gap in TILE_GAPS:
                            x = x + bt_ref[h, TILE_GAPS.index(gap)]
                    out.append(x)
                return out

            m = [None] * n_lt
            for kc in range(len(plan)):
                for lt, x in enumerate(tiles(kc)):
                    if x is not None:
                        m[lt] = x if m[lt] is None else jnp.maximum(m[lt], x)
            m = [jnp.max(x, axis=0, keepdims=True) for x in m]
            l = [None] * n_lt
            for kc in range(len(plan)):
                ps = []
                for lt, x in enumerate(tiles(kc)):
                    if x is None:
                        ps.append(zero_tile)
                    else:
                        p = jnp.exp(x - m[lt])
                        l[lt] = p if l[lt] is None else l[lt] + p
                        ps.append(p.astype(BF16))
                p_scr[slot, kc * CHUNK:(kc + 1) * CHUNK, :] = jnp.concatenate(ps, axis=1)
            return jnp.concatenate([1.0 / jnp.sum(x, axis=0, keepdims=True) for x in l], axis=1)

        def weighted_values(h, inv_l):
            rows = slice(h * HEAD_DIM, (h + 1) * HEAD_DIM)
            acc = None
            for blk in range(k_rows // KEY_BLOCK):
                part = jnp.dot(vt_ref[vt_start + blk, rows, :],
                               p_scr[h % 2, blk * KEY_BLOCK:(blk + 1) * KEY_BLOCK, :],
                               preferred_element_type=F32)
                acc = part if acc is None else acc + part
            ot_scr[rows, :] = acc * inv_l

        scores(0)
        inv_prev = None
        for h in range(n_heads):
            if h + 1 < n_heads:
                scores(h + 1)
            if h > 0:
                weighted_values(h - 1, inv_prev)
            inv_prev = softmax(h)
        weighted_values(n_heads - 1, inv_prev)
        o_ref[0, q_rows, :] = ot_scr[...].T

    ramp = PAD_ROWS // Q_BLOCK
    for jj in range(min(ramp, n_blocks)):
        block(jj * (Q_BLOCK // CHUNK), jj * Q_BLOCK, 0, 0)

    def full_block(j, carry):
        block(LEFT_CHUNKS, pl.multiple_of(j * Q_BLOCK, Q_BLOCK),
              pl.multiple_of((j - ramp) * Q_BLOCK, Q_BLOCK), j - ramp)
        return carry

    lax.fori_loop(ramp, n_blocks, full_block, 0)


def _rel_bias_matrix(table, rows, cols, dist00):
    n = rows + cols - 1
    dist = np.arange(n) - (cols - 1) + dist00
    g = table[:, np.clip(dist, -MAX_REL, MAX_REL) + MAX_REL].astype(F32)
    period = rows + cols
    g = jnp.pad(g, ((0, 0), (0, period - n)))
    flat = jnp.tile(g, (1, rows + 1))[:, :rows * (period + 1)]
    hankel = flat.reshape(-1, rows, period + 1)[:, :, :cols]
    return hankel[:, :, ::-1]


def _prompt_bias_tiles(table):
    far = table[:, 2 * MAX_REL].astype(F32)[:, None, None]

    def chunk_pair(gap):
        kind = _gap_kind(gap)
        if kind == "near":
            return jnp.swapaxes(_rel_bias_matrix(table, CHUNK, CHUNK, gap * CHUNK), 1, 2) - far
        fill = NEG_BIG if kind == "masked" else 0.0
        return jnp.full((table.shape[0], CHUNK, CHUNK), fill, F32)

    tiles = [jnp.concatenate([chunk_pair(g), chunk_pair(g + 1)], axis=2) for g in TILE_GAPS]
    return jnp.stack(tiles, axis=1)


def _attend_prompt(q, k, v_t, table):
    b, t, w = q.shape
    n_heads = w // HEAD_DIM
    assert t % Q_BLOCK == 0 and PAD_ROWS % Q_BLOCK == 0 and LANES // CHUNK == 2 and Q_BLOCK % LANES == 0
    n_blocks = t // Q_BLOCK
    k_rows = PAD_ROWS + Q_BLOCK
    tiles = _prompt_bias_tiles(table)
    seq = pl.BlockSpec((1, t, w), lambda i: (i, 0, 0))
    return pl.pallas_call(
        functools.partial(_attn_prompt_kernel, n_heads=n_heads, n_blocks=n_blocks),
        grid=(b,),
        in_specs=[seq, seq, pl.BlockSpec((n_blocks, w, KEY_BLOCK), lambda i: (i, 0, 0)),
                  _const_spec(tiles.shape)],
        out_specs=seq,
        out_shape=jax.ShapeDtypeStruct((b, t, w), F32),
        scratch_shapes=[pltpu.VMEM((2, k_rows, Q_BLOCK), F32), pltpu.VMEM((2, k_rows, Q_BLOCK), BF16),
                        pltpu.VMEM((w, Q_BLOCK), F32)],
        compiler_params=_params(("arbitrary",)),
        name="attn_prompt",
    )(q, k, v_t, tiles)


def _attn_sample_kernel(q_ref, k_ref, v_ref, ckt_ref, cvt_ref, bc_ref, bn_ref, o_ref, *, n_heads):
    for i in range(q_ref.shape[0]):
        for h in range(n_heads):
            hs = slice(h * HEAD_DIM, (h + 1) * HEAD_DIM)
            qh = q_ref[i, :, hs]
            s_c = jnp.dot(qh, ckt_ref[i, hs, :].astype(BF16), preferred_element_type=F32) + bc_ref[h]
            s_n = lax.dot_general(qh, k_ref[i, :, hs], NT_DIMS, preferred_element_type=F32) + bn_ref[h]
            m = jnp.maximum(jnp.max(s_c, axis=-1, keepdims=True), jnp.max(s_n, axis=-1, keepdims=True))
            p_c = jnp.exp(s_c - m)
            p_n = jnp.exp(s_n - m)
            l = jnp.sum(p_c, axis=-1, keepdims=True) + jnp.sum(p_n, axis=-1, keepdims=True)
            o = lax.dot_general(p_c.astype(BF16), cvt_ref[i, hs, :].astype(BF16), NT_DIMS,
                                preferred_element_type=F32)
            o = o + jnp.dot(p_n.astype(BF16), v_ref[i, :, hs], preferred_element_type=F32)
            o_ref[i, :, hs] = o / l


def _attend_sample(q, k, v, k_cache, v_cache, table):
    b, t, w = q.shape
    n_heads = w // HEAD_DIM
    r = k_cache.shape[1]
    bias_c = _rel_bias_matrix(table, t, r, r)
    bias_n = _rel_bias_matrix(table, t, t, 0)
    channel_major = lambda c: jnp.transpose(c, (0, 2, 3, 1)).reshape(b, w, r)
    per_step = math.gcd(b, SAMPLE_SEQS_PER_STEP)
    new = pl.BlockSpec((per_step, t, w), lambda i: (i, 0, 0))
    old = pl.BlockSpec((per_step, w, r), lambda i: (i, 0, 0))
    return pl.pallas_call(
        functools.partial(_attn_sample_kernel, n_heads=n_heads),
        grid=(b // per_step,),
        in_specs=[new, new, new, old, old, _const_spec(bias_c.shape), _const_spec(bias_n.shape)],
        out_specs=new,
        out_shape=jax.ShapeDtypeStruct((b, t, w), F32),
        compiler_params=_params(("arbitrary",)),
        name="attn_sample",
    )(q, k, v, channel_major(k_cache), channel_major(v_cache), bias_c, bias_n)


def _out_projection(y_lru_planes, x, ya, gl_ref, ga_ref, wo_ref, gf_ref, token=None):
    lru_w = sum(y.shape[1] for y in y_lru_planes)
    attn_w = ya.shape[1]
    gl, ga = gl_ref[...], ga_ref[...]
    if token is not None:
        gl = gl + jnp.tile(token, (1, lru_w // LANES))
        ga = ga + jnp.tile(token, (1, attn_w // LANES))
    sum_sq = sum(jnp.sum(y * y, axis=-1, keepdims=True) for y in y_lru_planes)
    inv_rms = lax.rsqrt(sum_sq * (1.0 / lru_w) + EPS)
    mix = jnp.dot(_rms(ya, ga).astype(BF16), wo_ref[lru_w:, :], preferred_element_type=F32)
    c0 = 0
    for y in y_lru_planes:
        cols = slice(c0, c0 + y.shape[1])
        mix = mix + jnp.dot((y * inv_rms * gl[:, cols]).astype(BF16), wo_ref[cols, :],
                            preferred_element_type=F32)
        c0 += y.shape[1]
    x1 = x + mix
    return x1, _rms(x1, gf_ref[...]).astype(BF16)


def _ffn_column_pipeline(xn_s, acc_s, buf, hmid_s, wg_ref, wu_ref, wd_ref, side_work=()):
    n_tiles = wg_ref.shape[1] // FFN_TILE

    def gate_up(c):
        cols = slice(c * FFN_TILE, (c + 1) * FFN_TILE)
        xn = xn_s[buf]
        return (jnp.dot(xn, wg_ref[:, cols], preferred_element_type=F32),
                jnp.dot(xn, wu_ref[:, cols], preferred_element_type=F32))

    side_work = list(side_work)
    pending = gate_up(0)
    for c in range(n_tiles):
        upcoming = gate_up(c + 1) if c + 1 < n_tiles else None
        g, up = pending
        hmid = _silu_from_half(g) * up
        hmid_s[:, c * FFN_TILE:(c + 1) * FFN_TILE] = hmid.astype(BF16)
        if (c + 1) % DOWN_GROUP == 0 or c + 1 == n_tiles:
            ks = slice((c // DOWN_GROUP) * DOWN_GROUP * FFN_TILE, (c + 1) * FFN_TILE)
            acc_s[buf] = acc_s[buf] + jnp.dot(hmid_s[:, ks], wd_ref[ks, :], preferred_element_type=F32)
        token = _zero_token(hmid[-SUBLANES:, -LANES:])
        for _ in range(-(-len(side_work) // (n_tiles - c))):
            side_work.pop(0)(token)
        pending = upcoming
    assert not side_work


def _out_ffn_kernel(x_ref, yl_ref, ya_ref, gl_ref, ga_ref, wo_ref, gf_ref, wg_ref, wu_ref, wd_ref,
                    gfin_ref, o_ref, xn_s, hmid_s, acc_s, *, final_norm):
    x1, xn = _out_projection([yl_ref[...]], x_ref[...], ya_ref[...], gl_ref, ga_ref, wo_ref, gf_ref)
    acc_s[0] = x1
    xn_s[0] = xn
    _ffn_column_pipeline(xn_s, acc_s, 0, hmid_s, wg_ref, wu_ref, wd_ref)
    o_ref[...] = _rms(acc_s[0], gfin_ref[...]) if final_norm else acc_s[0]


def _ffn_scratch(tm, d, d_ff, bufs):
    assert d_ff % FFN_TILE == 0
    return [pltpu.VMEM((bufs, tm, d), BF16), pltpu.VMEM((tm, d_ff), BF16), pltpu.VMEM((bufs, tm, d), F32)]


def _out_ffn(x2d, y_lru, y_att, gl, ga, wo, gf, wg, wu, wd, gfin, final_norm):
    rows, d = x2d.shape
    tm = min(ROW_TILE, rows)
    assert rows % tm == 0
    lru_w, attn_w, d_ff = y_lru.shape[1], y_att.shape[1], wg.shape[1]
    row_spec = lambda w: pl.BlockSpec((tm, w), lambda i: (i, 0))
    return pl.pallas_call(
        functools.partial(_out_ffn_kernel, final_norm=final_norm),
        grid=(rows // tm,),
        in_specs=[row_spec(d), row_spec(lru_w), row_spec(attn_w), _const_spec((1, lru_w)),
                  _const_spec((1, attn_w)), _const_spec(wo.shape), _const_spec((1, d)),
                  _const_spec(wg.shape), _const_spec(wu.shape), _const_spec(wd.shape), _const_spec((1, d))],
        out_specs=row_spec(d),
        out_shape=jax.ShapeDtypeStruct((rows, d), F32),
        scratch_shapes=_ffn_scratch(tm, d, d_ff, 1),
        compiler_params=_params(("arbitrary",)),
        name="out_ffn",
    )(x2d, y_lru, y_att, gl, ga, wo, gf, wg, wu, wd, gfin)


def _lru_out_ffn_kernel(x_ref, ya_ref, u_ref, gate_ref, cw_ref, cb_ref, wbd_ref, ba_ref, bx_ref, lam_ref,
                        gl_ref, ga_ref, wo_ref, gf_ref, wg_ref, wu_ref, wd_ref, gfin_ref,
                        o_ref, convn_ref, hlast_ref,
                        ext_s, uc_s, ucb_s, pre_s, y_s, hc_s, xn_s, hmid_s, acc_s,
                        *, tiles_per_seq, n_row_tiles, final_norm):
    s = pl.program_id(0)
    tm, lru_w = u_ref.shape
    hist = CONV_WIDTH - 1
    base = SUBLANES
    n_halves, half = wbd_ref.shape[0], wbd_ref.shape[1]
    slot = s % 2

    @pl.when(s == 0)
    def _():
        y_s[1] = jnp.zeros((n_halves, tm, half), F32)
        ext_s[base - hist:base, :] = jnp.zeros((hist, lru_w), F32)
        hc_s[...] = jnp.zeros((1, lru_w), F32)

    seq_start = (s % tiles_per_seq) == 0
    ext_s[base - hist:base, :] = jnp.where(seq_start, 0.0, ext_s[base - hist:base, :])
    ext_s[base:base + tm, :] = u_ref[...]
    h_start = jnp.where(seq_start, 0.0, hc_s[...])
    carry = [h_start[:, hf * half:(hf + 1) * half] for hf in range(n_halves)]
    half_c_lsl = (0.5 * LRU_C) * _log_sigmoid(lam_ref[...])
    masks = _scan_masks(half)

    def conv(pb, token):
        rows = slice(pb * LRU_PIECE_ROWS, (pb + 1) * LRU_PIECE_ROWS)
        bias = cb_ref[...] + jnp.tile(token, (1, lru_w // LANES))
        uc = _conv_rows(ext_s, base + pb * LRU_PIECE_ROWS, LRU_PIECE_ROWS, cw_ref[...], bias)
        for hf in range(n_halves):
            plane = uc[:, hf * half:(hf + 1) * half]
            uc_s[hf, rows, :] = plane
            ucb_s[hf, rows, :] = plane.astype(BF16)

    def gate_matmul(gb, token):
        del token
        rows = slice(gb * LRU_GATE_ROWS, (gb + 1) * LRU_GATE_ROWS)
        for hf in range(n_halves):
            pre = jnp.dot(ucb_s[hf, rows, :], wbd_ref[hf], preferred_element_type=F32)
            pre_s[2 * hf, rows, :] = pre[:, :half]
            pre_s[2 * hf + 1, rows, :] = pre[:, half:]

    def recur(pb, hf, token):
        rows = slice(pb * LRU_PIECE_ROWS, (pb + 1) * LRU_PIECE_ROWS)
        cols = slice(hf * half, (hf + 1) * half)
        zero = jnp.tile(token, (1, half // LANES))
        a, b = _gate_terms(pre_s[2 * hf, rows, :], pre_s[2 * hf + 1, rows, :],
                           uc_s[hf, rows, :], ba_ref[:, cols] + zero, bx_ref[:, cols] + zero,
                           half_c_lsl[:, cols])
        h_prev = carry[hf]
        hs = []
        for g in range(LRU_PIECE_ROWS // SUBLANES):
            grp = slice(g * SUBLANES, (g + 1) * SUBLANES)
            h = _scan_group(a[grp], b[grp], h_prev, masks)
            h_prev = h[SUBLANES - 1:SUBLANES, :]
            hs.append(h)
        carry[hf] = h_prev
        y_s[slot, hf, rows, :] = _gelu_tanh_from_half(gate_ref[hf, rows, :]) * jnp.concatenate(hs, axis=0)

    per = LRU_GATE_ROWS // LRU_PIECE_ROWS
    n_gate_blocks = tm // LRU_GATE_ROWS
    side_work = [functools.partial(conv, pb) for pb in range(per)]
    for gb in range(n_gate_blocks):
        side_work.append(functools.partial(gate_matmul, gb))
        for i in range(per):
            if gb + 1 < n_gate_blocks:
                side_work.append(functools.partial(conv, (gb + 1) * per + i))
            side_work += [functools.partial(recur, gb * per + i, hf) for hf in range(n_halves)]

    x1, xn = _out_projection([y_s[1 - slot, hf] for hf in range(n_halves)], x_ref[...], ya_ref[...],
                             gl_ref, ga_ref, wo_ref, gf_ref)
    acc_s[0] = x1
    xn_s[0] = xn
    _ffn_column_pipeline(xn_s, acc_s, 0, hmid_s, wg_ref, wu_ref, wd_ref, side_work)
    o_ref[...] = _rms(acc_s[0], gfin_ref[...]) if final_norm else acc_s[0]

    tail = ext_s[base + tm - hist:base + tm, :]
    ext_s[base - hist:base, :] = tail
    h_end = jnp.concatenate(carry, axis=1)
    hc_s[...] = h_end

    @pl.when(s < n_row_tiles)
    def _():
        convn_ref[0] = tail
        hlast_ref[0] = h_end


def _lru_out_ffn(x2d, y_att, u, gate, seq_len, p, gfin, final_norm):
    rows, d = x2d.shape
    tm = ROW_TILE
    lru_w, attn_w = u.shape[1], y_att.shape[1]
    d_ff = p["w_gate"].shape[1]
    n_halves, half = p["wbd"].shape[0], p["wbd"].shape[1]
    assert rows % tm == 0 and seq_len % tm == 0 and gate.shape == (n_halves, rows, half)
    assert tm % LRU_GATE_ROWS == 0 and LRU_GATE_ROWS % LRU_PIECE_ROWS == 0
    n = rows // tm
    tiles_per_seq = seq_len // tm
    hist = CONV_WIDTH - 1
    tile = lambda s, lag: jnp.clip(s - lag, 0, n - 1)
    rows_of = lambda w, lag: pl.BlockSpec((tm, w), lambda s: (tile(s, lag), 0))
    per_seq = lambda r: pl.BlockSpec((1, r, lru_w), lambda s: (tile(s, 0) // tiles_per_seq, 0, 0))
    consts = [p["conv_w"], p["conv_b"], p["wbd"], p["lru_ba"], p["lru_bx"], p["lru_lambda"],
              p["norm_lru_out"], p["norm_attn_out"], p["w_out"], p["norm_ffn"], p["w_gate"], p["w_up"],
              p["w_down"], gfin]
    return pl.pallas_call(
        functools.partial(_lru_out_ffn_kernel, tiles_per_seq=tiles_per_seq, n_row_tiles=n, final_norm=final_norm),
        grid=(n + 1,),
        in_specs=[rows_of(d, 1), rows_of(attn_w, 1), rows_of(lru_w, 0),
                  pl.BlockSpec((n_halves, tm, half), lambda s: (0, tile(s, 0), 0))]
                 + [_const_spec(c.shape) for c in consts],
        out_specs=[rows_of(d, 1), per_seq(hist), per_seq(1)],
        out_shape=[jax.ShapeDtypeStruct((rows, d), F32),
                   jax.ShapeDtypeStruct((rows // seq_len, hist, lru_w), F32),
                   jax.ShapeDtypeStruct((rows // seq_len, 1, lru_w), F32)],
        scratch_shapes=[pltpu.VMEM((SUBLANES + tm, lru_w), F32), pltpu.VMEM((n_halves, tm, half), F32),
                        pltpu.VMEM((n_halves, tm, half), BF16), pltpu.VMEM((2 * n_halves, tm, half), F32),
                        pltpu.VMEM((2, n_halves, tm, half), F32), pltpu.VMEM((1, lru_w), F32)]
                       + _ffn_scratch(tm, d, d_ff, 1),
        compiler_params=_params(("arbitrary",)),
        name="lru_out_ffn",
    )(x2d, y_att, u, gate, *consts)


def _block_diag_gates(wa, wx):
    n, c, _ = wa.shape
    per = MXU_DIM // c
    assert n % per == 0
    eye = jnp.eye(per, dtype=wa.dtype)

    def bd(w):
        w = w.reshape(n // per, per, c, c)
        return (eye[None, :, None, :, None] * w[:, :, :, None, :]).reshape(n // per, per * c, per * c)

    return jnp.concatenate([bd(wa), bd(wx)], axis=-1).astype(BF16)


def _layer(x, conv0, h0, prompt, attend, final_gain, p):
    b, t, d = x.shape
    lru_w = p["conv_w"].shape[1]
    attn_w = p["norm_attn_out"].shape[1]
    n_heads = attn_w // HEAD_DIM
    rows = b * t
    x2d = x.reshape(rows, d)
    u, gate, q, k, v, k_tail, v_tail = _in_projection(x2d, p["norm_mix"], p["w_in"], lru_w, attn_w, t, prompt)
    y_att = attend(q.reshape(b, t, attn_w), k.reshape(b, t, attn_w), v if prompt else v.reshape(b, t, attn_w))
    y_att = y_att.reshape(rows, attn_w)
    final_norm = final_gain is not None
    gfin = final_gain if final_norm else p["norm_ffn"]
    if prompt:
        y, conv_new, h_last = _lru_out_ffn(x2d, y_att, u, gate, t, p, gfin, final_norm)
    else:
        y_lru, conv_new, h_last = _lru_group(u.reshape(b, t, lru_w), gate.reshape(b, t, lru_w), conv0, h0,
                                             p["conv_w"], p["conv_b"], p["wbd"], p["lru_ba"], p["lru_bx"],
                                             p["lru_lambda"])
        y = _out_ffn(x2d, y_lru.reshape(rows, lru_w), y_att, p["norm_lru_out"], p["norm_attn_out"],
                     p["w_out"], p["norm_ffn"], p["w_gate"], p["w_up"], p["w_down"], gfin, final_norm)
    if prompt:
        keep = k_tail.shape[2]
        k_tail = k_tail.reshape(b, n_heads, HEAD_DIM, keep).transpose(0, 3, 1, 2)
        v_tail = v_tail.reshape(b, n_heads, HEAD_DIM, keep).transpose(0, 3, 1, 2)
    else:
        k_tail = k_tail.reshape(b, t, n_heads, HEAD_DIM)
        v_tail = v_tail.reshape(b, t, n_heads, HEAD_DIM)
    return y.reshape(b, t, d), conv_new, h_last.reshape(b, lru_w), k_tail, v_tail


def kernel(x_prompt, x_sample, state_conv, state_lru, cache_k, cache_v, norm_mix, w_in, conv_w, conv_b, lru_wa, lru_ba, lru_wx, lru_bx, lru_lambda, rel_bias, norm_lru_out, norm_attn_out, w_out, norm_ffn, w_gate, w_up, w_down, norm_final):
    depth = w_in.shape[0]
    xp, xs = x_prompt, x_sample
    row = lambda a: a.reshape(1, -1)
    outs = [[] for _ in range(8)]
    for l in range(depth):
        lru_w, attn_w = conv_w.shape[-1], norm_attn_out.shape[-1]
        p = dict(
            norm_mix=row(norm_mix[l]), w_in=_in_weights(w_in[l], lru_w, attn_w), conv_w=0.5 * conv_w[l],
            conv_b=row(0.5 * conv_b[l]), wbd=_block_diag_gates(lru_wa[l], lru_wx[l]),
            lru_ba=row(0.5 * lru_ba[l]), lru_bx=row(0.5 * lru_bx[l]),
            lru_lambda=row(lru_lambda[l]), norm_lru_out=row(norm_lru_out[l]),
            norm_attn_out=row(norm_attn_out[l]), w_out=w_out[l].astype(BF16), norm_ffn=row(norm_ffn[l]),
            w_gate=(0.5 * w_gate[l]).astype(BF16), w_up=w_up[l].astype(BF16), w_down=w_down[l].astype(BF16))
        final_gain = row(norm_final) if l == depth - 1 else None
        att_p = functools.partial(_attend_prompt, table=rel_bias[l])
        xp, cp, hp, kp, vp = _layer(xp, None, None, True, att_p, final_gain, p)
        att_s = functools.partial(_attend_sample, k_cache=cache_k[l], v_cache=cache_v[l], table=rel_bias[l])
        xs, cs, hs, ksn, vsn = _layer(xs, state_conv[l], state_lru[l], False, att_s, final_gain, p)
        for lst, val in zip(outs, (cp, hp, kp, vp, cs, hs, ksn, vsn)):
            lst.append(val)
    return (xp, xs) + tuple(jnp.stack(o) for o in outs)
```

```python
import functools
import math

import jax
import jax.numpy as jnp
import numpy as np
from jax import lax
from jax.experimental import pallas as pl
from jax.experimental.pallas import tpu as pltpu

F32 = jnp.float32
BF16 = jnp.bfloat16

EPS = 1e-6
CHUNK = 64
LEFT_CHUNKS = 8
PAD_ROWS = LEFT_CHUNKS * CHUNK
MAX_REL = 128
HEAD_DIM = 64
N_LRU_BLOCKS = 8
CONV_WIDTH = 4
LRU_C = 8.0
NEG_BIG = -1e30

ROW_TILE = 512
LRU_TIME_TILE = 512
LRU_ROWS_PER_STEP = 128
LRU_UNROLL_GROUPS = 8
Q_BLOCK = 256
KEY_BLOCK = Q_BLOCK
FFN_TILE = 256
DOWN_GROUP = 4
SAMPLE_SEQS_PER_STEP = 4
LRU_GATE_ROWS = 256
LRU_PIECE_ROWS = 64
SUBLANES = 8
MXU_DIM = 256
LANES = 128
VMEM_LIMIT = 56 * 1024 * 1024

NT_DIMS = (((1,), (1,)), ((), ()))

NEAR_GAPS = -(-(MAX_REL + CHUNK - 1) // CHUNK)
TILE_GAPS = (-1,) + tuple(range(NEAR_GAPS)) + (LEFT_CHUNKS,)


def _params(sem):
    return pltpu.CompilerParams(dimension_semantics=sem, vmem_limit_bytes=VMEM_LIMIT)


def _const_spec(shape):
    nd = len(shape)
    return pl.BlockSpec(shape, lambda *_: (0,) * nd, pipeline_mode=pl.Buffered(1))


def _rms(x, gain):
    return x * lax.rsqrt(jnp.mean(x * x, axis=-1, keepdims=True) + EPS) * gain


def _zero_token(v):
    bits = pltpu.bitcast(v[:SUBLANES, :LANES], jnp.uint32)
    zero = lax.shift_right_logical(lax.shift_right_logical(bits, jnp.uint32(31)), jnp.uint32(1))
    return pltpu.bitcast(zero, F32)[0:1, :]


def _silu_from_half(hx):
    return hx * (jnp.tanh(hx) + 1.0)


def _gelu_tanh_from_half(hx):
    c = math.sqrt(2.0 / math.pi)
    return hx * (1.0 + jnp.tanh(hx * (2.0 * c + (8.0 * 0.044715 * c) * (hx * hx))))


def _in_weights(w, lru_w, attn_w):
    u, gate, q, k, v = jnp.split(w, np.cumsum([lru_w, lru_w, attn_w, attn_w]), axis=1)
    return jnp.concatenate([v, k, q * (HEAD_DIM ** -0.5), gate * 0.5, u], axis=1).astype(BF16)


def _split_columns(proj, lru_w, attn_w):
    c0, c1, c2, c3 = attn_w, 2 * attn_w, 3 * attn_w, 3 * attn_w + lru_w
    return proj[:, :c0], proj[:, c0:c1], proj[:, c1:c2], proj[:, c2:c3], proj[:, c3:]


def _project(x_ref, gain_ref, w_ref, lru_w, attn_w):
    xn = _rms(x_ref[...], gain_ref[...]).astype(BF16)
    proj = jnp.dot(xn, w_ref[...], preferred_element_type=F32)
    return _split_columns(proj, lru_w, attn_w)


def _inproj_prompt_kernel(x_ref, gain_ref, w_ref, u_ref, gate_ref, q_ref, k_ref, vt_ref, ktail_ref,
                          vtail_ref, *, lru_w, attn_w, tiles_per_seq):
    v, k, q, gate, u = _project(x_ref, gain_ref, w_ref, lru_w, attn_w)
    u_ref[...] = u
    plane = gate_ref.shape[2]
    for hf in range(gate_ref.shape[0]):
        gate_ref[hf] = gate[:, hf * plane:(hf + 1) * plane]
    q_ref[...] = q.astype(BF16)
    k_ref[...] = k.astype(BF16)
    v_t = v.T
    for blk in range(vt_ref.shape[0]):
        vt_ref[blk] = v_t[:, blk * KEY_BLOCK:(blk + 1) * KEY_BLOCK].astype(BF16)

    @pl.when(pl.program_id(0) % tiles_per_seq == tiles_per_seq - 1)
    def _():
        ktail_ref[0] = k.T
        vtail_ref[0] = v_t


def _inproj_sample_kernel(x_ref, gain_ref, w_ref, u_ref, gate_ref, q_ref, k_ref, v_ref, kt_ref, vt_ref,
                          *, lru_w, attn_w):
    v, k, q, gate, u = _project(x_ref, gain_ref, w_ref, lru_w, attn_w)
    u_ref[...] = u
    gate_ref[...] = gate
    q_ref[...] = q.astype(BF16)
    k_ref[...] = k.astype(BF16)
    v_ref[...] = v.astype(BF16)
    tm = k.shape[0]
    n_heads = attn_w // HEAD_DIM
    for h in range(n_heads):
        hs = slice(h * HEAD_DIM, (h + 1) * HEAD_DIM)
        kt_ref[pl.ds(h, tm, stride=n_heads), :] = k[:, hs]
        vt_ref[pl.ds(h, tm, stride=n_heads), :] = v[:, hs]


def _in_projection(x2d, gain, w_bf16, lru_w, attn_w, seq_len, prompt):
    rows, d = x2d.shape
    tm = min(ROW_TILE, rows)
    assert rows % tm == 0
    n_tiles = rows // tm
    n_cols = w_bf16.shape[1]
    row_spec = lambda w: pl.BlockSpec((tm, w), lambda i: (i, 0))
    in_specs = [row_spec(d), _const_spec((1, d)), _const_spec((d, n_cols))]
    common_specs = [row_spec(lru_w), row_spec(lru_w), row_spec(attn_w), row_spec(attn_w)]
    common_shapes = [jax.ShapeDtypeStruct((rows, lru_w), F32), jax.ShapeDtypeStruct((rows, lru_w), F32),
                     jax.ShapeDtypeStruct((rows, attn_w), BF16), jax.ShapeDtypeStruct((rows, attn_w), BF16)]
    if prompt:
        assert seq_len % tm == 0 and tm % KEY_BLOCK == 0 and tm == min(PAD_ROWS, seq_len)
        planes = lru_w // MXU_DIM
        common_specs[1] = pl.BlockSpec((planes, tm, MXU_DIM), lambda i: (0, i, 0))
        common_shapes[1] = jax.ShapeDtypeStruct((planes, rows, MXU_DIM), F32)
        tiles_per_seq = seq_len // tm
        blocks = tm // KEY_BLOCK
        tail_spec = pl.BlockSpec((1, attn_w, tm), lambda i: (i // tiles_per_seq, 0, 0))
        tail_shape = jax.ShapeDtypeStruct((rows // seq_len, attn_w, tm), F32)
        body = functools.partial(_inproj_prompt_kernel, lru_w=lru_w, attn_w=attn_w, tiles_per_seq=tiles_per_seq)
        out_specs = common_specs + [pl.BlockSpec((blocks, attn_w, KEY_BLOCK), lambda i: (i, 0, 0)),
                                    tail_spec, tail_spec]
        out_shape = common_shapes + [jax.ShapeDtypeStruct((rows // KEY_BLOCK, attn_w, KEY_BLOCK), BF16),
                                     tail_shape, tail_shape]
    else:
        n_heads = attn_w // HEAD_DIM
        tail_spec = pl.BlockSpec((tm * n_heads, HEAD_DIM), lambda i: (i, 0))
        tail_shape = jax.ShapeDtypeStruct((rows * n_heads, HEAD_DIM), F32)
        body = functools.partial(_inproj_sample_kernel, lru_w=lru_w, attn_w=attn_w)
        out_specs = common_specs + [row_spec(attn_w), tail_spec, tail_spec]
        out_shape = common_shapes + [jax.ShapeDtypeStruct((rows, attn_w), BF16), tail_shape, tail_shape]
    return pl.pallas_call(
        body, grid=(n_tiles,), in_specs=in_specs, out_specs=out_specs, out_shape=out_shape,
        compiler_params=_params(("arbitrary",)), name="in_projection",
    )(x2d, gain, w_bf16)


def _log_sigmoid(x):
    return jnp.minimum(x, 0.0) - jnp.log1p(jnp.exp(-jnp.abs(x)))


def _conv_rows(ext_s, r0, n, w, bias):
    hist = CONV_WIDTH - 1
    uc = bias + ext_s[r0:r0 + n, :] * w[hist:hist + 1]
    for kk in range(hist):
        uc = uc + ext_s[r0 - hist + kk:r0 - hist + kk + n, :] * w[kk:kk + 1]
    return uc


def _gate_terms(half_pre_a, half_pre_x, half_uc, half_ba, half_bx, half_c_log_sig_lam):
    t_r = jnp.tanh(half_pre_a + half_ba)
    t_i = jnp.tanh(half_pre_x + half_bx)
    log_a = half_c_log_sig_lam * t_r + half_c_log_sig_lam
    a = jnp.exp(log_a)
    s = -jnp.tanh(log_a)
    y = (2.0 * s) / (1.0 + s)
    gain = jnp.where(y > 0.0, y * lax.rsqrt(y), 0.0)
    return a, gain * (half_uc * (t_i + 1.0))


def _scan_masks(width):
    row = lax.broadcasted_iota(jnp.int32, (SUBLANES, width), 0)
    return [row >= (1 << i) for i in range(SUBLANES.bit_length() - 1)]


def _scan_group(a, b, h_prev, masks):
    for i, keep in enumerate(masks):
        a_sh = jnp.where(keep, pltpu.roll(a, 1 << i, 0), 1.0)
        b_sh = jnp.where(keep, pltpu.roll(b, 1 << i, 0), 0.0)
        b = a * b_sh + b
        a = a * a_sh
    return a * h_prev + b


def _lru_kernel(u_ref, gate_ref, conv0_ref, h0_ref, cw_ref, cb_ref, wbd_ref, ba_ref, bx_ref, lam_ref,
                y_ref, convn_ref, hlast_ref, ext_s, a_s, b_s, h_s, hc_s, *, tc):
    c = pl.program_id(1)
    hist = CONV_WIDTH - 1
    base = SUBLANES
    half_c_lsl = (0.5 * LRU_C) * _log_sigmoid(lam_ref[...])
    half = wbd_ref.shape[1]
    masks = _scan_masks(a_s.shape[2])
    n_seqs = u_ref.shape[0]

    @pl.when(c == 0)
    def _():
        for i in range(n_seqs):
            ext_s[i, base - hist:base, :] = conv0_ref[i]
            hc_s[i] = h0_ref[i]

    for i in range(n_seqs):
        ext_s[i, base:base + tc, :] = u_ref[i]
        uc = _conv_rows(ext_s.at[i], base, tc, cw_ref[...], cb_ref[...])
        tail = ext_s[i, base + tc - hist:base + tc, :]
        ext_s[i, base - hist:base, :] = tail
        convn_ref[i] = tail

        ucb = uc.astype(BF16)
        for hf in range(wbd_ref.shape[0]):
            cols = slice(hf * half, (hf + 1) * half)
            pre = jnp.dot(ucb[:, cols], wbd_ref[hf], preferred_element_type=F32)
            a, b = _gate_terms(pre[:, :half], pre[:, half:], uc[:, cols], ba_ref[:, cols], bx_ref[:, cols],
                               half_c_lsl[:, cols])
            a_s[i, :, cols] = a
            b_s[i, :, cols] = b

        def group(gi, h_prev):
            r0 = pl.multiple_of(gi * SUBLANES, SUBLANES)
            h = _scan_group(a_s[i, pl.ds(r0, SUBLANES), :], b_s[i, pl.ds(r0, SUBLANES), :], h_prev, masks)
            h_s[i, pl.ds(r0, SUBLANES), :] = h
            return h[SUBLANES - 1:SUBLANES, :]

        h_last = lax.fori_loop(0, tc // SUBLANES, group, hc_s[i], unroll=tc // SUBLANES <= LRU_UNROLL_GROUPS)
        hc_s[i] = h_last
        hlast_ref[i] = h_last
        y_ref[i] = _gelu_tanh_from_half(gate_ref[i]) * h_s[i]


def _lru_group(u, gate, conv0, h0, conv_w, conv_b, wbd, ba, bx, lam):
    b, t, w = u.shape
    tc = min(LRU_TIME_TILE, t)
    assert t % tc == 0 and tc % SUBLANES == 0 and tc >= CONV_WIDTH - 1
    hist = CONV_WIDTH - 1
    per_step = math.gcd(b, max(1, LRU_ROWS_PER_STEP // tc))
    seq_spec = pl.BlockSpec((per_step, tc, w), lambda i, c: (i, c, 0))
    per_batch = lambda r: pl.BlockSpec((per_step, r, w), lambda i, c: (i, 0, 0))
    return pl.pallas_call(
        functools.partial(_lru_kernel, tc=tc),
        grid=(b // per_step, t // tc),
        in_specs=[seq_spec, seq_spec, per_batch(hist), per_batch(1),
                  _const_spec(conv_w.shape), _const_spec((1, w)), _const_spec(wbd.shape),
                  _const_spec((1, w)), _const_spec((1, w)), _const_spec((1, w))],
        out_specs=[seq_spec, per_batch(hist), per_batch(1)],
        out_shape=[jax.ShapeDtypeStruct((b, t, w), F32), jax.ShapeDtypeStruct((b, hist, w), F32),
                   jax.ShapeDtypeStruct((b, 1, w), F32)],
        scratch_shapes=[pltpu.VMEM((per_step, SUBLANES + tc, w), F32), pltpu.VMEM((per_step, tc, w), F32),
                        pltpu.VMEM((per_step, tc, w), F32), pltpu.VMEM((per_step, tc, w), F32),
                        pltpu.VMEM((per_step, 1, w), F32)],
        compiler_params=_params(("arbitrary", "arbitrary")),
        name="rg_lru",
    )(u, gate, conv0, h0.reshape(b, 1, w), conv_w, conv_b, wbd, ba, bx, lam)


def _gap_kind(gap):
    if gap < 0 or gap > LEFT_CHUNKS:
        return "masked"
    return "near" if gap < NEAR_GAPS else "far"


def _attn_prompt_kernel(q_ref, k_ref, vt_ref, bt_ref, o_ref, s_scr, p_scr, ot_scr, *, n_heads, n_blocks):
    lane = lax.broadcasted_iota(jnp.int32, (1, LANES), 1)
    head_lanes = (lane < HEAD_DIM, lane >= HEAD_DIM)
    q_chunks_per_tile = LANES // CHUNK
    zero_tile = jnp.zeros((CHUNK, LANES), BF16)

    def block(lead, q_start, k_start, vt_start):
        k_rows = lead * CHUNK + Q_BLOCK
        q_rows = pl.ds(q_start, Q_BLOCK)
        n_lt = Q_BLOCK // LANES
        plan = []
        for kc in range(k_rows // CHUNK):
            gaps = [lt * q_chunks_per_tile + lead - kc for lt in range(n_lt)]
            plan.append([None if _gap_kind(g) == "masked" and _gap_kind(g + 1) == "masked" else g
                         for g in gaps])

        def scores(h):
            pair = slice((h // 2) * LANES, (h // 2 + 1) * LANES)
            qm = jnp.where(head_lanes[h % 2], q_ref[0, q_rows, pair], jnp.zeros((), BF16))
            for r0 in range(0, k_rows, k_rows // 2):
                kw = k_ref[0, pl.ds(k_start + r0, k_rows // 2), pair]
                s_scr[h % 2, r0:r0 + k_rows // 2, :] = lax.dot_general(kw, qm, NT_DIMS,
                                                                       preferred_element_type=F32)

        def softmax(h):
            slot = h % 2

            def tiles(kc):
                t = s_scr[slot, kc * CHUNK:(kc + 1) * CHUNK, :]
                out = []
                for lt, gap in enumerate(plan[kc]):
                    x = None
                    if gap is not None:
                        x = t[:, lt * LANES:(lt + 1) * LANES]
                        if gap in TILE_GAPS:
                            x = x + bt_ref[h, TILE_GAPS.index(gap)]
                    out.append(x)
                return out

            m = [None] * n_lt
            for kc in range(len(plan)):
                for lt, x in enumerate(tiles(kc)):
                    if x is not None:
                        m[lt] = x if m[lt] is None else jnp.maximum(m[lt], x)
            m = [jnp.max(x, axis=0, keepdims=True) for x in m]
            l = [None] * n_lt
            for kc in range(len(plan)):
                ps = []
                for lt, x in enumerate(tiles(kc)):
                    if x is None:
                        ps.append(zero_tile)
                    else:
                        p = jnp.exp(x - m[lt])
                        l[lt] = p if l[lt] is None else l[lt] + p
                        ps.append(p.astype(BF16))
                p_scr[slot, kc * CHUNK:(kc + 1) * CHUNK, :] = jnp.concatenate(ps, axis=1)
            return jnp.concatenate([1.0 / jnp.sum(x, axis=0, keepdims=True) for x in l], axis=1)

        def weighted_values(h, inv_l):
            rows = slice(h * HEAD_DIM, (h + 1) * HEAD_DIM)
            acc = None
            for blk in range(k_rows // KEY_BLOCK):
                part = jnp.dot(vt_ref[vt_start + blk, rows, :],
                               p_scr[h % 2, blk * KEY_BLOCK:(blk + 1) * KEY_BLOCK, :],
                               preferred_element_type=F32)
                acc = part if acc is None else acc + part
            ot_scr[rows, :] = acc * inv_l

        scores(0)
        inv_prev = None
        for h in range(n_heads):
            if h + 1 < n_heads:
                scores(h + 1)
            if h > 0:
                weighted_values(h - 1, inv_prev)
            inv_prev = softmax(h)
        weighted_values(n_heads - 1, inv_prev)
        o_ref[0, q_rows, :] = ot_scr[...].T

    ramp = PAD_ROWS // Q_BLOCK
    for jj in range(min(ramp, n_blocks)):
        block(jj * (Q_BLOCK // CHUNK), jj * Q_BLOCK, 0, 0)

    per_trip = next(n for n in (6, 3, 2, 1) if (n_blocks - ramp) % n == 0)

    def full_blocks(i, carry):
        for jj in range(per_trip):
            j = ramp + i * per_trip + jj
            block(LEFT_CHUNKS, pl.multiple_of(j * Q_BLOCK, Q_BLOCK),
                  pl.multiple_of((j - ramp) * Q_BLOCK, Q_BLOCK), j - ramp)
        return carry

    lax.fori_loop(0, max(n_blocks - ramp, 0) // per_trip, full_blocks, 0)


def _rel_bias_matrix(table, rows, cols, dist00):
    n = rows + cols - 1
    dist = np.arange(n) - (cols - 1) + dist00
    g = table[:, np.clip(dist, -MAX_REL, MAX_REL) + MAX_REL].astype(F32)
    period = rows + cols
    g = jnp.pad(g, ((0, 0), (0, period - n)))
    flat = jnp.tile(g, (1, rows + 1))[:, :rows * (period + 1)]
    hankel = flat.reshape(-1, rows, period + 1)[:, :, :cols]
    return hankel[:, :, ::-1]


def _prompt_bias_tiles(table):
    far = table[:, 2 * MAX_REL].astype(F32)[:, None, None]

    def chunk_pair(gap):
        kind = _gap_kind(gap)
        if kind == "near":
            return jnp.swapaxes(_rel_bias_matrix(table, CHUNK, CHUNK, gap * CHUNK), 1, 2) - far
        fill = NEG_BIG if kind == "masked" else 0.0
        return jnp.full((table.shape[0], CHUNK, CHUNK), fill, F32)

    tiles = [jnp.concatenate([chunk_pair(g), chunk_pair(g + 1)], axis=2) for g in TILE_GAPS]
    return jnp.stack(tiles, axis=1)


def _attend_prompt(q, k, v_t, table):
    b, t, w = q.shape
    n_heads = w // HEAD_DIM
    assert t % Q_BLOCK == 0 and PAD_ROWS % Q_BLOCK == 0 and LANES // CHUNK == 2 and Q_BLOCK % LANES == 0
    n_blocks = t // Q_BLOCK
    k_rows = PAD_ROWS + Q_BLOCK
    tiles = _prompt_bias_tiles(table)
    seq = pl.BlockSpec((1, t, w), lambda i: (i, 0, 0))
    return pl.pallas_call(
        functools.partial(_attn_prompt_kernel, n_heads=n_heads, n_blocks=n_blocks),
        grid=(b,),
        in_specs=[seq, seq, pl.BlockSpec((n_blocks, w, KEY_BLOCK), lambda i: (i, 0, 0)),
                  _const_spec(tiles.shape)],
        out_specs=seq,
        out_shape=jax.ShapeDtypeStruct((b, t, w), F32),
        scratch_shapes=[pltpu.VMEM((2, k_rows, Q_BLOCK), F32), pltpu.VMEM((2, k_rows, Q_BLOCK), BF16),
                        pltpu.VMEM((w, Q_BLOCK), F32)],
        compiler_params=_params(("arbitrary",)),
        name="attn_prompt",
    )(q, k, v_t, tiles)


def _attn_sample_kernel(q_ref, k_ref, v_ref, ckt_ref, cvt_ref, bc_ref, bn_ref, o_ref, *, n_heads):
    for i in range(q_ref.shape[0]):
        for h in range(n_heads):
            hs = slice(h * HEAD_DIM, (h + 1) * HEAD_DIM)
            qh = q_ref[i, :, hs]
            s_c = jnp.dot(qh, ckt_ref[i, hs, :].astype(BF16), preferred_element_type=F32) + bc_ref[h]
            s_n = lax.dot_general(qh, k_ref[i, :, hs], NT_DIMS, preferred_element_type=F32) + bn_ref[h]
            m = jnp.maximum(jnp.max(s_c, axis=-1, keepdims=True), jnp.max(s_n, axis=-1, keepdims=True))
            p_c = jnp.exp(s_c - m)
            p_n = jnp.exp(s_n - m)
            l = jnp.sum(p_c, axis=-1, keepdims=True) + jnp.sum(p_n, axis=-1, keepdims=True)
            o = lax.dot_general(p_c.astype(BF16), cvt_ref[i, hs, :].astype(BF16), NT_DIMS,
                                preferred_element_type=F32)
            o = o + jnp.dot(p_n.astype(BF16), v_ref[i, :, hs], preferred_element_type=F32)
            o_ref[i, :, hs] = o / l


def _attend_sample(q, k, v, k_cache, v_cache, table):
    b, t, w = q.shape
    n_heads = w // HEAD_DIM
    r = k_cache.shape[1]
    bias_c = _rel_bias_matrix(table, t, r, r)
    bias_n = _rel_bias_matrix(table, t, t, 0)
    channel_major = lambda c: jnp.transpose(c, (0, 2, 3, 1)).reshape(b, w, r)
    per_step = math.gcd(b, SAMPLE_SEQS_PER_STEP)
    new = pl.BlockSpec((per_step, t, w), lambda i: (i, 0, 0))
    old = pl.BlockSpec((per_step, w, r), lambda i: (i, 0, 0))
    return pl.pallas_call(
        functools.partial(_attn_sample_kernel, n_heads=n_heads),
        grid=(b // per_step,),
        in_specs=[new, new, new, old, old, _const_spec(bias_c.shape), _const_spec(bias_n.shape)],
        out_specs=new,
        out_shape=jax.ShapeDtypeStruct((b, t, w), F32),
        compiler_params=_params(("arbitrary",)),
        name="attn_sample",
    )(q, k, v, channel_major(k_cache), channel_major(v_cache), bias_c, bias_n)


def _out_projection(y_lru_planes, x, ya, gl_ref, ga_ref, wo_ref, gf_ref, token=None):
    lru_w = sum(y.shape[1] for y in y_lru_planes)
    attn_w = ya.shape[1]
    gl, ga = gl_ref[...], ga_ref[...]
    if token is not None:
        gl = gl + jnp.tile(token, (1, lru_w // LANES))
        ga = ga + jnp.tile(token, (1, attn_w // LANES))
    sum_sq = sum(jnp.sum(y * y, axis=-1, keepdims=True) for y in y_lru_planes)
    inv_rms = lax.rsqrt(sum_sq * (1.0 / lru_w) + EPS)
    mix = jnp.dot(_rms(ya, ga).astype(BF16), wo_ref[lru_w:, :], preferred_element_type=F32)
    c0 = 0
    for y in y_lru_planes:
        cols = slice(c0, c0 + y.shape[1])
        mix = mix + jnp.dot((y * inv_rms * gl[:, cols]).astype(BF16), wo_ref[cols, :],
                            preferred_element_type=F32)
        c0 += y.shape[1]
    x1 = x + mix
    return x1, _rms(x1, gf_ref[...]).astype(BF16)


def _ffn_column_pipeline(xn_s, acc_s, buf, hmid_s, wg_ref, wu_ref, wd_ref, side_work=()):
    n_tiles = wg_ref.shape[1] // FFN_TILE

    def gate_up(c):
        cols = slice(c * FFN_TILE, (c + 1) * FFN_TILE)
        xn = xn_s[buf]
        return (jnp.dot(xn, wg_ref[:, cols], preferred_element_type=F32),
                jnp.dot(xn, wu_ref[:, cols], preferred_element_type=F32))

    side_work = list(side_work)
    pending = gate_up(0)
    for c in range(n_tiles):
        upcoming = gate_up(c + 1) if c + 1 < n_tiles else None
        g, up = pending
        hmid = _silu_from_half(g) * up
        hmid_s[:, c * FFN_TILE:(c + 1) * FFN_TILE] = hmid.astype(BF16)
        if (c + 1) % DOWN_GROUP == 0 or c + 1 == n_tiles:
            ks = slice((c // DOWN_GROUP) * DOWN_GROUP * FFN_TILE, (c + 1) * FFN_TILE)
            acc_s[buf] = acc_s[buf] + jnp.dot(hmid_s[:, ks], wd_ref[ks, :], preferred_element_type=F32)
        token = _zero_token(hmid[-SUBLANES:, -LANES:])
        for _ in range(-(-len(side_work) // (n_tiles - c))):
            side_work.pop(0)(token)
        pending = upcoming
    assert not side_work


def _out_ffn_kernel(x_ref, yl_ref, ya_ref, gl_ref, ga_ref, wo_ref, gf_ref, wg_ref, wu_ref, wd_ref,
                    gfin_ref, o_ref, xn_s, hmid_s, acc_s, *, final_norm):
    x1, xn = _out_projection([yl_ref[...]], x_ref[...], ya_ref[...], gl_ref, ga_ref, wo_ref, gf_ref)
    acc_s[0] = x1
    xn_s[0] = xn
    _ffn_column_pipeline(xn_s, acc_s, 0, hmid_s, wg_ref, wu_ref, wd_ref)
    o_ref[...] = _rms(acc_s[0], gfin_ref[...]) if final_norm else acc_s[0]


def _ffn_scratch(tm, d, d_ff, bufs):
    assert d_ff % FFN_TILE == 0
    return [pltpu.VMEM((bufs, tm, d), BF16), pltpu.VMEM((tm, d_ff), BF16), pltpu.VMEM((bufs, tm, d), F32)]


def _out_ffn(x2d, y_lru, y_att, gl, ga, wo, gf, wg, wu, wd, gfin, final_norm):
    rows, d = x2d.shape
    tm = min(ROW_TILE, rows)
    assert rows % tm == 0
    lru_w, attn_w, d_ff = y_lru.shape[1], y_att.shape[1], wg.shape[1]
    row_spec = lambda w: pl.BlockSpec((tm, w), lambda i: (i, 0))
    return pl.pallas_call(
        functools.partial(_out_ffn_kernel, final_norm=final_norm),
        grid=(rows // tm,),
        in_specs=[row_spec(d), row_spec(lru_w), row_spec(attn_w), _const_spec((1, lru_w)),
                  _const_spec((1, attn_w)), _const_spec(wo.shape), _const_spec((1, d)),
                  _const_spec(wg.shape), _const_spec(wu.shape), _const_spec(wd.shape), _const_spec((1, d))],
        out_specs=row_spec(d),
        out_shape=jax.ShapeDtypeStruct((rows, d), F32),
        scratch_shapes=_ffn_scratch(tm, d, d_ff, 1),
        compiler_params=_params(("arbitrary",)),
        name="out_ffn",
    )(x2d, y_lru, y_att, gl, ga, wo, gf, wg, wu, wd, gfin)


def _lru_out_ffn_kernel(x_ref, ya_ref, u_ref, gate_ref, cw_ref, cb_ref, wbd_ref, ba_ref, bx_ref, lam_ref,
                        gl_ref, ga_ref, wo_ref, gf_ref, wg_ref, wu_ref, wd_ref, gfin_ref,
                        o_ref, convn_ref, hlast_ref,
                        ext_s, uc_s, ucb_s, pre_s, y_s, hc_s, xn_s, hmid_s, acc_s,
                        *, tiles_per_seq, n_row_tiles, final_norm):
    s = pl.program_id(0)
    tm, lru_w = u_ref.shape
    hist = CONV_WIDTH - 1
    base = SUBLANES
    n_halves, half = wbd_ref.shape[0], wbd_ref.shape[1]
    slot = s % 2

    @pl.when(s == 0)
    def _():
        y_s[1] = jnp.zeros((n_halves, tm, half), F32)
        ext_s[base - hist:base, :] = jnp.zeros((hist, lru_w), F32)
        hc_s[...] = jnp.zeros((1, lru_w), F32)

    seq_start = (s % tiles_per_seq) == 0
    ext_s[base - hist:base, :] = jnp.where(seq_start, 0.0, ext_s[base - hist:base, :])
    ext_s[base:base + tm, :] = u_ref[...]
    h_start = jnp.where(seq_start, 0.0, hc_s[...])
    carry = [h_start[:, hf * half:(hf + 1) * half] for hf in range(n_halves)]
    half_c_lsl = (0.5 * LRU_C) * _log_sigmoid(lam_ref[...])
    masks = _scan_masks(half)

    def conv(pb, token):
        rows = slice(pb * LRU_PIECE_ROWS, (pb + 1) * LRU_PIECE_ROWS)
        bias = cb_ref[...] + jnp.tile(token, (1, lru_w // LANES))
        uc = _conv_rows(ext_s, base + pb * LRU_PIECE_ROWS, LRU_PIECE_ROWS, cw_ref[...], bias)
        for hf in range(n_halves):
            plane = uc[:, hf * half:(hf + 1) * half]
            uc_s[hf, rows, :] = plane
            ucb_s[hf, rows, :] = plane.astype(BF16)

    def gate_matmul(gb, token):
        del token
        rows = slice(gb * LRU_GATE_ROWS, (gb + 1) * LRU_GATE_ROWS)
        for hf in range(n_halves):
            pre = jnp.dot(ucb_s[hf, rows, :], wbd_ref[hf], preferred_element_type=F32)
            pre_s[2 * hf, rows, :] = pre[:, :half]
            pre_s[2 * hf + 1, rows, :] = pre[:, half:]

    def recur(pb, hf, token):
        rows = slice(pb * LRU_PIECE_ROWS, (pb + 1) * LRU_PIECE_ROWS)
        cols = slice(hf * half, (hf + 1) * half)
        zero = jnp.tile(token, (1, half // LANES))
        a, b = _gate_terms(pre_s[2 * hf, rows, :], pre_s[2 * hf + 1, rows, :],
                           uc_s[hf, rows, :], ba_ref[:, cols] + zero, bx_ref[:, cols] + zero,
                           half_c_lsl[:, cols])
        h_prev = carry[hf]
        hs = []
        for g in range(LRU_PIECE_ROWS // SUBLANES):
            grp = slice(g * SUBLANES, (g + 1) * SUBLANES)
            h = _scan_group(a[grp], b[grp], h_prev, masks)
            h_prev = h[SUBLANES - 1:SUBLANES, :]
            hs.append(h)
        carry[hf] = h_prev
        y_s[slot, hf, rows, :] = _gelu_tanh_from_half(gate_ref[hf, rows, :]) * jnp.concatenate(hs, axis=0)

    per = LRU_GATE_ROWS // LRU_PIECE_ROWS
    n_gate_blocks = tm // LRU_GATE_ROWS
    side_work = [functools.partial(conv, pb) for pb in range(per)]
    for gb in range(n_gate_blocks):
        side_work.append(functools.partial(gate_matmul, gb))
        for i in range(per):
            if gb + 1 < n_gate_blocks:
                side_work.append(functools.partial(conv, (gb + 1) * per + i))
            side_work += [functools.partial(recur, gb * per + i, hf) for hf in range(n_halves)]

    x1, xn = _out_projection([y_s[1 - slot, hf] for hf in range(n_halves)], x_ref[...], ya_ref[...],
                             gl_ref, ga_ref, wo_ref, gf_ref)
    acc_s[0] = x1
    xn_s[0] = xn
    _ffn_column_pipeline(xn_s, acc_s, 0, hmid_s, wg_ref, wu_ref, wd_ref, side_work)
    o_ref[...] = _rms(acc_s[0], gfin_ref[...]) if final_norm else acc_s[0]

    tail = ext_s[base + tm - hist:base + tm, :]
    ext_s[base - hist:base, :] = tail
    h_end = jnp.concatenate(carry, axis=1)
    hc_s[...] = h_end

    @pl.when(s < n_row_tiles)
    def _():
        convn_ref[0] = tail
        hlast_ref[0] = h_end


def _lru_out_ffn(x2d, y_att, u, gate, seq_len, p, gfin, final_norm):
    rows, d = x2d.shape
    tm = ROW_TILE
    lru_w, attn_w = u.shape[1], y_att.shape[1]
    d_ff = p["w_gate"].shape[1]
    n_halves, half = p["wbd"].shape[0], p["wbd"].shape[1]
    assert rows % tm == 0 and seq_len % tm == 0 and gate.shape == (n_halves, rows, half)
    assert tm % LRU_GATE_ROWS == 0 and LRU_GATE_ROWS % LRU_PIECE_ROWS == 0
    n = rows // tm
    tiles_per_seq = seq_len // tm
    hist = CONV_WIDTH - 1
    tile = lambda s, lag: jnp.clip(s - lag, 0, n - 1)
    rows_of = lambda w, lag: pl.BlockSpec((tm, w), lambda s: (tile(s, lag), 0))
    per_seq = lambda r: pl.BlockSpec((1, r, lru_w), lambda s: (tile(s, 0) // tiles_per_seq, 0, 0))
    consts = [p["conv_w"], p["conv_b"], p["wbd"], p["lru_ba"], p["lru_bx"], p["lru_lambda"],
              p["norm_lru_out"], p["norm_attn_out"], p["w_out"], p["norm_ffn"], p["w_gate"], p["w_up"],
              p["w_down"], gfin]
    return pl.pallas_call(
        functools.partial(_lru_out_ffn_kernel, tiles_per_seq=tiles_per_seq, n_row_tiles=n, final_norm=final_norm),
        grid=(n + 1,),
        in_specs=[rows_of(d, 1), rows_of(attn_w, 1), rows_of(lru_w, 0),
                  pl.BlockSpec((n_halves, tm, half), lambda s: (0, tile(s, 0), 0))]
                 + [_const_spec(c.shape) for c in consts],
        out_specs=[rows_of(d, 1), per_seq(hist), per_seq(1)],
        out_shape=[jax.ShapeDtypeStruct((rows, d), F32),
                   jax.ShapeDtypeStruct((rows // seq_len, hist, lru_w), F32),
                   jax.ShapeDtypeStruct((rows // seq_len, 1, lru_w), F32)],
        scratch_shapes=[pltpu.VMEM((SUBLANES + tm, lru_w), F32), pltpu.VMEM((n_halves, tm, half), F32),
                        pltpu.VMEM((n_halves, tm, half), BF16), pltpu.VMEM((2 * n_halves, tm, half), F32),
                        pltpu.VMEM((2, n_halves, tm, half), F32), pltpu.VMEM((1, lru_w), F32)]
                       + _ffn_scratch(tm, d, d_ff, 1),
        compiler_params=_params(("arbitrary",)),
        name="lru_out_ffn",
    )(x2d, y_att, u, gate, *consts)


def _block_diag_gates(wa, wx):
    n, c, _ = wa.shape
    per = MXU_DIM // c
    assert n % per == 0
    eye = jnp.eye(per, dtype=wa.dtype)

    def bd(w):
        w = w.reshape(n // per, per, c, c)
        return (eye[None, :, None, :, None] * w[:, :, :, None, :]).reshape(n // per, per * c, per * c)

    return jnp.concatenate([bd(wa), bd(wx)], axis=-1).astype(BF16)


def _layer(x, conv0, h0, prompt, attend, final_gain, p):
    b, t, d = x.shape
    lru_w = p["conv_w"].shape[1]
    attn_w = p["norm_attn_out"].shape[1]
    n_heads = attn_w // HEAD_DIM
    rows = b * t
    x2d = x.reshape(rows, d)
    u, gate, q, k, v, k_tail, v_tail = _in_projection(x2d, p["norm_mix"], p["w_in"], lru_w, attn_w, t, prompt)
    y_att = attend(q.reshape(b, t, attn_w), k.reshape(b, t, attn_w), v if prompt else v.reshape(b, t, attn_w))
    y_att = y_att.reshape(rows, attn_w)
    final_norm = final_gain is not None
    gfin = final_gain if final_norm else p["norm_ffn"]
    if prompt:
        y, conv_new, h_last = _lru_out_ffn(x2d, y_att, u, gate, t, p, gfin, final_norm)
    else:
        y_lru, conv_new, h_last = _lru_group(u.reshape(b, t, lru_w), gate.reshape(b, t, lru_w), conv0, h0,
                                             p["conv_w"], p["conv_b"], p["wbd"], p["lru_ba"], p["lru_bx"],
                                             p["lru_lambda"])
        y = _out_ffn(x2d, y_lru.reshape(rows, lru_w), y_att, p["norm_lru_out"], p["norm_attn_out"],
                     p["w_out"], p["norm_ffn"], p["w_gate"], p["w_up"], p["w_down"], gfin, final_norm)
    if prompt:
        keep = k_tail.shape[2]
        k_tail = k_tail.reshape(b, n_heads, HEAD_DIM, keep).transpose(0, 3, 1, 2)
        v_tail = v_tail.reshape(b, n_heads, HEAD_DIM, keep).transpose(0, 3, 1, 2)
    else:
        k_tail = k_tail.reshape(b, t, n_heads, HEAD_DIM)
        v_tail = v_tail.reshape(b, t, n_heads, HEAD_DIM)
    return y.reshape(b, t, d), conv_new, h_last.reshape(b, lru_w), k_tail, v_tail


def kernel(x_prompt, x_sample, state_conv, state_lru, cache_k, cache_v, norm_mix, w_in, conv_w, conv_b, lru_wa, lru_ba, lru_wx, lru_bx, lru_lambda, rel_bias, norm_lru_out, norm_attn_out, w_out, norm_ffn, w_gate, w_up, w_down, norm_final):
    depth = w_in.shape[0]
    xp, xs = x_prompt, x_sample
    row = lambda a: a.reshape(1, -1)
    outs = [[] for _ in range(8)]
    for l in range(depth):
        lru_w, attn_w = conv_w.shape[-1], norm_attn_out.shape[-1]
        p = dict(
            norm_mix=row(norm_mix[l]), w_in=_in_weights(w_in[l], lru_w, attn_w), conv_w=0.5 * conv_w[l],
            conv_b=row(0.5 * conv_b[l]), wbd=_block_diag_gates(lru_wa[l], lru_wx[l]),
            lru_ba=row(0.5 * lru_ba[l]), lru_bx=row(0.5 * lru_bx[l]),
            lru_lambda=row(lru_lambda[l]), norm_lru_out=row(norm_lru_out[l]),
            norm_attn_out=row(norm_attn_out[l]), w_out=w_out[l].astype(BF16), norm_ffn=row(norm_ffn[l]),
            w_gate=(0.5 * w_gate[l]).astype(BF16), w_up=w_up[l].astype(BF16), w_down=w_down[l].astype(BF16))
        final_gain = row(norm_final) if l == depth - 1 else None
        att_p = functools.partial(_attend_prompt, table=rel_bias[l])
        xp, cp, hp, kp, vp = _layer(xp, None, None, True, att_p, final_gain, p)
        att_s = functools.partial(_attend_sample, k_cache=cache_k[l], v_cache=cache_v[l], table=rel_bias[l])
        xs, cs, hs, ksn, vsn = _layer(xs, state_conv[l], state_lru[l], False, att_s, final_gain, p)
        for lst, val in zip(outs, (cp, hp, kp, vp, cs, hs, ksn, vsn)):
            lst.append(val)
    return (xp, xs) + tuple(jnp.stack(o) for o in outs)
```

```python
import functools
import math

import jax
import jax.numpy as jnp
import numpy as np
from jax import lax
from jax.experimental import pallas as pl
from jax.experimental.pallas import tpu as pltpu

F32 = jnp.float32
BF16 = jnp.bfloat16

EPS = 1e-6
CHUNK = 64
LEFT_CHUNKS = 8
PAD_ROWS = LEFT_CHUNKS * CHUNK
MAX_REL = 128
HEAD_DIM = 64
N_LRU_BLOCKS = 8
CONV_WIDTH = 4
LRU_C = 8.0
NEG_BIG = -1e30

ROW_TILE = 512
INPROJ_TILES_PER_STEP = 2
LRU_TIME_TILE = 512
LRU_ROWS_PER_STEP = 128
LRU_UNROLL_GROUPS = 8
Q_BLOCK = 256
KEY_BLOCK = Q_BLOCK
FFN_TILE = 256
DOWN_GROUP = 4
SAMPLE_SEQS_PER_STEP = 4
LRU_GATE_ROWS = 256
LRU_PIECE_ROWS = 64
SUBLANES = 8
MXU_DIM = 256
LANES = 128
VMEM_LIMIT = 56 * 1024 * 1024

NT_DIMS = (((1,), (1,)), ((), ()))

NEAR_GAPS = -(-(MAX_REL + CHUNK - 1) // CHUNK)
TILE_GAPS = (-1,) + tuple(range(NEAR_GAPS)) + (LEFT_CHUNKS,)


def _params(sem):
    return pltpu.CompilerParams(dimension_semantics=sem, vmem_limit_bytes=VMEM_LIMIT)


def _const_spec(shape):
    nd = len(shape)
    return pl.BlockSpec(shape, lambda *_: (0,) * nd, pipeline_mode=pl.Buffered(1))


def _rms(x, gain):
    return x * lax.rsqrt(jnp.mean(x * x, axis=-1, keepdims=True) + EPS) * gain


def _zero_token(v):
    bits = pltpu.bitcast(v[:SUBLANES, :LANES], jnp.uint32)
    zero = lax.shift_right_logical(lax.shift_right_logical(bits, jnp.uint32(31)), jnp.uint32(1))
    return pltpu.bitcast(zero, F32)[0:1, :]


def _silu_from_half(hx):
    return hx * (jnp.tanh(hx) + 1.0)


def _gelu_tanh_from_half(hx):
    c = math.sqrt(2.0 / math.pi)
    return hx * (1.0 + jnp.tanh(hx * (2.0 * c + (8.0 * 0.044715 * c) * (hx * hx))))


def _in_weights(w, lru_w, attn_w):
    u, gate, q, k, v = jnp.split(w, np.cumsum([lru_w, lru_w, attn_w, attn_w]), axis=1)
    return jnp.concatenate([v, k, q * (HEAD_DIM ** -0.5), gate * 0.5, u], axis=1).astype(BF16)


def _split_columns(proj, lru_w, attn_w):
    c0, c1, c2, c3 = attn_w, 2 * attn_w, 3 * attn_w, 3 * attn_w + lru_w
    return proj[:, :c0], proj[:, c0:c1], proj[:, c1:c2], proj[:, c2:c3], proj[:, c3:]


def _project(x_ref, gain_ref, w_ref, lru_w, attn_w):
    xn = _rms(x_ref[...], gain_ref[...]).astype(BF16)
    proj = jnp.dot(xn, w_ref[...], preferred_element_type=F32)
    return _split_columns(proj, lru_w, attn_w)


def _inproj_prompt_kernel(x_ref, gain_ref, w_ref, u_ref, gate_ref, q_ref, k_ref, vt_ref, ktail_ref,
                          vtail_ref, *, lru_w, attn_w, steps_per_seq, tile_rows):
    n_sub = x_ref.shape[0] // tile_rows
    blocks = tile_rows // KEY_BLOCK
    plane = gate_ref.shape[2]
    for t in range(n_sub):
        rows = slice(t * tile_rows, (t + 1) * tile_rows)
        xn = _rms(x_ref[rows, :], gain_ref[...]).astype(BF16)
        proj = jnp.dot(xn, w_ref[...], preferred_element_type=F32)
        v, k, q, gate, u = _split_columns(proj, lru_w, attn_w)
        u_ref[rows, :] = u
        for hf in range(gate_ref.shape[0]):
            gate_ref[hf, rows, :] = gate[:, hf * plane:(hf + 1) * plane]
        q_ref[rows, :] = q.astype(BF16)
        k_ref[rows, :] = k.astype(BF16)
        v_t = v.T
        for blk in range(blocks):
            vt_ref[t * blocks + blk] = v_t[:, blk * KEY_BLOCK:(blk + 1) * KEY_BLOCK].astype(BF16)

    @pl.when(pl.program_id(0) % steps_per_seq == steps_per_seq - 1)
    def _():
        ktail_ref[0] = k.T
        vtail_ref[0] = v_t


def _inproj_sample_kernel(x_ref, gain_ref, w_ref, u_ref, gate_ref, q_ref, k_ref, v_ref, kt_ref, vt_ref,
                          *, lru_w, attn_w):
    v, k, q, gate, u = _project(x_ref, gain_ref, w_ref, lru_w, attn_w)
    u_ref[...] = u
    gate_ref[...] = gate
    q_ref[...] = q.astype(BF16)
    k_ref[...] = k.astype(BF16)
    v_ref[...] = v.astype(BF16)
    tm = k.shape[0]
    n_heads = attn_w // HEAD_DIM
    for h in range(n_heads):
        hs = slice(h * HEAD_DIM, (h + 1) * HEAD_DIM)
        kt_ref[pl.ds(h, tm, stride=n_heads), :] = k[:, hs]
        vt_ref[pl.ds(h, tm, stride=n_heads), :] = v[:, hs]


def _in_projection(x2d, gain, w_bf16, lru_w, attn_w, seq_len, prompt):
    rows, d = x2d.shape
    tm = min(ROW_TILE, rows)
    assert rows % tm == 0
    n_tiles = rows // tm
    n_cols = w_bf16.shape[1]
    row_spec = lambda w: pl.BlockSpec((tm, w), lambda i: (i, 0))
    in_specs = [row_spec(d), _const_spec((1, d)), _const_spec((d, n_cols))]
    common_specs = [row_spec(lru_w), row_spec(lru_w), row_spec(attn_w), row_spec(attn_w)]
    common_shapes = [jax.ShapeDtypeStruct((rows, lru_w), F32), jax.ShapeDtypeStruct((rows, lru_w), F32),
                     jax.ShapeDtypeStruct((rows, attn_w), BF16), jax.ShapeDtypeStruct((rows, attn_w), BF16)]
    if prompt:
        assert seq_len % tm == 0 and tm % KEY_BLOCK == 0 and tm == min(PAD_ROWS, seq_len)
        tile_rows = tm
        tm = tile_rows * math.gcd(seq_len // tile_rows, INPROJ_TILES_PER_STEP)
        n_tiles = rows // tm
        row_spec = lambda w: pl.BlockSpec((tm, w), lambda i: (i, 0))
        in_specs[0] = row_spec(d)
        common_specs = [row_spec(lru_w), None, row_spec(attn_w), row_spec(attn_w)]
        planes = lru_w // MXU_DIM
        common_specs[1] = pl.BlockSpec((planes, tm, MXU_DIM), lambda i: (0, i, 0))
        common_shapes[1] = jax.ShapeDtypeStruct((planes, rows, MXU_DIM), F32)
        steps_per_seq = seq_len // tm
        blocks = tm // KEY_BLOCK
        tail_spec = pl.BlockSpec((1, attn_w, tile_rows), lambda i: (i // steps_per_seq, 0, 0))
        tail_shape = jax.ShapeDtypeStruct((rows // seq_len, attn_w, tile_rows), F32)
        body = functools.partial(_inproj_prompt_kernel, lru_w=lru_w, attn_w=attn_w, steps_per_seq=steps_per_seq,
                                 tile_rows=tile_rows)
        out_specs = common_specs + [pl.BlockSpec((blocks, attn_w, KEY_BLOCK), lambda i: (i, 0, 0)),
                                    tail_spec, tail_spec]
        out_shape = common_shapes + [jax.ShapeDtypeStruct((rows // KEY_BLOCK, attn_w, KEY_BLOCK), BF16),
                                     tail_shape, tail_shape]
    else:
        n_heads = attn_w // HEAD_DIM
        tail_spec = pl.BlockSpec((tm * n_heads, HEAD_DIM), lambda i: (i, 0))
        tail_shape = jax.ShapeDtypeStruct((rows * n_heads, HEAD_DIM), F32)
        body = functools.partial(_inproj_sample_kernel, lru_w=lru_w, attn_w=attn_w)
        out_specs = common_specs + [row_spec(attn_w), tail_spec, tail_spec]
        out_shape = common_shapes + [jax.ShapeDtypeStruct((rows, attn_w), BF16), tail_shape, tail_shape]
    return pl.pallas_call(
        body, grid=(n_tiles,), in_specs=in_specs, out_specs=out_specs, out_shape=out_shape,
        compiler_params=_params(("arbitrary",)), name="in_projection",
    )(x2d, gain, w_bf16)


def _log_sigmoid(x):
    return jnp.minimum(x, 0.0) - jnp.log1p(jnp.exp(-jnp.abs(x)))


def _conv_rows(ext_s, r0, n, w, bias):
    hist = CONV_WIDTH - 1
    uc = bias + ext_s[r0:r0 + n, :] * w[hist:hist + 1]
    for kk in range(hist):
        uc = uc + ext_s[r0 - hist + kk:r0 - hist + kk + n, :] * w[kk:kk + 1]
    return uc


def _gate_terms(half_pre_a, half_pre_x, half_uc, half_ba, half_bx, half_c_log_sig_lam):
    t_r = jnp.tanh(half_pre_a + half_ba)
    t_i = jnp.tanh(half_pre_x + half_bx)
    log_a = half_c_log_sig_lam * t_r + half_c_log_sig_lam
    a = jnp.exp(log_a)
    s = -jnp.tanh(log_a)
    y = (2.0 * s) / (1.0 + s)
    gain = jnp.where(y > 0.0, y * lax.rsqrt(y), 0.0)
    return a, gain * (half_uc * (t_i + 1.0))


def _scan_masks(width):
    row = lax.broadcasted_iota(jnp.int32, (SUBLANES, width), 0)
    return [row >= (1 << i) for i in range(SUBLANES.bit_length() - 1)]


def _scan_group(a, b, h_prev, masks):
    for i, keep in enumerate(masks):
        a_sh = jnp.where(keep, pltpu.roll(a, 1 << i, 0), 1.0)
        b_sh = jnp.where(keep, pltpu.roll(b, 1 << i, 0), 0.0)
        b = a * b_sh + b
        a = a * a_sh
    return a * h_prev + b


def _lru_kernel(u_ref, gate_ref, conv0_ref, h0_ref, cw_ref, cb_ref, wbd_ref, ba_ref, bx_ref, lam_ref,
                y_ref, convn_ref, hlast_ref, ext_s, a_s, b_s, h_s, hc_s, *, tc):
    c = pl.program_id(1)
    hist = CONV_WIDTH - 1
    base = SUBLANES
    half_c_lsl = (0.5 * LRU_C) * _log_sigmoid(lam_ref[...])
    half = wbd_ref.shape[1]
    masks = _scan_masks(a_s.shape[2])
    n_seqs = u_ref.shape[0]

    @pl.when(c == 0)
    def _():
        for i in range(n_seqs):
            ext_s[i, base - hist:base, :] = conv0_ref[i]
            hc_s[i] = h0_ref[i]

    for i in range(n_seqs):
        ext_s[i, base:base + tc, :] = u_ref[i]
        uc = _conv_rows(ext_s.at[i], base, tc, cw_ref[...], cb_ref[...])
        tail = ext_s[i, base + tc - hist:base + tc, :]
        ext_s[i, base - hist:base, :] = tail
        convn_ref[i] = tail

        ucb = uc.astype(BF16)
        for hf in range(wbd_ref.shape[0]):
            cols = slice(hf * half, (hf + 1) * half)
            pre = jnp.dot(ucb[:, cols], wbd_ref[hf], preferred_element_type=F32)
            a, b = _gate_terms(pre[:, :half], pre[:, half:], uc[:, cols], ba_ref[:, cols], bx_ref[:, cols],
                               half_c_lsl[:, cols])
            a_s[i, :, cols] = a
            b_s[i, :, cols] = b

        def group(gi, h_prev):
            r0 = pl.multiple_of(gi * SUBLANES, SUBLANES)
            h = _scan_group(a_s[i, pl.ds(r0, SUBLANES), :], b_s[i, pl.ds(r0, SUBLANES), :], h_prev, masks)
            h_s[i, pl.ds(r0, SUBLANES), :] = h
            return h[SUBLANES - 1:SUBLANES, :]

        h_last = lax.fori_loop(0, tc // SUBLANES, group, hc_s[i], unroll=tc // SUBLANES <= LRU_UNROLL_GROUPS)
        hc_s[i] = h_last
        hlast_ref[i] = h_last
        y_ref[i] = _gelu_tanh_from_half(gate_ref[i]) * h_s[i]


def _lru_group(u, gate, conv0, h0, conv_w, conv_b, wbd, ba, bx, lam):
    b, t, w = u.shape
    tc = min(LRU_TIME_TILE, t)
    assert t % tc == 0 and tc % SUBLANES == 0 and tc >= CONV_WIDTH - 1
    hist = CONV_WIDTH - 1
    per_step = math.gcd(b, max(1, LRU_ROWS_PER_STEP // tc))
    seq_spec = pl.BlockSpec((per_step, tc, w), lambda i, c: (i, c, 0))
    per_batch = lambda r: pl.BlockSpec((per_step, r, w), lambda i, c: (i, 0, 0))
    return pl.pallas_call(
        functools.partial(_lru_kernel, tc=tc),
        grid=(b // per_step, t // tc),
        in_specs=[seq_spec, seq_spec, per_batch(hist), per_batch(1),
                  _const_spec(conv_w.shape), _const_spec((1, w)), _const_spec(wbd.shape),
                  _const_spec((1, w)), _const_spec((1, w)), _const_spec((1, w))],
        out_specs=[seq_spec, per_batch(hist), per_batch(1)],
        out_shape=[jax.ShapeDtypeStruct((b, t, w), F32), jax.ShapeDtypeStruct((b, hist, w), F32),
                   jax.ShapeDtypeStruct((b, 1, w), F32)],
        scratch_shapes=[pltpu.VMEM((per_step, SUBLANES + tc, w), F32), pltpu.VMEM((per_step, tc, w), F32),
                        pltpu.VMEM((per_step, tc, w), F32), pltpu.VMEM((per_step, tc, w), F32),
                        pltpu.VMEM((per_step, 1, w), F32)],
        compiler_params=_params(("arbitrary", "arbitrary")),
        name="rg_lru",
    )(u, gate, conv0, h0.reshape(b, 1, w), conv_w, conv_b, wbd, ba, bx, lam)


def _gap_kind(gap):
    if gap < 0 or gap > LEFT_CHUNKS:
        return "masked"
    return "near" if gap < NEAR_GAPS else "far"


def _attn_prompt_kernel(q_ref, k_ref, vt_ref, bt_ref, o_ref, s_scr, p_scr, ot_scr, *, n_heads, n_blocks):
    lane = lax.broadcasted_iota(jnp.int32, (1, LANES), 1)
    head_lanes = (lane < HEAD_DIM, lane >= HEAD_DIM)
    q_chunks_per_tile = LANES // CHUNK
    zero_tile = jnp.zeros((CHUNK, LANES), BF16)

    def block(lead, q_start, k_start, vt_start):
        k_rows = lead * CHUNK + Q_BLOCK
        q_rows = pl.ds(q_start, Q_BLOCK)
        n_lt = Q_BLOCK // LANES
        plan = []
        for kc in range(k_rows // CHUNK):
            gaps = [lt * q_chunks_per_tile + lead - kc for lt in range(n_lt)]
            plan.append([None if _gap_kind(g) == "masked" and _gap_kind(g + 1) == "masked" else g
                         for g in gaps])

        def scores(h):
            pair = slice((h // 2) * LANES, (h // 2 + 1) * LANES)
            qm = jnp.where(head_lanes[h % 2], q_ref[0, q_rows, pair], jnp.zeros((), BF16))
            for r0 in range(0, k_rows, k_rows // 2):
                kw = k_ref[0, pl.ds(k_start + r0, k_rows // 2), pair]
                s_scr[h % 2, r0:r0 + k_rows // 2, :] = lax.dot_general(kw, qm, NT_DIMS,
                                                                       preferred_element_type=F32)

        def softmax(h):
            slot = h % 2

            def tiles(kc):
                t = s_scr[slot, kc * CHUNK:(kc + 1) * CHUNK, :]
                out = []
                for lt, gap in enumerate(plan[kc]):
                    x = None
                    if gap is not None:
                        x = t[:, lt * LANES:(lt + 1) * LANES]
                        if gap in TILE_GAPS:
                            x = x + bt_ref[h, TILE_GAPS.index(gap)]
                    out.append(x)
                return out

            m = [None] * n_lt
            for kc in range(len(plan)):
                for lt, x in enumerate(tiles(kc)):
                    if x is not None:
                        m[lt] = x if m[lt] is None else jnp.maximum(m[lt], x)
            m = [jnp.max(x, axis=0, keepdims=True) for x in m]
            l = [None] * n_lt
            for kc in range(len(plan)):
                ps = []
                for lt, x in enumerate(tiles(kc)):
                    if x is None:
                        ps.append(zero_tile)
                    else:
                        p = jnp.exp(x - m[lt])
                        l[lt] = p if l[lt] is None else l[lt] + p
                        ps.append(p.astype(BF16))
                p_scr[slot, kc * CHUNK:(kc + 1) * CHUNK, :] = jnp.concatenate(ps, axis=1)
            return jnp.concatenate([1.0 / jnp.sum(x, axis=0, keepdims=True) for x in l], axis=1)

        def weighted_values(h, inv_l):
            rows = slice(h * HEAD_DIM, (h + 1) * HEAD_DIM)
            acc = None
            for blk in range(k_rows // KEY_BLOCK):
                part = jnp.dot(vt_ref[vt_start + blk, rows, :],
                               p_scr[h % 2, blk * KEY_BLOCK:(blk + 1) * KEY_BLOCK, :],
                               preferred_element_type=F32)
                acc = part if acc is None else acc + part
            ot_scr[rows, :] = acc * inv_l

        scores(0)
        inv_prev = None
        for h in range(n_heads):
            if h + 1 < n_heads:
                scores(h + 1)
            if h > 0:
                weighted_values(h - 1, inv_prev)
            inv_prev = softmax(h)
        weighted_values(n_heads - 1, inv_prev)
        o_ref[0, q_rows, :] = ot_scr[...].T

    ramp = PAD_ROWS // Q_BLOCK
    for jj in range(min(ramp, n_blocks)):
        block(jj * (Q_BLOCK // CHUNK), jj * Q_BLOCK, 0, 0)

    per_trip = next(n for n in (6, 3, 2, 1) if (n_blocks - ramp) % n == 0)

    def full_blocks(i, carry):
        for jj in range(per_trip):
            j = ramp + i * per_trip + jj
            block(LEFT_CHUNKS, pl.multiple_of(j * Q_BLOCK, Q_BLOCK),
                  pl.multiple_of((j - ramp) * Q_BLOCK, Q_BLOCK), j - ramp)
        return carry

    lax.fori_loop(0, max(n_blocks - ramp, 0) // per_trip, full_blocks, 0)


def _rel_bias_matrix(table, rows, cols, dist00):
    n = rows + cols - 1
    dist = np.arange(n) - (cols - 1) + dist00
    g = table[:, np.clip(dist, -MAX_REL, MAX_REL) + MAX_REL].astype(F32)
    period = rows + cols
    g = jnp.pad(g, ((0, 0), (0, period - n)))
    flat = jnp.tile(g, (1, rows + 1))[:, :rows * (period + 1)]
    hankel = flat.reshape(-1, rows, period + 1)[:, :, :cols]
    return hankel[:, :, ::-1]


def _prompt_bias_tiles(table):
    far = table[:, 2 * MAX_REL].astype(F32)[:, None, None]

    def chunk_pair(gap):
        kind = _gap_kind(gap)
        if kind == "near":
            return jnp.swapaxes(_rel_bias_matrix(table, CHUNK, CHUNK, gap * CHUNK), 1, 2) - far
        fill = NEG_BIG if kind == "masked" else 0.0
        return jnp.full((table.shape[0], CHUNK, CHUNK), fill, F32)

    tiles = [jnp.concatenate([chunk_pair(g), chunk_pair(g + 1)], axis=2) for g in TILE_GAPS]
    return jnp.stack(tiles, axis=1)


def _attend_prompt(q, k, v_t, table):
    b, t, w = q.shape
    n_heads = w // HEAD_DIM
    assert t % Q_BLOCK == 0 and PAD_ROWS % Q_BLOCK == 0 and LANES // CHUNK == 2 and Q_BLOCK % LANES == 0
    n_blocks = t // Q_BLOCK
    k_rows = PAD_ROWS + Q_BLOCK
    tiles = _prompt_bias_tiles(table)
    seq = pl.BlockSpec((1, t, w), lambda i: (i, 0, 0))
    return pl.pallas_call(
        functools.partial(_attn_prompt_kernel, n_heads=n_heads, n_blocks=n_blocks),
        grid=(b,),
        in_specs=[seq, seq, pl.BlockSpec((n_blocks, w, KEY_BLOCK), lambda i: (i, 0, 0)),
                  _const_spec(tiles.shape)],
        out_specs=seq,
        out_shape=jax.ShapeDtypeStruct((b, t, w), F32),
        scratch_shapes=[pltpu.VMEM((2, k_rows, Q_BLOCK), F32), pltpu.VMEM((2, k_rows, Q_BLOCK), BF16),
                        pltpu.VMEM((w, Q_BLOCK), F32)],
        compiler_params=_params(("arbitrary",)),
        name="attn_prompt",
    )(q, k, v_t, tiles)


def _attn_sample_kernel(q_ref, k_ref, v_ref, ckt_ref, cvt_ref, bc_ref, bn_ref, o_ref, *, n_heads):
    for i in range(q_ref.shape[0]):
        for h in range(n_heads):
            hs = slice(h * HEAD_DIM, (h + 1) * HEAD_DIM)
            qh = q_ref[i, :, hs]
            s_c = jnp.dot(qh, ckt_ref[i, hs, :].astype(BF16), preferred_element_type=F32) + bc_ref[h]
            s_n = lax.dot_general(qh, k_ref[i, :, hs], NT_DIMS, preferred_element_type=F32) + bn_ref[h]
            m = jnp.maximum(jnp.max(s_c, axis=-1, keepdims=True), jnp.max(s_n, axis=-1, keepdims=True))
            p_c = jnp.exp(s_c - m)
            p_n = jnp.exp(s_n - m)
            l = jnp.sum(p_c, axis=-1, keepdims=True) + jnp.sum(p_n, axis=-1, keepdims=True)
            o = lax.dot_general(p_c.astype(BF16), cvt_ref[i, hs, :].astype(BF16), NT_DIMS,
                                preferred_element_type=F32)
            o = o + jnp.dot(p_n.astype(BF16), v_ref[i, :, hs], preferred_element_type=F32)
            o_ref[i, :, hs] = o / l


def _attend_sample(q, k, v, k_cache, v_cache, table):
    b, t, w = q.shape
    n_heads = w // HEAD_DIM
    r = k_cache.shape[1]
    bias_c = _rel_bias_matrix(table, t, r, r)
    bias_n = _rel_bias_matrix(table, t, t, 0)
    channel_major = lambda c: jnp.transpose(c, (0, 2, 3, 1)).reshape(b, w, r)
    per_step = math.gcd(b, SAMPLE_SEQS_PER_STEP)
    new = pl.BlockSpec((per_step, t, w), lambda i: (i, 0, 0))
    old = pl.BlockSpec((per_step, w, r), lambda i: (i, 0, 0))
    return pl.pallas_call(
        functools.partial(_attn_sample_kernel, n_heads=n_heads),
        grid=(b // per_step,),
        in_specs=[new, new, new, old, old, _const_spec(bias_c.shape), _const_spec(bias_n.shape)],
        out_specs=new,
        out_shape=jax.ShapeDtypeStruct((b, t, w), F32),
        compiler_params=_params(("arbitrary",)),
        name="attn_sample",
    )(q, k, v, channel_major(k_cache), channel_major(v_cache), bias_c, bias_n)


def _out_projection(y_lru_planes, x, ya, gl_ref, ga_ref, wo_ref, gf_ref, token=None):
    lru_w = sum(y.shape[1] for y in y_lru_planes)
    attn_w = ya.shape[1]
    gl, ga = gl_ref[...], ga_ref[...]
    if token is not None:
        gl = gl + jnp.tile(token, (1, lru_w // LANES))
        ga = ga + jnp.tile(token, (1, attn_w // LANES))
    sum_sq = sum(jnp.sum(y * y, axis=-1, keepdims=True) for y in y_lru_planes)
    inv_rms = lax.rsqrt(sum_sq * (1.0 / lru_w) + EPS)
    mix = jnp.dot(_rms(ya, ga).astype(BF16), wo_ref[lru_w:, :], preferred_element_type=F32)
    c0 = 0
    for y in y_lru_planes:
        cols = slice(c0, c0 + y.shape[1])
        mix = mix + jnp.dot((y * inv_rms * gl[:, cols]).astype(BF16), wo_ref[cols, :],
                            preferred_element_type=F32)
        c0 += y.shape[1]
    x1 = x + mix
    return x1, _rms(x1, gf_ref[...]).astype(BF16)


def _ffn_column_pipeline(xn_s, acc_s, buf, hmid_s, wg_ref, wu_ref, wd_ref, side_work=()):
    n_tiles = wg_ref.shape[1] // FFN_TILE

    def gate_up(c):
        cols = slice(c * FFN_TILE, (c + 1) * FFN_TILE)
        xn = xn_s[buf]
        return (jnp.dot(xn, wg_ref[:, cols], preferred_element_type=F32),
                jnp.dot(xn, wu_ref[:, cols], preferred_element_type=F32))

    side_work = list(side_work)
    pending = gate_up(0)
    for c in range(n_tiles):
        upcoming = gate_up(c + 1) if c + 1 < n_tiles else None
        g, up = pending
        hmid = _silu_from_half(g) * up
        hmid_s[:, c * FFN_TILE:(c + 1) * FFN_TILE] = hmid.astype(BF16)
        if (c + 1) % DOWN_GROUP == 0 or c + 1 == n_tiles:
            ks = slice((c // DOWN_GROUP) * DOWN_GROUP * FFN_TILE, (c + 1) * FFN_TILE)
            acc_s[buf] = acc_s[buf] + jnp.dot(hmid_s[:, ks], wd_ref[ks, :], preferred_element_type=F32)
        token = _zero_token(hmid[-SUBLANES:, -LANES:])
        for _ in range(-(-len(side_work) // (n_tiles - c))):
            side_work.pop(0)(token)
        pending = upcoming
    assert not side_work


def _out_ffn_kernel(x_ref, yl_ref, ya_ref, gl_ref, ga_ref, wo_ref, gf_ref, wg_ref, wu_ref, wd_ref,
                    gfin_ref, o_ref, xn_s, hmid_s, acc_s, *, final_norm):
    x1, xn = _out_projection([yl_ref[...]], x_ref[...], ya_ref[...], gl_ref, ga_ref, wo_ref, gf_ref)
    acc_s[0] = x1
    xn_s[0] = xn
    _ffn_column_pipeline(xn_s, acc_s, 0, hmid_s, wg_ref, wu_ref, wd_ref)
    o_ref[...] = _rms(acc_s[0], gfin_ref[...]) if final_norm else acc_s[0]


def _ffn_scratch(tm, d, d_ff, bufs):
    assert d_ff % FFN_TILE == 0
    return [pltpu.VMEM((bufs, tm, d), BF16), pltpu.VMEM((tm, d_ff), BF16), pltpu.VMEM((bufs, tm, d), F32)]


def _out_ffn(x2d, y_lru, y_att, gl, ga, wo, gf, wg, wu, wd, gfin, final_norm):
    rows, d = x2d.shape
    tm = min(ROW_TILE, rows)
    assert rows % tm == 0
    lru_w, attn_w, d_ff = y_lru.shape[1], y_att.shape[1], wg.shape[1]
    row_spec = lambda w: pl.BlockSpec((tm, w), lambda i: (i, 0))
    return pl.pallas_call(
        functools.partial(_out_ffn_kernel, final_norm=final_norm),
        grid=(rows // tm,),
        in_specs=[row_spec(d), row_spec(lru_w), row_spec(attn_w), _const_spec((1, lru_w)),
                  _const_spec((1, attn_w)), _const_spec(wo.shape), _const_spec((1, d)),
                  _const_spec(wg.shape), _const_spec(wu.shape), _const_spec(wd.shape), _const_spec((1, d))],
        out_specs=row_spec(d),
        out_shape=jax.ShapeDtypeStruct((rows, d), F32),
        scratch_shapes=_ffn_scratch(tm, d, d_ff, 1),
        compiler_params=_params(("arbitrary",)),
        name="out_ffn",
    )(x2d, y_lru, y_att, gl, ga, wo, gf, wg, wu, wd, gfin)


def _lru_out_ffn_kernel(x_ref, ya_ref, u_ref, gate_ref, cw_ref, cb_ref, wbd_ref, ba_ref, bx_ref, lam_ref,
                        gl_ref, ga_ref, wo_ref, gf_ref, wg_ref, wu_ref, wd_ref, gfin_ref,
                        o_ref, convn_ref, hlast_ref,
                        ext_s, uc_s, ucb_s, pre_s, y_s, hc_s, xn_s, hmid_s, acc_s,
                        *, tiles_per_seq, n_row_tiles, final_norm):
    s = pl.program_id(0)
    tm, lru_w = u_ref.shape
    hist = CONV_WIDTH - 1
    base = SUBLANES
    n_halves, half = wbd_ref.shape[0], wbd_ref.shape[1]
    slot = s % 2

    @pl.when(s == 0)
    def _():
        y_s[1] = jnp.zeros((n_halves, tm, half), F32)
        ext_s[base - hist:base, :] = jnp.zeros((hist, lru_w), F32)
        hc_s[...] = jnp.zeros((1, lru_w), F32)

    seq_start = (s % tiles_per_seq) == 0
    ext_s[base - hist:base, :] = jnp.where(seq_start, 0.0, ext_s[base - hist:base, :])
    ext_s[base:base + tm, :] = u_ref[...]
    h_start = jnp.where(seq_start, 0.0, hc_s[...])
    carry = [h_start[:, hf * half:(hf + 1) * half] for hf in range(n_halves)]
    half_c_lsl = (0.5 * LRU_C) * _log_sigmoid(lam_ref[...])
    masks = _scan_masks(half)

    def conv(pb, token):
        rows = slice(pb * LRU_PIECE_ROWS, (pb + 1) * LRU_PIECE_ROWS)
        bias = cb_ref[...] + jnp.tile(token, (1, lru_w // LANES))
        uc = _conv_rows(ext_s, base + pb * LRU_PIECE_ROWS, LRU_PIECE_ROWS, cw_ref[...], bias)
        for hf in range(n_halves):
            plane = uc[:, hf * half:(hf + 1) * half]
            uc_s[hf, rows, :] = plane
            ucb_s[hf, rows, :] = plane.astype(BF16)

    def gate_matmul(gb, token):
        del token
        rows = slice(gb * LRU_GATE_ROWS, (gb + 1) * LRU_GATE_ROWS)
        for hf in range(n_halves):
            pre = jnp.dot(ucb_s[hf, rows, :], wbd_ref[hf], preferred_element_type=F32)
            pre_s[2 * hf, rows, :] = pre[:, :half]
            pre_s[2 * hf + 1, rows, :] = pre[:, half:]

    def recur(pb, hf, token):
        rows = slice(pb * LRU_PIECE_ROWS, (pb + 1) * LRU_PIECE_ROWS)
        cols = slice(hf * half, (hf + 1) * half)
        zero = jnp.tile(token, (1, half // LANES))
        a, b = _gate_terms(pre_s[2 * hf, rows, :], pre_s[2 * hf + 1, rows, :],
                           uc_s[hf, rows, :], ba_ref[:, cols] + zero, bx_ref[:, cols] + zero,
                           half_c_lsl[:, cols])
        h_prev = carry[hf]
        hs = []
        for g in range(LRU_PIECE_ROWS // SUBLANES):
            grp = slice(g * SUBLANES, (g + 1) * SUBLANES)
            h = _scan_group(a[grp], b[grp], h_prev, masks)
            h_prev = h[SUBLANES - 1:SUBLANES, :]
            hs.append(h)
        carry[hf] = h_prev
        y_s[slot, hf, rows, :] = _gelu_tanh_from_half(gate_ref[hf, rows, :]) * jnp.concatenate(hs, axis=0)

    per = LRU_GATE_ROWS // LRU_PIECE_ROWS
    n_gate_blocks = tm // LRU_GATE_ROWS
    side_work = [functools.partial(conv, pb) for pb in range(per)]
    for gb in range(n_gate_blocks):
        side_work.append(functools.partial(gate_matmul, gb))
        for i in range(per):
            if gb + 1 < n_gate_blocks:
                side_work.append(functools.partial(conv, (gb + 1) * per + i))
            side_work += [functools.partial(recur, gb * per + i, hf) for hf in range(n_halves)]

    x1, xn = _out_projection([y_s[1 - slot, hf] for hf in range(n_halves)], x_ref[...], ya_ref[...],
                             gl_ref, ga_ref, wo_ref, gf_ref)
    acc_s[0] = x1
    xn_s[0] = xn
    _ffn_column_pipeline(xn_s, acc_s, 0, hmid_s, wg_ref, wu_ref, wd_ref, side_work)
    o_ref[...] = _rms(acc_s[0], gfin_ref[...]) if final_norm else acc_s[0]

    tail = ext_s[base + tm - hist:base + tm, :]
    ext_s[base - hist:base, :] = tail
    h_end = jnp.concatenate(carry, axis=1)
    hc_s[...] = h_end

    @pl.when(s < n_row_tiles)
    def _():
        convn_ref[0] = tail
        hlast_ref[0] = h_end


def _lru_out_ffn(x2d, y_att, u, gate, seq_len, p, gfin, final_norm):
    rows, d = x2d.shape
    tm = ROW_TILE
    lru_w, attn_w = u.shape[1], y_att.shape[1]
    d_ff = p["w_gate"].shape[1]
    n_halves, half = p["wbd"].shape[0], p["wbd"].shape[1]
    assert rows % tm == 0 and seq_len % tm == 0 and gate.shape == (n_halves, rows, half)
    assert tm % LRU_GATE_ROWS == 0 and LRU_GATE_ROWS % LRU_PIECE_ROWS == 0
    n = rows // tm
    tiles_per_seq = seq_len // tm
    hist = CONV_WIDTH - 1
    tile = lambda s, lag: jnp.clip(s - lag, 0, n - 1)
    rows_of = lambda w, lag: pl.BlockSpec((tm, w), lambda s: (tile(s, lag), 0))
    per_seq = lambda r: pl.BlockSpec((1, r, lru_w), lambda s: (tile(s, 0) // tiles_per_seq, 0, 0))
    consts = [p["conv_w"], p["conv_b"], p["wbd"], p["lru_ba"], p["lru_bx"], p["lru_lambda"],
              p["norm_lru_out"], p["norm_attn_out"], p["w_out"], p["norm_ffn"], p["w_gate"], p["w_up"],
              p["w_down"], gfin]
    return pl.pallas_call(
        functools.partial(_lru_out_ffn_kernel, tiles_per_seq=tiles_per_seq, n_row_tiles=n, final_norm=final_norm),
        grid=(n + 1,),
        in_specs=[rows_of(d, 1), rows_of(attn_w, 1), rows_of(lru_w, 0),
                  pl.BlockSpec((n_halves, tm, half), lambda s: (0, tile(s, 0), 0))]
                 + [_const_spec(c.shape) for c in consts],
        out_specs=[rows_of(d, 1), per_seq(hist), per_seq(1)],
        out_shape=[jax.ShapeDtypeStruct((rows, d), F32),
                   jax.ShapeDtypeStruct((rows // seq_len, hist, lru_w), F32),
                   jax.ShapeDtypeStruct((rows // seq_len, 1, lru_w), F32)],
        scratch_shapes=[pltpu.VMEM((SUBLANES + tm, lru_w), F32), pltpu.VMEM((n_halves, tm, half), F32),
                        pltpu.VMEM((n_halves, tm, half), BF16), pltpu.VMEM((2 * n_halves, tm, half), F32),
                        pltpu.VMEM((2, n_halves, tm, half), F32), pltpu.VMEM((1, lru_w), F32)]
                       + _ffn_scratch(tm, d, d_ff, 1),
        compiler_params=_params(("arbitrary",)),
        name="lru_out_ffn",
    )(x2d, y_att, u, gate, *consts)


def _block_diag_gates(wa, wx):
    n, c, _ = wa.shape
    per = MXU_DIM // c
    assert n % per == 0
    eye = jnp.eye(per, dtype=wa.dtype)

    def bd(w):
        w = w.reshape(n // per, per, c, c)
        return (eye[None, :, None, :, None] * w[:, :, :, None, :]).reshape(n // per, per * c, per * c)

    return jnp.concatenate([bd(wa), bd(wx)], axis=-1).astype(BF16)


def _layer(x, conv0, h0, prompt, attend, final_gain, p):
    b, t, d = x.shape
    lru_w = p["conv_w"].shape[1]
    attn_w = p["norm_attn_out"].shape[1]
    n_heads = attn_w // HEAD_DIM
    rows = b * t
    x2d = x.reshape(rows, d)
    u, gate, q, k, v, k_tail, v_tail = _in_projection(x2d, p["norm_mix"], p["w_in"], lru_w, attn_w, t, prompt)
    y_att = attend(q.reshape(b, t, attn_w), k.reshape(b, t, attn_w), v if prompt else v.reshape(b, t, attn_w))
    y_att = y_att.reshape(rows, attn_w)
    final_norm = final_gain is not None
    gfin = final_gain if final_norm else p["norm_ffn"]
    if prompt:
        y, conv_new, h_last = _lru_out_ffn(x2d, y_att, u, gate, t, p, gfin, final_norm)
    else:
        y_lru, conv_new, h_last = _lru_group(u.reshape(b, t, lru_w), gate.reshape(b, t, lru_w), conv0, h0,
                                             p["conv_w"], p["conv_b"], p["wbd"], p["lru_ba"], p["lru_bx"],
                                             p["lru_lambda"])
        y = _out_ffn(x2d, y_lru.reshape(rows, lru_w), y_att, p["norm_lru_out"], p["norm_attn_out"],
                     p["w_out"], p["norm_ffn"], p["w_gate"], p["w_up"], p["w_down"], gfin, final_norm)
    if prompt:
        keep = k_tail.shape[2]
        k_tail = k_tail.reshape(b, n_heads, HEAD_DIM, keep).transpose(0, 3, 1, 2)
        v_tail = v_tail.reshape(b, n_heads, HEAD_DIM, keep).transpose(0, 3, 1, 2)
    else:
        k_tail = k_tail.reshape(b, t, n_heads, HEAD_DIM)
        v_tail = v_tail.reshape(b, t, n_heads, HEAD_DIM)
    return y.reshape(b, t, d), conv_new, h_last.reshape(b, lru_w), k_tail, v_tail


def kernel(x_prompt, x_sample, state_conv, state_lru, cache_k, cache_v, norm_mix, w_in, conv_w, conv_b, lru_wa, lru_ba, lru_wx, lru_bx, lru_lambda, rel_bias, norm_lru_out, norm_attn_out, w_out, norm_ffn, w_gate, w_up, w_down, norm_final):
    depth = w_in.shape[0]
    xp, xs = x_prompt, x_sample
    row = lambda a: a.reshape(1, -1)
    outs = [[] for _ in range(8)]
    for l in range(depth):
        lru_w, attn_w = conv_w.shape[-1], norm_attn_out.shape[-1]
        p = dict(
            norm_mix=row(norm_mix[l]), w_in=_in_weights(w_in[l], lru_w, attn_w), conv_w=0.5 * conv_w[l],
            conv_b=row(0.5 * conv_b[l]), wbd=_block_diag_gates(lru_wa[l], lru_wx[l]),
            lru_ba=row(0.5 * lru_ba[l]), lru_bx=row(0.5 * lru_bx[l]),
            lru_lambda=row(lru_lambda[l]), norm_lru_out=row(norm_lru_out[l]),
            norm_attn_out=row(norm_attn_out[l]), w_out=w_out[l].astype(BF16), norm_ffn=row(norm_ffn[l]),
            w_gate=(0.5 * w_gate[l]).astype(BF16), w_up=w_up[l].astype(BF16), w_down=w_down[l].astype(BF16))
        final_gain = row(norm_final) if l == depth - 1 else None
        att_p = functools.partial(_attend_prompt, table=rel_bias[l])
        xp, cp, hp, kp, vp = _layer(xp, None, None, True, att_p, final_gain, p)
        att_s = functools.partial(_attend_sample, k_cache=cache_k[l], v_cache=cache_v[l], table=rel_bias[l])
        xs, cs, hs, ksn, vsn = _layer(xs, state_conv[l], state_lru[l], False, att_s, final_gain, p)
        for lst, val in zip(outs, (cp, hp, kp, vp, cs, hs, ksn, vsn)):
            lst.append(val)
    return (xp, xs) + tuple(jnp.stack(o) for o in outs)
```

```python
import functools
import math

import jax
import jax.numpy as jnp
import numpy as np
from jax import lax
from jax.experimental import pallas as pl
from jax.experimental.pallas import tpu as pltpu

F32 = jnp.float32
BF16 = jnp.bfloat16

EPS = 1e-6
CHUNK = 64
LEFT_CHUNKS = 8
PAD_ROWS = LEFT_CHUNKS * CHUNK
MAX_REL = 128
HEAD_DIM = 64
N_LRU_BLOCKS = 8
CONV_WIDTH = 4
LRU_C = 8.0
NEG_BIG = -1e30

ROW_TILE = 512
INPROJ_TILES_PER_STEP = 2
LRU_TIME_TILE = 512
LRU_ROWS_PER_STEP = 128
LRU_UNROLL_GROUPS = 8
Q_BLOCK = 256
KEY_BLOCK = Q_BLOCK
FFN_TILE = 256
DOWN_GROUP = 4
SAMPLE_SEQS_PER_STEP = 8
LRU_GATE_ROWS = 256
LRU_PIECE_ROWS = 64
SUBLANES = 8
MXU_DIM = 256
LANES = 128
VMEM_LIMIT = 56 * 1024 * 1024

NT_DIMS = (((1,), (1,)), ((), ()))

NEAR_GAPS = -(-(MAX_REL + CHUNK - 1) // CHUNK)
TILE_GAPS = (-1,) + tuple(range(NEAR_GAPS)) + (LEFT_CHUNKS,)


def _params(sem):
    return pltpu.CompilerParams(dimension_semantics=sem, vmem_limit_bytes=VMEM_LIMIT)


def _const_spec(shape):
    nd = len(shape)
    return pl.BlockSpec(shape, lambda *_: (0,) * nd, pipeline_mode=pl.Buffered(1))


def _rms(x, gain):
    return x * lax.rsqrt(jnp.mean(x * x, axis=-1, keepdims=True) + EPS) * gain


def _zero_token(v):
    bits = pltpu.bitcast(v[:SUBLANES, :LANES], jnp.uint32)
    zero = lax.shift_right_logical(lax.shift_right_logical(bits, jnp.uint32(31)), jnp.uint32(1))
    return pltpu.bitcast(zero, F32)[0:1, :]


def _silu_from_half(hx):
    return hx * (jnp.tanh(hx) + 1.0)


def _gelu_tanh_from_half(hx):
    c = math.sqrt(2.0 / math.pi)
    return hx * (1.0 + jnp.tanh(hx * (2.0 * c + (8.0 * 0.044715 * c) * (hx * hx))))


def _in_weights(w, lru_w, attn_w):
    u, gate, q, k, v = jnp.split(w, np.cumsum([lru_w, lru_w, attn_w, attn_w]), axis=1)
    return jnp.concatenate([v, k, q * (HEAD_DIM ** -0.5), gate * 0.5, u], axis=1).astype(BF16)


def _split_columns(proj, lru_w, attn_w):
    c0, c1, c2, c3 = attn_w, 2 * attn_w, 3 * attn_w, 3 * attn_w + lru_w
    return proj[:, :c0], proj[:, c0:c1], proj[:, c1:c2], proj[:, c2:c3], proj[:, c3:]


def _project(x_ref, gain_ref, w_ref, lru_w, attn_w):
    xn = _rms(x_ref[...], gain_ref[...]).astype(BF16)
    proj = jnp.dot(xn, w_ref[...], preferred_element_type=F32)
    return _split_columns(proj, lru_w, attn_w)


def _inproj_prompt_kernel(x_ref, gain_ref, w_ref, u_ref, gate_ref, q_ref, k_ref, vt_ref, ktail_ref,
                          vtail_ref, *, lru_w, attn_w, steps_per_seq, tile_rows):
    n_sub = x_ref.shape[0] // tile_rows
    blocks = tile_rows // KEY_BLOCK
    plane = gate_ref.shape[2]
    for t in range(n_sub):
        rows = slice(t * tile_rows, (t + 1) * tile_rows)
        xn = _rms(x_ref[rows, :], gain_ref[...]).astype(BF16)
        proj = jnp.dot(xn, w_ref[...], preferred_element_type=F32)
        v, k, q, gate, u = _split_columns(proj, lru_w, attn_w)
        u_ref[rows, :] = u
        for hf in range(gate_ref.shape[0]):
            gate_ref[hf, rows, :] = gate[:, hf * plane:(hf + 1) * plane]
        q_ref[rows, :] = q.astype(BF16)
        k_ref[rows, :] = k.astype(BF16)
        v_t = v.T
        for blk in range(blocks):
            vt_ref[t * blocks + blk] = v_t[:, blk * KEY_BLOCK:(blk + 1) * KEY_BLOCK].astype(BF16)

    @pl.when(pl.program_id(0) % steps_per_seq == steps_per_seq - 1)
    def _():
        ktail_ref[0] = k.T
        vtail_ref[0] = v_t


def _inproj_sample_kernel(x_ref, gain_ref, w_ref, u_ref, gate_ref, q_ref, k_ref, v_ref, kt_ref, vt_ref,
                          *, lru_w, attn_w):
    v, k, q, gate, u = _project(x_ref, gain_ref, w_ref, lru_w, attn_w)
    u_ref[...] = u
    gate_ref[...] = gate
    q_ref[...] = q.astype(BF16)
    k_ref[...] = k.astype(BF16)
    v_ref[...] = v.astype(BF16)
    tm = k.shape[0]
    n_heads = attn_w // HEAD_DIM
    for h in range(n_heads):
        hs = slice(h * HEAD_DIM, (h + 1) * HEAD_DIM)
        kt_ref[pl.ds(h, tm, stride=n_heads), :] = k[:, hs]
        vt_ref[pl.ds(h, tm, stride=n_heads), :] = v[:, hs]


def _in_projection(x2d, gain, w_bf16, lru_w, attn_w, seq_len, prompt):
    rows, d = x2d.shape
    tm = min(ROW_TILE, rows)
    assert rows % tm == 0
    n_tiles = rows // tm
    n_cols = w_bf16.shape[1]
    row_spec = lambda w: pl.BlockSpec((tm, w), lambda i: (i, 0))
    in_specs = [row_spec(d), _const_spec((1, d)), _const_spec((d, n_cols))]
    common_specs = [row_spec(lru_w), row_spec(lru_w), row_spec(attn_w), row_spec(attn_w)]
    common_shapes = [jax.ShapeDtypeStruct((rows, lru_w), F32), jax.ShapeDtypeStruct((rows, lru_w), F32),
                     jax.ShapeDtypeStruct((rows, attn_w), BF16), jax.ShapeDtypeStruct((rows, attn_w), BF16)]
    if prompt:
        assert seq_len % tm == 0 and tm % KEY_BLOCK == 0 and tm == min(PAD_ROWS, seq_len)
        tile_rows = tm
        tm = tile_rows * math.gcd(seq_len // tile_rows, INPROJ_TILES_PER_STEP)
        n_tiles = rows // tm
        row_spec = lambda w: pl.BlockSpec((tm, w), lambda i: (i, 0))
        in_specs[0] = row_spec(d)
        common_specs = [row_spec(lru_w), None, row_spec(attn_w), row_spec(attn_w)]
        planes = lru_w // MXU_DIM
        common_specs[1] = pl.BlockSpec((planes, tm, MXU_DIM), lambda i: (0, i, 0))
        common_shapes[1] = jax.ShapeDtypeStruct((planes, rows, MXU_DIM), F32)
        steps_per_seq = seq_len // tm
        blocks = tm // KEY_BLOCK
        tail_spec = pl.BlockSpec((1, attn_w, tile_rows), lambda i: (i // steps_per_seq, 0, 0))
        tail_shape = jax.ShapeDtypeStruct((rows // seq_len, attn_w, tile_rows), F32)
        body = functools.partial(_inproj_prompt_kernel, lru_w=lru_w, attn_w=attn_w, steps_per_seq=steps_per_seq,
                                 tile_rows=tile_rows)
        out_specs = common_specs + [pl.BlockSpec((blocks, attn_w, KEY_BLOCK), lambda i: (i, 0, 0)),
                                    tail_spec, tail_spec]
        out_shape = common_shapes + [jax.ShapeDtypeStruct((rows // KEY_BLOCK, attn_w, KEY_BLOCK), BF16),
                                     tail_shape, tail_shape]
    else:
        n_heads = attn_w // HEAD_DIM
        tail_spec = pl.BlockSpec((tm * n_heads, HEAD_DIM), lambda i: (i, 0))
        tail_shape = jax.ShapeDtypeStruct((rows * n_heads, HEAD_DIM), F32)
        body = functools.partial(_inproj_sample_kernel, lru_w=lru_w, attn_w=attn_w)
        out_specs = common_specs + [row_spec(attn_w), tail_spec, tail_spec]
        out_shape = common_shapes + [jax.ShapeDtypeStruct((rows, attn_w), BF16), tail_shape, tail_shape]
    return pl.pallas_call(
        body, grid=(n_tiles,), in_specs=in_specs, out_specs=out_specs, out_shape=out_shape,
        compiler_params=_params(("arbitrary",)), name="in_projection",
    )(x2d, gain, w_bf16)


def _log_sigmoid(x):
    return jnp.minimum(x, 0.0) - jnp.log1p(jnp.exp(-jnp.abs(x)))


def _conv_rows(ext_s, r0, n, w, bias):
    hist = CONV_WIDTH - 1
    uc = bias + ext_s[r0:r0 + n, :] * w[hist:hist + 1]
    for kk in range(hist):
        uc = uc + ext_s[r0 - hist + kk:r0 - hist + kk + n, :] * w[kk:kk + 1]
    return uc


def _gate_terms(half_pre_a, half_pre_x, half_uc, half_ba, half_bx, half_c_log_sig_lam):
    t_r = jnp.tanh(half_pre_a + half_ba)
    t_i = jnp.tanh(half_pre_x + half_bx)
    log_a = half_c_log_sig_lam * t_r + half_c_log_sig_lam
    a = jnp.exp(log_a)
    s = -jnp.tanh(log_a)
    y = (2.0 * s) / (1.0 + s)
    gain = jnp.where(y > 0.0, y * lax.rsqrt(y), 0.0)
    return a, gain * (half_uc * (t_i + 1.0))


def _scan_masks(width):
    row = lax.broadcasted_iota(jnp.int32, (SUBLANES, width), 0)
    return [row >= (1 << i) for i in range(SUBLANES.bit_length() - 1)]


def _scan_group(a, b, h_prev, masks):
    for i, keep in enumerate(masks):
        a_sh = jnp.where(keep, pltpu.roll(a, 1 << i, 0), 1.0)
        b_sh = jnp.where(keep, pltpu.roll(b, 1 << i, 0), 0.0)
        b = a * b_sh + b
        a = a * a_sh
    return a * h_prev + b


def _lru_kernel(u_ref, gate_ref, conv0_ref, h0_ref, cw_ref, cb_ref, wbd_ref, ba_ref, bx_ref, lam_ref,
                y_ref, convn_ref, hlast_ref, ext_s, a_s, b_s, h_s, hc_s, *, tc):
    c = pl.program_id(1)
    hist = CONV_WIDTH - 1
    base = SUBLANES
    half_c_lsl = (0.5 * LRU_C) * _log_sigmoid(lam_ref[...])
    half = wbd_ref.shape[1]
    masks = _scan_masks(a_s.shape[2])
    n_seqs = u_ref.shape[0]

    @pl.when(c == 0)
    def _():
        for i in range(n_seqs):
            ext_s[i, base - hist:base, :] = conv0_ref[i]
            hc_s[i] = h0_ref[i]

    for i in range(n_seqs):
        ext_s[i, base:base + tc, :] = u_ref[i]
        uc = _conv_rows(ext_s.at[i], base, tc, cw_ref[...], cb_ref[...])
        tail = ext_s[i, base + tc - hist:base + tc, :]
        ext_s[i, base - hist:base, :] = tail
        convn_ref[i] = tail

        ucb = uc.astype(BF16)
        for hf in range(wbd_ref.shape[0]):
            cols = slice(hf * half, (hf + 1) * half)
            pre = jnp.dot(ucb[:, cols], wbd_ref[hf], preferred_element_type=F32)
            a, b = _gate_terms(pre[:, :half], pre[:, half:], uc[:, cols], ba_ref[:, cols], bx_ref[:, cols],
                               half_c_lsl[:, cols])
            a_s[i, :, cols] = a
            b_s[i, :, cols] = b

        def group(gi, h_prev):
            r0 = pl.multiple_of(gi * SUBLANES, SUBLANES)
            h = _scan_group(a_s[i, pl.ds(r0, SUBLANES), :], b_s[i, pl.ds(r0, SUBLANES), :], h_prev, masks)
            h_s[i, pl.ds(r0, SUBLANES), :] = h
            return h[SUBLANES - 1:SUBLANES, :]

        h_last = lax.fori_loop(0, tc // SUBLANES, group, hc_s[i], unroll=tc // SUBLANES <= LRU_UNROLL_GROUPS)
        hc_s[i] = h_last
        hlast_ref[i] = h_last
        y_ref[i] = _gelu_tanh_from_half(gate_ref[i]) * h_s[i]


def _lru_group(u, gate, conv0, h0, conv_w, conv_b, wbd, ba, bx, lam):
    b, t, w = u.shape
    tc = min(LRU_TIME_TILE, t)
    assert t % tc == 0 and tc % SUBLANES == 0 and tc >= CONV_WIDTH - 1
    hist = CONV_WIDTH - 1
    per_step = math.gcd(b, max(1, LRU_ROWS_PER_STEP // tc))
    seq_spec = pl.BlockSpec((per_step, tc, w), lambda i, c: (i, c, 0))
    per_batch = lambda r: pl.BlockSpec((per_step, r, w), lambda i, c: (i, 0, 0))
    return pl.pallas_call(
        functools.partial(_lru_kernel, tc=tc),
        grid=(b // per_step, t // tc),
        in_specs=[seq_spec, seq_spec, per_batch(hist), per_batch(1),
                  _const_spec(conv_w.shape), _const_spec((1, w)), _const_spec(wbd.shape),
                  _const_spec((1, w)), _const_spec((1, w)), _const_spec((1, w))],
        out_specs=[seq_spec, per_batch(hist), per_batch(1)],
        out_shape=[jax.ShapeDtypeStruct((b, t, w), F32), jax.ShapeDtypeStruct((b, hist, w), F32),
                   jax.ShapeDtypeStruct((b, 1, w), F32)],
        scratch_shapes=[pltpu.VMEM((per_step, SUBLANES + tc, w), F32), pltpu.VMEM((per_step, tc, w), F32),
                        pltpu.VMEM((per_step, tc, w), F32), pltpu.VMEM((per_step, tc, w), F32),
                        pltpu.VMEM((per_step, 1, w), F32)],
        compiler_params=_params(("arbitrary", "arbitrary")),
        name="rg_lru",
    )(u, gate, conv0, h0.reshape(b, 1, w), conv_w, conv_b, wbd, ba, bx, lam)


def _gap_kind(gap):
    if gap < 0 or gap > LEFT_CHUNKS:
        return "masked"
    return "near" if gap < NEAR_GAPS else "far"


def _attn_prompt_kernel(q_ref, k_ref, vt_ref, bt_ref, o_ref, s_scr, p_scr, ot_scr, *, n_heads, n_blocks):
    lane = lax.broadcasted_iota(jnp.int32, (1, LANES), 1)
    head_lanes = (lane < HEAD_DIM, lane >= HEAD_DIM)
    q_chunks_per_tile = LANES // CHUNK
    zero_tile = jnp.zeros((CHUNK, LANES), BF16)

    def block(lead, q_start, k_start, vt_start):
        k_rows = lead * CHUNK + Q_BLOCK
        q_rows = pl.ds(q_start, Q_BLOCK)
        n_lt = Q_BLOCK // LANES
        plan = []
        for kc in range(k_rows // CHUNK):
            gaps = [lt * q_chunks_per_tile + lead - kc for lt in range(n_lt)]
            plan.append([None if _gap_kind(g) == "masked" and _gap_kind(g + 1) == "masked" else g
                         for g in gaps])

        def scores(h):
            pair = slice((h // 2) * LANES, (h // 2 + 1) * LANES)
            qm = jnp.where(head_lanes[h % 2], q_ref[0, q_rows, pair], jnp.zeros((), BF16))
            for r0 in range(0, k_rows, k_rows // 2):
                kw = k_ref[0, pl.ds(k_start + r0, k_rows // 2), pair]
                s_scr[h % 2, r0:r0 + k_rows // 2, :] = lax.dot_general(kw, qm, NT_DIMS,
                                                                       preferred_element_type=F32)

        def softmax(h):
            slot = h % 2

            def tiles(kc):
                t = s_scr[slot, kc * CHUNK:(kc + 1) * CHUNK, :]
                out = []
                for lt, gap in enumerate(plan[kc]):
                    x = None
                    if gap is not None:
                        x = t[:, lt * LANES:(lt + 1) * LANES]
                        if gap in TILE_GAPS:
                            x = x + bt_ref[h, TILE_GAPS.index(gap)]
                    out.append(x)
                return out

            m = [None] * n_lt
            for kc in range(len(plan)):
                for lt, x in enumerate(tiles(kc)):
                    if x is not None:
                        m[lt] = x if m[lt] is None else jnp.maximum(m[lt], x)
            m = [jnp.max(x, axis=0, keepdims=True) for x in m]
            l = [None] * n_lt
            for kc in range(len(plan)):
                ps = []
                for lt, x in enumerate(tiles(kc)):
                    if x is None:
                        ps.append(zero_tile)
                    else:
                        p = jnp.exp(x - m[lt])
                        l[lt] = p if l[lt] is None else l[lt] + p
                        ps.append(p.astype(BF16))
                p_scr[slot, kc * CHUNK:(kc + 1) * CHUNK, :] = jnp.concatenate(ps, axis=1)
            return jnp.concatenate([1.0 / jnp.sum(x, axis=0, keepdims=True) for x in l], axis=1)

        def weighted_values(h, inv_l):
            rows = slice(h * HEAD_DIM, (h + 1) * HEAD_DIM)
            acc = None
            for blk in range(k_rows // KEY_BLOCK):
                part = jnp.dot(vt_ref[vt_start + blk, rows, :],
                               p_scr[h % 2, blk * KEY_BLOCK:(blk + 1) * KEY_BLOCK, :],
                               preferred_element_type=F32)
                acc = part if acc is None else acc + part
            ot_scr[rows, :] = acc * inv_l

        scores(0)
        inv_prev = None
        for h in range(n_heads):
            if h + 1 < n_heads:
                scores(h + 1)
            if h > 0:
                weighted_values(h - 1, inv_prev)
            inv_prev = softmax(h)
        weighted_values(n_heads - 1, inv_prev)
        o_ref[0, q_rows, :] = ot_scr[...].T

    ramp = PAD_ROWS // Q_BLOCK
    for jj in range(min(ramp, n_blocks)):
        block(jj * (Q_BLOCK // CHUNK), jj * Q_BLOCK, 0, 0)

    per_trip = next(n for n in (6, 3, 2, 1) if (n_blocks - ramp) % n == 0)

    def full_blocks(i, carry):
        for jj in range(per_trip):
            j = ramp + i * per_trip + jj
            block(LEFT_CHUNKS, pl.multiple_of(j * Q_BLOCK, Q_BLOCK),
                  pl.multiple_of((j - ramp) * Q_BLOCK, Q_BLOCK), j - ramp)
        return carry

    lax.fori_loop(0, max(n_blocks - ramp, 0) // per_trip, full_blocks, 0)


def _rel_bias_matrix(table, rows, cols, dist00):
    n = rows + cols - 1
    dist = np.arange(n) - (cols - 1) + dist00
    g = table[:, np.clip(dist, -MAX_REL, MAX_REL) + MAX_REL].astype(F32)
    period = rows + cols
    g = jnp.pad(g, ((0, 0), (0, period - n)))
    flat = jnp.tile(g, (1, rows + 1))[:, :rows * (period + 1)]
    hankel = flat.reshape(-1, rows, period + 1)[:, :, :cols]
    return hankel[:, :, ::-1]


def _prompt_bias_tiles(table):
    far = table[:, 2 * MAX_REL].astype(F32)[:, None, None]

    def chunk_pair(gap):
        kind = _gap_kind(gap)
        if kind == "near":
            return jnp.swapaxes(_rel_bias_matrix(table, CHUNK, CHUNK, gap * CHUNK), 1, 2) - far
        fill = NEG_BIG if kind == "masked" else 0.0
        return jnp.full((table.shape[0], CHUNK, CHUNK), fill, F32)

    tiles = [jnp.concatenate([chunk_pair(g), chunk_pair(g + 1)], axis=2) for g in TILE_GAPS]
    return jnp.stack(tiles, axis=1)


def _attend_prompt(q, k, v_t, table):
    b, t, w = q.shape
    n_heads = w // HEAD_DIM
    assert t % Q_BLOCK == 0 and PAD_ROWS % Q_BLOCK == 0 and LANES // CHUNK == 2 and Q_BLOCK % LANES == 0
    n_blocks = t // Q_BLOCK
    k_rows = PAD_ROWS + Q_BLOCK
    tiles = _prompt_bias_tiles(table)
    seq = pl.BlockSpec((1, t, w), lambda i: (i, 0, 0))
    return pl.pallas_call(
        functools.partial(_attn_prompt_kernel, n_heads=n_heads, n_blocks=n_blocks),
        grid=(b,),
        in_specs=[seq, seq, pl.BlockSpec((n_blocks, w, KEY_BLOCK), lambda i: (i, 0, 0)),
                  _const_spec(tiles.shape)],
        out_specs=seq,
        out_shape=jax.ShapeDtypeStruct((b, t, w), F32),
        scratch_shapes=[pltpu.VMEM((2, k_rows, Q_BLOCK), F32), pltpu.VMEM((2, k_rows, Q_BLOCK), BF16),
                        pltpu.VMEM((w, Q_BLOCK), F32)],
        compiler_params=_params(("arbitrary",)),
        name="attn_prompt",
    )(q, k, v_t, tiles)


def _attn_sample_kernel(q_ref, k_ref, v_ref, ckt_ref, cvt_ref, bc_ref, bn_ref, o_ref, *, n_heads):
    for i in range(q_ref.shape[0]):
        for h in range(n_heads):
            hs = slice(h * HEAD_DIM, (h + 1) * HEAD_DIM)
            qh = q_ref[i, :, hs]
            s_c = jnp.dot(qh, ckt_ref[i, hs, :].astype(BF16), preferred_element_type=F32) + bc_ref[h]
            s_n = lax.dot_general(qh, k_ref[i, :, hs], NT_DIMS, preferred_element_type=F32) + bn_ref[h]
            m = jnp.maximum(jnp.max(s_c, axis=-1, keepdims=True), jnp.max(s_n, axis=-1, keepdims=True))
            p_c = jnp.exp(s_c - m)
            p_n = jnp.exp(s_n - m)
            l = jnp.sum(p_c, axis=-1, keepdims=True) + jnp.sum(p_n, axis=-1, keepdims=True)
            o = lax.dot_general(p_c.astype(BF16), cvt_ref[i, hs, :].astype(BF16), NT_DIMS,
                                preferred_element_type=F32)
            o = o + jnp.dot(p_n.astype(BF16), v_ref[i, :, hs], preferred_element_type=F32)
            o_ref[i, :, hs] = o / l


def _attend_sample(q, k, v, k_cache, v_cache, table):
    b, t, w = q.shape
    n_heads = w // HEAD_DIM
    r = k_cache.shape[1]
    bias_c = _rel_bias_matrix(table, t, r, r)
    bias_n = _rel_bias_matrix(table, t, t, 0)
    channel_major = lambda c: jnp.transpose(c, (0, 2, 3, 1)).reshape(b, w, r)
    per_step = math.gcd(b, SAMPLE_SEQS_PER_STEP)
    new = pl.BlockSpec((per_step, t, w), lambda i: (i, 0, 0))
    old = pl.BlockSpec((per_step, w, r), lambda i: (i, 0, 0))
    return pl.pallas_call(
        functools.partial(_attn_sample_kernel, n_heads=n_heads),
        grid=(b // per_step,),
        in_specs=[new, new, new, old, old, _const_spec(bias_c.shape), _const_spec(bias_n.shape)],
        out_specs=new,
        out_shape=jax.ShapeDtypeStruct((b, t, w), F32),
        compiler_params=_params(("arbitrary",)),
        name="attn_sample",
    )(q, k, v, channel_major(k_cache), channel_major(v_cache), bias_c, bias_n)


def _out_projection(y_lru_planes, x, ya, gl_ref, ga_ref, wo_ref, gf_ref, token=None):
    lru_w = sum(y.shape[1] for y in y_lru_planes)
    attn_w = ya.shape[1]
    gl, ga = gl_ref[...], ga_ref[...]
    if token is not None:
        gl = gl + jnp.tile(token, (1, lru_w // LANES))
        ga = ga + jnp.tile(token, (1, attn_w // LANES))
    sum_sq = sum(jnp.sum(y * y, axis=-1, keepdims=True) for y in y_lru_planes)
    inv_rms = lax.rsqrt(sum_sq * (1.0 / lru_w) + EPS)
    mix = jnp.dot(_rms(ya, ga).astype(BF16), wo_ref[lru_w:, :], preferred_element_type=F32)
    c0 = 0
    for y in y_lru_planes:
        cols = slice(c0, c0 + y.shape[1])
        mix = mix + jnp.dot((y * inv_rms * gl[:, cols]).astype(BF16), wo_ref[cols, :],
                            preferred_element_type=F32)
        c0 += y.shape[1]
    x1 = x + mix
    return x1, _rms(x1, gf_ref[...]).astype(BF16)


def _ffn_column_pipeline(xn_s, acc_s, buf, hmid_s, wg_ref, wu_ref, wd_ref, side_work=()):
    n_tiles = wg_ref.shape[1] // FFN_TILE

    def gate_up(c):
        cols = slice(c * FFN_TILE, (c + 1) * FFN_TILE)
        xn = xn_s[buf]
        return (jnp.dot(xn, wg_ref[:, cols], preferred_element_type=F32),
                jnp.dot(xn, wu_ref[:, cols], preferred_element_type=F32))

    side_work = list(side_work)
    pending = gate_up(0)
    for c in range(n_tiles):
        upcoming = gate_up(c + 1) if c + 1 < n_tiles else None
        g, up = pending
        hmid = _silu_from_half(g) * up
        hmid_s[:, c * FFN_TILE:(c + 1) * FFN_TILE] = hmid.astype(BF16)
        if (c + 1) % DOWN_GROUP == 0 or c + 1 == n_tiles:
            ks = slice((c // DOWN_GROUP) * DOWN_GROUP * FFN_TILE, (c + 1) * FFN_TILE)
            acc_s[buf] = acc_s[buf] + jnp.dot(hmid_s[:, ks], wd_ref[ks, :], preferred_element_type=F32)
        token = _zero_token(hmid[-SUBLANES:, -LANES:])
        for _ in range(-(-len(side_work) // (n_tiles - c))):
            side_work.pop(0)(token)
        pending = upcoming
    assert not side_work


def _out_ffn_kernel(x_ref, yl_ref, ya_ref, gl_ref, ga_ref, wo_ref, gf_ref, wg_ref, wu_ref, wd_ref,
                    gfin_ref, o_ref, xn_s, hmid_s, acc_s, *, final_norm):
    x1, xn = _out_projection([yl_ref[...]], x_ref[...], ya_ref[...], gl_ref, ga_ref, wo_ref, gf_ref)
    acc_s[0] = x1
    xn_s[0] = xn
    _ffn_column_pipeline(xn_s, acc_s, 0, hmid_s, wg_ref, wu_ref, wd_ref)
    o_ref[...] = _rms(acc_s[0], gfin_ref[...]) if final_norm else acc_s[0]


def _ffn_scratch(tm, d, d_ff, bufs):
    assert d_ff % FFN_TILE == 0
    return [pltpu.VMEM((bufs, tm, d), BF16), pltpu.VMEM((tm, d_ff), BF16), pltpu.VMEM((bufs, tm, d), F32)]


def _out_ffn(x2d, y_lru, y_att, gl, ga, wo, gf, wg, wu, wd, gfin, final_norm):
    rows, d = x2d.shape
    tm = min(ROW_TILE, rows)
    assert rows % tm == 0
    lru_w, attn_w, d_ff = y_lru.shape[1], y_att.shape[1], wg.shape[1]
    row_spec = lambda w: pl.BlockSpec((tm, w), lambda i: (i, 0))
    return pl.pallas_call(
        functools.partial(_out_ffn_kernel, final_norm=final_norm),
        grid=(rows // tm,),
        in_specs=[row_spec(d), row_spec(lru_w), row_spec(attn_w), _const_spec((1, lru_w)),
                  _const_spec((1, attn_w)), _const_spec(wo.shape), _const_spec((1, d)),
                  _const_spec(wg.shape), _const_spec(wu.shape), _const_spec(wd.shape), _const_spec((1, d))],
        out_specs=row_spec(d),
        out_shape=jax.ShapeDtypeStruct((rows, d), F32),
        scratch_shapes=_ffn_scratch(tm, d, d_ff, 1),
        compiler_params=_params(("arbitrary",)),
        name="out_ffn",
    )(x2d, y_lru, y_att, gl, ga, wo, gf, wg, wu, wd, gfin)


def _lru_out_ffn_kernel(x_ref, ya_ref, u_ref, gate_ref, cw_ref, cb_ref, wbd_ref, ba_ref, bx_ref, lam_ref,
                        gl_ref, ga_ref, wo_ref, gf_ref, wg_ref, wu_ref, wd_ref, gfin_ref,
                        o_ref, convn_ref, hlast_ref,
                        ext_s, uc_s, ucb_s, pre_s, y_s, hc_s, xn_s, hmid_s, acc_s,
                        *, tiles_per_seq, n_row_tiles, final_norm):
    s = pl.program_id(0)
    tm, lru_w = u_ref.shape
    hist = CONV_WIDTH - 1
    base = SUBLANES
    n_halves, half = wbd_ref.shape[0], wbd_ref.shape[1]
    slot = s % 2

    @pl.when(s == 0)
    def _():
        y_s[1] = jnp.zeros((n_halves, tm, half), F32)
        ext_s[base - hist:base, :] = jnp.zeros((hist, lru_w), F32)
        hc_s[...] = jnp.zeros((1, lru_w), F32)

    seq_start = (s % tiles_per_seq) == 0
    ext_s[base - hist:base, :] = jnp.where(seq_start, 0.0, ext_s[base - hist:base, :])
    ext_s[base:base + tm, :] = u_ref[...]
    h_start = jnp.where(seq_start, 0.0, hc_s[...])
    carry = [h_start[:, hf * half:(hf + 1) * half] for hf in range(n_halves)]
    half_c_lsl = (0.5 * LRU_C) * _log_sigmoid(lam_ref[...])
    masks = _scan_masks(half)

    def conv(pb, token):
        rows = slice(pb * LRU_PIECE_ROWS, (pb + 1) * LRU_PIECE_ROWS)
        bias = cb_ref[...] + jnp.tile(token, (1, lru_w // LANES))
        uc = _conv_rows(ext_s, base + pb * LRU_PIECE_ROWS, LRU_PIECE_ROWS, cw_ref[...], bias)
        for hf in range(n_halves):
            plane = uc[:, hf * half:(hf + 1) * half]
            uc_s[hf, rows, :] = plane
            ucb_s[hf, rows, :] = plane.astype(BF16)

    def gate_matmul(gb, token):
        del token
        rows = slice(gb * LRU_GATE_ROWS, (gb + 1) * LRU_GATE_ROWS)
        for hf in range(n_halves):
            pre = jnp.dot(ucb_s[hf, rows, :], wbd_ref[hf], preferred_element_type=F32)
            pre_s[2 * hf, rows, :] = pre[:, :half]
            pre_s[2 * hf + 1, rows, :] = pre[:, half:]

    def recur(pb, hf, token):
        rows = slice(pb * LRU_PIECE_ROWS, (pb + 1) * LRU_PIECE_ROWS)
        cols = slice(hf * half, (hf + 1) * half)
        zero = jnp.tile(token, (1, half // LANES))
        a, b = _gate_terms(pre_s[2 * hf, rows, :], pre_s[2 * hf + 1, rows, :],
                           uc_s[hf, rows, :], ba_ref[:, cols] + zero, bx_ref[:, cols] + zero,
                           half_c_lsl[:, cols])
        h_prev = carry[hf]
        hs = []
        for g in range(LRU_PIECE_ROWS // SUBLANES):
            grp = slice(g * SUBLANES, (g + 1) * SUBLANES)
            h = _scan_group(a[grp], b[grp], h_prev, masks)
            h_prev = h[SUBLANES - 1:SUBLANES, :]
            hs.append(h)
        carry[hf] = h_prev
        y_s[slot, hf, rows, :] = _gelu_tanh_from_half(gate_ref[hf, rows, :]) * jnp.concatenate(hs, axis=0)

    per = LRU_GATE_ROWS // LRU_PIECE_ROWS
    n_gate_blocks = tm // LRU_GATE_ROWS
    side_work = [functools.partial(conv, pb) for pb in range(per)]
    for gb in range(n_gate_blocks):
        side_work.append(functools.partial(gate_matmul, gb))
        for i in range(per):
            if gb + 1 < n_gate_blocks:
                side_work.append(functools.partial(conv, (gb + 1) * per + i))
            side_work += [functools.partial(recur, gb * per + i, hf) for hf in range(n_halves)]

    x1, xn = _out_projection([y_s[1 - slot, hf] for hf in range(n_halves)], x_ref[...], ya_ref[...],
                             gl_ref, ga_ref, wo_ref, gf_ref)
    acc_s[0] = x1
    xn_s[0] = xn
    _ffn_column_pipeline(xn_s, acc_s, 0, hmid_s, wg_ref, wu_ref, wd_ref, side_work)
    o_ref[...] = _rms(acc_s[0], gfin_ref[...]) if final_norm else acc_s[0]

    tail = ext_s[base + tm - hist:base + tm, :]
    ext_s[base - hist:base, :] = tail
    h_end = jnp.concatenate(carry, axis=1)
    hc_s[...] = h_end

    @pl.when(s < n_row_tiles)
    def _():
        convn_ref[0] = tail
        hlast_ref[0] = h_end


def _lru_out_ffn(x2d, y_att, u, gate, seq_len, p, gfin, final_norm):
    rows, d = x2d.shape
    tm = ROW_TILE
    lru_w, attn_w = u.shape[1], y_att.shape[1]
    d_ff = p["w_gate"].shape[1]
    n_halves, half = p["wbd"].shape[0], p["wbd"].shape[1]
    assert rows % tm == 0 and seq_len % tm == 0 and gate.shape == (n_halves, rows, half)
    assert tm % LRU_GATE_ROWS == 0 and LRU_GATE_ROWS % LRU_PIECE_ROWS == 0
    n = rows // tm
    tiles_per_seq = seq_len // tm
    hist = CONV_WIDTH - 1
    tile = lambda s, lag: jnp.clip(s - lag, 0, n - 1)
    rows_of = lambda w, lag: pl.BlockSpec((tm, w), lambda s: (tile(s, lag), 0))
    per_seq = lambda r: pl.BlockSpec((1, r, lru_w), lambda s: (tile(s, 0) // tiles_per_seq, 0, 0))
    consts = [p["conv_w"], p["conv_b"], p["wbd"], p["lru_ba"], p["lru_bx"], p["lru_lambda"],
              p["norm_lru_out"], p["norm_attn_out"], p["w_out"], p["norm_ffn"], p["w_gate"], p["w_up"],
              p["w_down"], gfin]
    return pl.pallas_call(
        functools.partial(_lru_out_ffn_kernel, tiles_per_seq=tiles_per_seq, n_row_tiles=n, final_norm=final_norm),
        grid=(n + 1,),
        in_specs=[rows_of(d, 1), rows_of(attn_w, 1), rows_of(lru_w, 0),
                  pl.BlockSpec((n_halves, tm, half), lambda s: (0, tile(s, 0), 0))]
                 + [_const_spec(c.shape) for c in consts],
        out_specs=[rows_of(d, 1), per_seq(hist), per_seq(1)],
        out_shape=[jax.ShapeDtypeStruct((rows, d), F32),
                   jax.ShapeDtypeStruct((rows // seq_len, hist, lru_w), F32),
                   jax.ShapeDtypeStruct((rows // seq_len, 1, lru_w), F32)],
        scratch_shapes=[pltpu.VMEM((SUBLANES + tm, lru_w), F32), pltpu.VMEM((n_halves, tm, half), F32),
                        pltpu.VMEM((n_halves, tm, half), BF16), pltpu.VMEM((2 * n_halves, tm, half), F32),
                        pltpu.VMEM((2, n_halves, tm, half), F32), pltpu.VMEM((1, lru_w), F32)]
                       + _ffn_scratch(tm, d, d_ff, 1),
        compiler_params=_params(("arbitrary",)),
        name="lru_out_ffn",
    )(x2d, y_att, u, gate, *consts)


def _block_diag_gates(wa, wx):
    n, c, _ = wa.shape
    per = MXU_DIM // c
    assert n % per == 0
    eye = jnp.eye(per, dtype=wa.dtype)

    def bd(w):
        w = w.reshape(n // per, per, c, c)
        return (eye[None, :, None, :, None] * w[:, :, :, None, :]).reshape(n // per, per * c, per * c)

    return jnp.concatenate([bd(wa), bd(wx)], axis=-1).astype(BF16)


def _layer(x, conv0, h0, prompt, attend, final_gain, p):
    b, t, d = x.shape
    lru_w = p["conv_w"].shape[1]
    attn_w = p["norm_attn_out"].shape[1]
    n_heads = attn_w // HEAD_DIM
    rows = b * t
    x2d = x.reshape(rows, d)
    u, gate, q, k, v, k_tail, v_tail = _in_projection(x2d, p["norm_mix"], p["w_in"], lru_w, attn_w, t, prompt)
    y_att = attend(q.reshape(b, t, attn_w), k.reshape(b, t, attn_w), v if prompt else v.reshape(b, t, attn_w))
    y_att = y_att.reshape(rows, attn_w)
    final_norm = final_gain is not None
    gfin = final_gain if final_norm else p["norm_ffn"]
    if prompt:
        y, conv_new, h_last = _lru_out_ffn(x2d, y_att, u, gate, t, p, gfin, final_norm)
    else:
        y_lru, conv_new, h_last = _lru_group(u.reshape(b, t, lru_w), gate.reshape(b, t, lru_w), conv0, h0,
                                             p["conv_w"], p["conv_b"], p["wbd"], p["lru_ba"], p["lru_bx"],
                                             p["lru_lambda"])
        y = _out_ffn(x2d, y_lru.reshape(rows, lru_w), y_att, p["norm_lru_out"], p["norm_attn_out"],
                     p["w_out"], p["norm_ffn"], p["w_gate"], p["w_up"], p["w_down"], gfin, final_norm)
    if prompt:
        keep = k_tail.shape[2]
        k_tail = k_tail.reshape(b, n_heads, HEAD_DIM, keep).transpose(0, 3, 1, 2)
        v_tail = v_tail.reshape(b, n_heads, HEAD_DIM, keep).transpose(0, 3, 1, 2)
    else:
        k_tail = k_tail.reshape(b, t, n_heads, HEAD_DIM)
        v_tail = v_tail.reshape(b, t, n_heads, HEAD_DIM)
    return y.reshape(b, t, d), conv_new, h_last.reshape(b, lru_w), k_tail, v_tail


def kernel(x_prompt, x_sample, state_conv, state_lru, cache_k, cache_v, norm_mix, w_in, conv_w, conv_b, lru_wa, lru_ba, lru_wx, lru_bx, lru_lambda, rel_bias, norm_lru_out, norm_attn_out, w_out, norm_ffn, w_gate, w_up, w_down, norm_final):
    depth = w_in.shape[0]
    xp, xs = x_prompt, x_sample
    row = lambda a: a.reshape(1, -1)
    outs = [[] for _ in range(8)]
    for l in range(depth):
        lru_w, attn_w = conv_w.shape[-1], norm_attn_out.shape[-1]
        p = dict(
            norm_mix=row(norm_mix[l]), w_in=_in_weights(w_in[l], lru_w, attn_w), conv_w=0.5 * conv_w[l],
            conv_b=row(0.5 * conv_b[l]), wbd=_block_diag_gates(lru_wa[l], lru_wx[l]),
            lru_ba=row(0.5 * lru_ba[l]), lru_bx=row(0.5 * lru_bx[l]),
            lru_lambda=row(lru_lambda[l]), norm_lru_out=row(norm_lru_out[l]),
            norm_attn_out=row(norm_attn_out[l]), w_out=w_out[l].astype(BF16), norm_ffn=row(norm_ffn[l]),
            w_gate=(0.5 * w_gate[l]).astype(BF16), w_up=w_up[l].astype(BF16), w_down=w_down[l].astype(BF16))
        final_gain = row(norm_final) if l == depth - 1 else None
        att_p = functools.partial(_attend_prompt, table=rel_bias[l])
        xp, cp, hp, kp, vp = _layer(xp, None, None, True, att_p, final_gain, p)
        att_s = functools.partial(_attend_sample, k_cache=cache_k[l], v_cache=cache_v[l], table=rel_bias[l])
        xs, cs, hs, ksn, vsn = _layer(xs, state_conv[l], state_lru[l], False, att_s, final_gain, p)
        for lst, val in zip(outs, (cp, hp, kp, vp, cs, hs, ksn, vsn)):
            lst.append(val)
    return (xp, xs) + tuple(jnp.stack(o) for o in outs)
```
